```python
import jax, jax.numpy as jnp
from jax import lax
import numpy as np

D_MODEL = 1024
BATCH = 8
SEQ = 2048
DEPTH = 2

CHUNK = 64
N_MEM = 256
HEAD_DIM = 64
A_HEADS = 6
A_WIDTH = A_HEADS * HEAD_DIM
A_DECAY_LORA = 32
A_ICLR_LORA = 32
A_VRES_LORA = 32
A_GATE_LORA = 64
A_GN_EPS = 64e-5
A_PROJ = 3 * A_WIDTH + A_DECAY_LORA + A_ICLR_LORA + A_GATE_LORA
B_HEADS = 6
B_WIDTH = B_HEADS * HEAD_DIM
B_PREV_CHUNKS = 8
B_BAND = (B_PREV_CHUNKS + 1) * CHUNK
REL_MAX = 256
N_REL = CHUNK - 1 + REL_MAX + 1
C_GROUPS = 4
C_GROUP_DIM = 64
C_WIDTH = C_GROUPS * C_GROUP_DIM
POOL_WINDOWS = (2, 4, 8, 16)
MIX_WIDTH = A_WIDTH + B_WIDTH + C_WIDTH
IN_PROJ = A_PROJ + 3 * B_WIDTH + C_WIDTH
X_HEADS = 4
X_HEAD_DIM = D_MODEL // X_HEADS
D_FF = 2816
RMS_EPS = 1e-6
NEG_INF = -1e30

kernel_name = "hybrid_rwkv7_chunkattn_pool_macaron"


def rms_norm(x, g):
    xf = x.astype(jnp.float32)
    y = xf * lax.rsqrt(jnp.mean(xf * xf, axis=-1, keepdims=True) + RMS_EPS)
    return (y * g.astype(jnp.float32)).astype(x.dtype)


def swiglu(x, w_in, w_out):
    gate, up = jnp.split(x @ w_in, 2, axis=-1)
    return (jax.nn.silu(gate) * up) @ w_out


def token_shift(p):
    return jnp.pad(p, ((0, 0), (1, 0), (0, 0)))[:, :-1]


def rwkv7_mixer(p, v_first, mu, w0, w_up, a0, a_up, g_up, k_k, k_a, r_k, gn_g, gn_b, vres):
    B_, T, _ = p.shape
    f32 = jnp.float32
    p = p + mu * (token_shift(p) - p)
    r, k, v, wd, ad, gd = jnp.split(
        p, [A_WIDTH, 2 * A_WIDTH, 3 * A_WIDTH, 3 * A_WIDTH + A_DECAY_LORA,
            3 * A_WIDTH + A_DECAY_LORA + A_ICLR_LORA], axis=-1)
    w = -jax.nn.softplus(-(w0 + jnp.tanh(wd) @ w_up)) - 0.5
    decay = jnp.exp(-jnp.exp(w.astype(f32)))
    a = jax.nn.sigmoid(a0 + ad @ a_up)
    g = jax.nn.sigmoid(gd) @ g_up
    if vres is None:
        v_first = v
    else:
        v0, v_down, v_up = vres
        v = v + (v_first - v) * jax.nn.sigmoid(v0 + (v @ v_down) @ v_up)
    kk = (k * k_k).reshape(B_, T, A_HEADS, HEAD_DIM).astype(f32)
    kk = kk / jnp.maximum(jnp.sqrt(jnp.sum(kk * kk, axis=-1, keepdims=True)), 1e-12)
    k = k * (1.0 + (a - 1.0) * k_a)
    hs = lambda t: t.reshape(B_, T, A_HEADS, HEAD_DIM).astype(f32)
    r_h, k_h, v_h, w_h, a_h = hs(r), hs(k), hs(v), hs(decay), hs(a)

    def step(S, inp):
        r_t, k_t, v_t, w_t, kk_t, a_t = inp
        sa = jnp.einsum('bhvk,bhk->bhv', S, -kk_t)
        S = (S * w_t[:, :, None, :] + sa[..., None] * (kk_t * a_t)[:, :, None, :]
             + v_t[..., None] * k_t[:, :, None, :])
        return S, jnp.einsum('bhvk,bhk->bhv', S, r_t)

    xs = tuple(jnp.moveaxis(t, 1, 0) for t in (r_h, k_h, v_h, w_h, kk, a_h))
    S0 = jnp.zeros((B_, A_HEADS, HEAD_DIM, HEAD_DIM), f32)
    _, y = lax.scan(step, S0, xs)
    y = jnp.moveaxis(y, 0, 1)
    mean = jnp.mean(y, axis=-1, keepdims=True)
    var = jnp.mean(jnp.square(y - mean), axis=-1, keepdims=True)
    y = ((y - mean) * lax.rsqrt(var + A_GN_EPS)).reshape(B_, T, A_WIDTH) * gn_g + gn_b
    bonus = (jnp.sum(r_h * k_h * r_k, axis=-1, keepdims=True) * v_h).reshape(B_, T, A_WIDTH)
    return ((y + bonus) * g).astype(p.dtype), v_first


def chunk_attention(q, k, v, q_gain, k_gain, rel_bias):
    B_, T, _ = q.shape
    NC = T // CHUNK
    q = rms_norm(q.reshape(B_, T, B_HEADS, HEAD_DIM), q_gain)
    k = rms_norm(k.reshape(B_, T, B_HEADS, HEAD_DIM), k_gain)
    v = v.reshape(B_, T, B_HEADS, HEAD_DIM)
    qc = q.reshape(B_, NC, CHUNK, B_HEADS, HEAD_DIM)
    pad = ((0, 0), (B_PREV_CHUNKS * CHUNK, 0), (0, 0), (0, 0))
    kc = jnp.pad(k, pad).reshape(B_, NC + B_PREV_CHUNKS, CHUNK, B_HEADS, HEAD_DIM)
    vc = jnp.pad(v, pad).reshape(B_, NC + B_PREV_CHUNKS, CHUNK, B_HEADS, HEAD_DIM)
    band_idx = np.arange(NC)[:, None] + np.arange(B_PREV_CHUNKS + 1)[None, :]
    kb = kc[:, band_idx].reshape(B_, NC, B_BAND, B_HEADS, HEAD_DIM)
    vb = vc[:, band_idx].reshape(B_, NC, B_BAND, B_HEADS, HEAD_DIM)
    s = jnp.einsum('bnihd,bnjhd->bhnij', qc, kb).astype(jnp.float32) * (HEAD_DIM ** -0.5)
    dist = B_PREV_CHUNKS * CHUNK + np.arange(CHUNK)[:, None] - np.arange(B_BAND)[None, :]
    rel_idx = np.clip(dist, -(CHUNK - 1), REL_MAX) + (CHUNK - 1)
    bias = rel_bias.astype(jnp.float32)[:, rel_idx]
    valid = (np.arange(NC)[:, None] + (np.arange(B_BAND) // CHUNK)[None, :]) >= B_PREV_CHUNKS
    s = jnp.where(valid[None, None, :, None, :], s + bias[:, None], NEG_INF)
    prob = jax.nn.softmax(s, axis=-1).astype(v.dtype)
    o = jnp.einsum('bhnij,bnjhd->bnihd', prob, vb)
    return o.reshape(B_, T, B_WIDTH)


def multiscale_pool(u, pool_w, pool_scale):
    B_, T, _ = u.shape
    uf = u.astype(jnp.float32).reshape(B_, T, C_GROUPS, C_GROUP_DIM)
    cs = jnp.cumsum(uf, axis=1)
    t1 = jnp.arange(1, T + 1, dtype=jnp.float32)
    outs = []
    for gi, win in enumerate(POOL_WINDOWS):
        c = cs[:, :, gi]
        prev = jnp.pad(c, ((0, 0), (win, 0), (0, 0)))[:, :T]
        cnt = jnp.minimum(t1, float(win))[None, :, None]
        outs.append((c - prev) / cnt - uf[:, :, gi])
    pooled = jnp.stack(outs, axis=2)
    y = jnp.einsum('btgc,gcd->btgd', pooled, pool_w.astype(jnp.float32)).reshape(B_, T, C_WIDTH)
    return (y * pool_scale).astype(u.dtype)


def memory_cross_attention(h, mem_n, wq, wkv, wo, q_gain, k_gain):
    B_, T, _ = h.shape
    M = mem_n.shape[1]
    q = rms_norm((h @ wq).reshape(B_, T, X_HEADS, X_HEAD_DIM), q_gain)
    k, v = jnp.split(mem_n @ wkv, 2, axis=-1)
    k = rms_norm(k.reshape(B_, M, X_HEADS, X_HEAD_DIM), k_gain)
    v = v.reshape(B_, M, X_HEADS, X_HEAD_DIM)
    s = jnp.einsum('bthd,bmhd->bhtm', q, k).astype(jnp.float32) * (X_HEAD_DIM ** -0.5)
    prob = jax.nn.softmax(s, axis=-1).astype(v.dtype)
    o = jnp.einsum('bhtm,bmhd->bthd', prob, v).reshape(B_, T, D_MODEL)
    return o @ wo


def setup_inputs(seed: int = 0) -> dict:
    key = jax.random.key(seed)
    ks = iter(jax.random.split(key, 64))
    f32 = jnp.float32
    L, D = DEPTH, D_MODEL

    def nrm(shape, scale):
        return scale * jax.random.normal(next(ks), shape, f32)

    def gain(shape, base=1.0, noise=0.1):
        return base + noise * jax.random.normal(next(ks), shape, f32)

    return {
        "x": nrm((BATCH, SEQ, D), 1.0),
        "mem": nrm((BATCH, N_MEM, D), 1.0),
        "norm_ffn1": gain((L, D)),
        "ffn1_wi": nrm((L, D, 2 * D_FF), D ** -0.5),
        "ffn1_wo": nrm((L, D_FF, D), D_FF ** -0.5),
        "norm_mix": gain((L, D)),
        "w_in": nrm((L, D, IN_PROJ), D ** -0.5),
        "w_out": nrm((L, MIX_WIDTH, D), MIX_WIDTH ** -0.5),
        "a_mu": jax.random.uniform(next(ks), (L, A_PROJ), f32),
        "a_w0": nrm((L, A_WIDTH), 0.5),
        "a_w_up": nrm((L, A_DECAY_LORA, A_WIDTH), 0.5 * A_DECAY_LORA ** -0.5),
        "a_a0": nrm((L, A_WIDTH), 0.5),
        "a_a_up": nrm((L, A_ICLR_LORA, A_WIDTH), 0.5 * A_ICLR_LORA ** -0.5),
        "a_g_up": nrm((L, A_GATE_LORA, A_WIDTH), A_GATE_LORA ** -0.5),
        "a_k_k": gain((L, A_WIDTH), 0.85, 0.05),
        "a_k_a": gain((L, A_WIDTH), 1.0, 0.05),
        "a_r_k": nrm((L, A_HEADS, HEAD_DIM), 0.1),
        "a_gn_g": gain((L, A_WIDTH)),
        "a_gn_b": nrm((L, A_WIDTH), 0.02),
        "a_v0": nrm((L - 1, A_WIDTH), 0.5),
        "a_v_down": nrm((L - 1, A_WIDTH, A_VRES_LORA), A_WIDTH ** -0.5),
        "a_v_up": nrm((L - 1, A_VRES_LORA, A_WIDTH), 0.5 * A_VRES_LORA ** -0.5),
        "b_q_gain": gain((L, HEAD_DIM)),
        "b_k_gain": gain((L, HEAD_DIM)),
        "b_rel_bias": nrm((L, B_HEADS, N_REL), 0.5),
        "c_pool_w": nrm((L, C_GROUPS, C_GROUP_DIM, C_GROUP_DIM), C_GROUP_DIM ** -0.5),
        "c_pool_scale": gain((L, C_WIDTH)),
        "norm_cross": gain((L, D)),
        "norm_mem": gain((L, D)),
        "x_wq": nrm((L, D, D), D ** -0.5),
        "x_wkv": nrm((L, D, 2 * D), D ** -0.5),
        "x_wo": nrm((L, D, D), D ** -0.5),
        "x_q_gain": gain((L, X_HEAD_DIM)),
        "x_k_gain": gain((L, X_HEAD_DIM)),
        "norm_ffn2": gain((L, D)),
        "ffn2_wi": nrm((L, D, 2 * D_FF), D ** -0.5),
        "ffn2_wo": nrm((L, D_FF, D), D_FF ** -0.5),
    }


def reference(x, mem, norm_ffn1, ffn1_wi, ffn1_wo, norm_mix, w_in, w_out,
              a_mu, a_w0, a_w_up, a_a0, a_a_up, a_g_up, a_k_k, a_k_a, a_r_k, a_gn_g, a_gn_b,
              a_v0, a_v_down, a_v_up, b_q_gain, b_k_gain, b_rel_bias, c_pool_w, c_pool_scale,
              norm_cross, norm_mem, x_wq, x_wkv, x_wo, x_q_gain, x_k_gain,
              norm_ffn2, ffn2_wi, ffn2_wo):
    split_pts = [A_PROJ, A_PROJ + B_WIDTH, A_PROJ + 2 * B_WIDTH, A_PROJ + 3 * B_WIDTH]
    v_first = None
    for l in range(DEPTH):
        x = x + 0.5 * swiglu(rms_norm(x, norm_ffn1[l]), ffn1_wi[l], ffn1_wo[l])
        h = rms_norm(x, norm_mix[l])
        p = h @ w_in[l]
        p_a, p_q, p_k, p_v, p_c = jnp.split(p, split_pts, axis=-1)
        vres = None if l == 0 else (a_v0[l - 1], a_v_down[l - 1], a_v_up[l - 1])
        y_a, v_first = rwkv7_mixer(p_a, v_first, a_mu[l], a_w0[l], a_w_up[l], a_a0[l], a_a_up[l],
                                   a_g_up[l], a_k_k[l], a_k_a[l], a_r_k[l], a_gn_g[l], a_gn_b[l], vres)
        y_b = chunk_attention(p_q, p_k, p_v, b_q_gain[l], b_k_gain[l], b_rel_bias[l])
        y_c = multiscale_pool(p_c, c_pool_w[l], c_pool_scale[l])
        x = x + jnp.concatenate([y_a, y_b, y_c], axis=-1) @ w_out[l]
        x = x + memory_cross_attention(rms_norm(x, norm_cross[l]), rms_norm(mem, norm_mem[l]),
                                       x_wq[l], x_wkv[l], x_wo[l], x_q_gain[l], x_k_gain[l])
        x = x + 0.5 * swiglu(rms_norm(x, norm_ffn2[l]), ffn2_wi[l], ffn2_wo[l])
    return x
```

```python
import functools

import numpy as np
import jax
import jax.numpy as jnp
from jax import lax
from jax.experimental import pallas as pl
from jax.experimental.pallas import tpu as pltpu

F32 = jnp.float32
BF16 = jnp.bfloat16
HI = lax.Precision.HIGHEST

LANES = 128
HEAD = 64
PAIR = 2 * HEAD
CHUNK = 64
PREV_CHUNKS = 8
BAND = (PREV_CHUNKS + 1) * CHUNK
REL_MAX = 256
POOL_WINDOWS = (2, 4, 8, 16)
LORA_W, LORA_A, LORA_G, LORA_V = 32, 32, 64, 32
RMS_EPS = 1e-6
GN_EPS = 64e-5
NEG_INF = -1e30
VMEM_LIMIT = 56 * 1024 * 1024


def _dot(a, b, precision=None):
    return jnp.dot(a, b, preferred_element_type=F32, precision=precision)


def _dot_nt(a, b):
    return lax.dot_general(a, b, (((1,), (1,)), ((), ())), preferred_element_type=F32)


def _dot_tn(a, b):
    return lax.dot_general(a, b, (((0,), (0,)), ((), ())), preferred_element_type=F32)


def _rms(x, g):
    return x * lax.rsqrt(jnp.mean(x * x, axis=-1, keepdims=True) + RMS_EPS) * g


def _sigmoid(x):
    return 1.0 / (1.0 + jnp.exp(-x))


def _params(*sem):
    return pltpu.CompilerParams(dimension_semantics=sem, vmem_limit_bytes=VMEM_LIMIT)


def _ffn_kernel(x_ref, g_ref, wg_ref, wu_ref, wo_ref, o_ref, h_ref, acc_ref):
    j = pl.program_id(1)

    @pl.when(j == 0)
    def _():
        h_ref[...] = _rms(x_ref[...], g_ref[...]).astype(BF16)
        acc_ref[...] = jnp.zeros_like(acc_ref)

    h = h_ref[...]
    gate = _dot(h, wg_ref[...])
    up = _dot(h, wu_ref[...])
    act = (gate * _sigmoid(gate) * up).astype(BF16)
    acc_ref[...] += _dot(act, wo_ref[...])

    @pl.when(j == pl.num_programs(1) - 1)
    def _():
        o_ref[...] = x_ref[...] + 0.5 * acc_ref[...]


def _ffn(x, g, wi, wo, *, tm, tf):
    n, d = x.shape
    dff = wo.shape[0]
    nf = dff // tf
    return pl.pallas_call(
        _ffn_kernel,
        grid=(n // tm, nf),
        in_specs=[
            pl.BlockSpec((tm, d), lambda i, j: (i, 0)),
            pl.BlockSpec((1, d), lambda i, j: (0, 0)),
            pl.BlockSpec((d, tf), lambda i, j: (0, j)),
            pl.BlockSpec((d, tf), lambda i, j: (0, nf + j)),
            pl.BlockSpec((tf, d), lambda i, j: (j, 0)),
        ],
        out_specs=pl.BlockSpec((tm, d), lambda i, j: (i, 0)),
        out_shape=jax.ShapeDtypeStruct((n, d), F32),
        scratch_shapes=[pltpu.VMEM((tm, d), BF16), pltpu.VMEM((tm, d), F32)],
        compiler_params=_params("parallel", "arbitrary"),
        name="ffn",
    )(x, g, wi, wi, wo)


def _inproj_kernel(x_ref, g_ref, w_ref, seg_ref, qg_ref, kg_ref,
                   pa_ref, q_ref, k_ref, v_ref, u_ref, *, a_proj, b_width):
    h = _rms(x_ref[...], g_ref[...]).astype(BF16)
    p = _dot(h, w_ref[...])
    pa_ref[...] = p[:, :a_proj]
    q = p[:, a_proj:a_proj + b_width]
    k = p[:, a_proj + b_width:a_proj + 2 * b_width]
    seg = seg_ref[...]
    qms = _dot(q * q, seg, HI) * (1.0 / HEAD)
    kms = _dot(k * k, seg, HI) * (1.0 / HEAD)
    q_ref[...] = (q * lax.rsqrt(qms + RMS_EPS) * qg_ref[...]).astype(BF16)
    k_ref[...] = (k * lax.rsqrt(kms + RMS_EPS) * kg_ref[...]).astype(BF16)
    v_ref[...] = p[:, a_proj + 2 * b_width:a_proj + 3 * b_width].astype(BF16)
    u_ref[...] = p[:, a_proj + 3 * b_width:]


def _inproj(x, g, w, seg, qg, kg, *, tm, a_proj, b_width, c_width):
    n, d = x.shape
    full = lambda a: pl.BlockSpec(a.shape, lambda i: (0,) * a.ndim)
    row = lambda w_: pl.BlockSpec((tm, w_), lambda i: (i, 0))
    return pl.pallas_call(
        functools.partial(_inproj_kernel, a_proj=a_proj, b_width=b_width),
        grid=(n // tm,),
        in_specs=[row(d), full(g), full(w), full(seg), full(qg), full(kg)],
        out_specs=[row(a_proj), row(b_width), row(b_width), row(b_width), row(c_width)],
        out_shape=[jax.ShapeDtypeStruct((n, a_proj), F32),
                   jax.ShapeDtypeStruct((n, b_width), BF16),
                   jax.ShapeDtypeStruct((n, b_width), BF16),
                   jax.ShapeDtypeStruct((n, b_width), BF16),
                   jax.ShapeDtypeStruct((n, c_width), F32)],
        compiler_params=_params("parallel"),
        name="in_proj",
    )(x, g, w, seg, qg, kg)


def _lane_lt_head(shape):
    return lax.broadcasted_iota(jnp.int32, shape, len(shape) - 1) < HEAD


def _blockdiag(x2):
    first = _lane_lt_head(x2.shape)
    zero = jnp.zeros_like(x2)
    return jnp.concatenate([jnp.where(first, x2, zero), jnp.where(first, zero, x2)], axis=0)


def _rwkv_chunk_maps(qa, qr, kb, kk, kbe, kke, v, wlast_row):
    c = CHUNK
    b16 = lambda t: t.astype(BF16)
    q2 = b16(jnp.concatenate([qa, qr], axis=0))
    kbd = b16(jnp.concatenate([_blockdiag(kb), _blockdiag(kk)], axis=0))
    s_all = _dot_nt(q2, kbd)
    t_idx = lax.broadcasted_iota(jnp.int32, (c, PAIR), 0)
    s_idx = lax.broadcasted_iota(jnp.int32, (c, PAIR), 1) % HEAD
    strict = t_idx > s_idx
    incl = t_idx >= s_idx
    zero = jnp.zeros((c, PAIR), F32)
    a_ab = jnp.where(strict, s_all[:c, :PAIR], zero)
    a_ak = jnp.where(strict, s_all[:c, PAIR:], zero)
    a_rb = jnp.where(incl, s_all[c:, :PAIR], zero)
    a_rk = jnp.where(incl, s_all[c:, PAIR:], zero)

    apow = a_ab
    tinv = jnp.where(t_idx == s_idx, 1.0, 0.0) + a_ab
    for _ in range(5):
        apow = _dot(b16(apow), b16(_blockdiag(apow)))
        tinv = tinv + _dot(b16(tinv), b16(_blockdiag(apow)))

    vbd = b16(_blockdiag(v))
    av = _dot(b16(jnp.concatenate([a_ak, a_rk], axis=0)), vbd)
    r1 = _dot(b16(tinv), b16(jnp.concatenate([_blockdiag(qa), _blockdiag(av[:c])], axis=1)))
    qa_p, u0 = r1[:, :PAIR], r1[:, PAIR:]
    r2 = _dot(b16(a_rb), b16(jnp.concatenate([_blockdiag(qa_p), _blockdiag(u0)], axis=1)))
    qr_p = qr + r2[:, :PAIR]
    y0 = r2[:, PAIR:] + av[c:]

    row = lax.broadcasted_iota(jnp.int32, (PAIR, PAIR), 0)
    col = lax.broadcasted_iota(jnp.int32, (PAIR, PAIR), 1)
    same_head = (row < HEAD) == (col < HEAD)
    zero2 = jnp.zeros((PAIR, PAIR), F32)
    m = jnp.where(same_head, _dot_tn(b16(qa_p), b16(kbe)), zero2)
    m = m + jnp.where(row == col, jnp.broadcast_to(wlast_row, (PAIR, PAIR)), zero2)
    n = jnp.where(same_head,
                  _dot_tn(b16(jnp.concatenate([u0, v], axis=0)),
                          b16(jnp.concatenate([kbe, kke], axis=0))), zero2)
    return qr_p, y0, m, n


def _rwkv_kernel(*refs, has_vres, tc, width):
    if has_vres:
        (pa_ref, mu_ref, wl_ref, w0_ref, a0_ref, kk_ref, ka_ref, rk_ref, gng_ref, gnb_ref, seg_ref,
         vf_ref, v0_ref, vd_ref, vu_ref, y_ref, prev_ref, s_ref, yraw_ref) = refs
    else:
        (pa_ref, mu_ref, wl_ref, w0_ref, a0_ref, kk_ref, ka_ref, rk_ref, gng_ref, gnb_ref, seg_ref,
         y_ref, vf_ref, prev_ref, s_ref, yraw_ref) = refs
    n_pairs = width // PAIR
    n_chunks = tc // CHUNK

    @pl.when(pl.program_id(1) == 0)
    def _():
        prev_ref[...] = jnp.zeros_like(prev_ref)
        s_ref[...] = jnp.zeros_like(s_ref)

    p = pa_ref[...]
    row = lax.broadcasted_iota(jnp.int32, p.shape, 0)
    shifted = jnp.where(row == 0, jnp.broadcast_to(prev_ref[0:1, :], p.shape), pltpu.roll(p, 1, axis=0))
    prev_ref[0:1, :] = p[tc - 1:tc, :]
    p = p + mu_ref[...] * (shifted - p)

    r = p[:, :width]
    k = p[:, width:2 * width]
    v = p[:, 2 * width:3 * width]
    lo_in = p[:, 3 * width:]
    lane = lax.broadcasted_iota(jnp.int32, lo_in.shape, 1)
    lo_act = jnp.where(lane < LORA_W, jnp.tanh(lo_in),
                       jnp.where(lane < LORA_W + LORA_A, lo_in, _sigmoid(lo_in)))
    lo = _dot(lo_act, wl_ref[...], HI)
    wz = w0_ref[...] + lo[:, :width]
    w = -(jnp.maximum(-wz, 0.0) + jnp.log(1.0 + jnp.exp(-jnp.abs(wz)))) - 0.5
    lw = -jnp.exp(w)
    a = _sigmoid(a0_ref[...] + lo[:, width:2 * width])
    g = lo[:, 2 * width:]

    if has_vres:
        gate = _sigmoid(v0_ref[...] + _dot(_dot(v, vd_ref[...], HI), vu_ref[...], HI))
        v = v + (vf_ref[...] - v) * gate
    else:
        vf_ref[...] = v

    seg = seg_ref[...]
    kkx = k * kk_ref[...]
    kk = kkx / jnp.maximum(jnp.sqrt(_dot(kkx * kkx, seg, HI)), 1e-12)
    kmod = k * (1.0 + (a - 1.0) * ka_ref[...])

    ti = lax.broadcasted_iota(jnp.int32, (tc, tc), 0)
    tj = lax.broadcasted_iota(jnp.int32, (tc, tc), 1)
    same_chunk = (ti // CHUNK) == (tj // CHUNK)
    one, zero = jnp.ones((tc, tc), F32), jnp.zeros((tc, tc), F32)
    cum = _dot(jnp.where(same_chunk & (tj <= ti), one, zero), lw, HI)
    tot = _dot(jnp.where(same_chunk, one, zero), lw, HI)
    w_inc = jnp.exp(cum)
    w_exc = jnp.exp(cum - lw)
    w_inv = jnp.exp(-cum)
    w_end = jnp.exp(tot - cum)
    w_tot = jnp.exp(tot)

    kka = kk * a
    qa_all = -kk * w_exc
    qr_all = r * w_inc
    kb_all = kka * w_inv
    kk_all = kmod * w_inv
    kbe_all = kka * w_end
    kke_all = kmod * w_end

    for pi in range(n_pairs):
        ls = slice(pi * PAIR, (pi + 1) * PAIR)
        maps = []
        for ci in range(n_chunks):
            rs = slice(ci * CHUNK, (ci + 1) * CHUNK)
            maps.append(_rwkv_chunk_maps(qa_all[rs, ls], qr_all[rs, ls], kb_all[rs, ls], kk_all[rs, ls],
                                         kbe_all[rs, ls], kke_all[rs, ls], v[rs, ls],
                                         w_tot[ci * CHUNK:ci * CHUNK + 1, ls]))
        s = s_ref[pi]
        for ci, (qr_p, y0, m, n) in enumerate(maps):
            s16 = s.astype(BF16)
            yraw_ref[ci * CHUNK:(ci + 1) * CHUNK, ls] = _dot_nt(qr_p.astype(BF16), s16) + y0
            s = _dot(s16, m.astype(BF16)) + n
        s_ref[pi] = s

    y = yraw_ref[...]
    mean = _dot(y, seg, HI) * (1.0 / HEAD)
    yc = y - mean
    var = _dot(yc * yc, seg, HI) * (1.0 / HEAD)
    yn = yc * lax.rsqrt(var + GN_EPS) * gng_ref[...] + gnb_ref[...]
    bonus = _dot(r * kmod * rk_ref[...], seg, HI) * v
    y_ref[...] = ((yn + bonus) * g).astype(y_ref.dtype)


def _rwkv(pa, vfirst, prm, *, tc):
    b, t, a_proj = pa.shape
    width = prm["w0"].shape[-1]
    has_vres = vfirst is not None
    full = lambda a: pl.BlockSpec(a.shape, lambda i, j: (0,) * a.ndim)
    tile = lambda w_: pl.BlockSpec((None, tc, w_), lambda i, j: (i, j, 0))
    names = ["mu", "wl", "w0", "a0", "k_k", "k_a", "r_k", "gn_g", "gn_b", "seg"]
    args = [pa] + [prm[nm] for nm in names]
    in_specs = [tile(a_proj)] + [full(prm[nm]) for nm in names]
    y_shape = jax.ShapeDtypeStruct((b, t, width), BF16)
    if has_vres:
        extra = [prm["v0"], prm["v_down"], prm["v_up"]]
        args += [vfirst] + extra
        in_specs += [tile(width)] + [full(a) for a in extra]
        out_specs, out_shape = tile(width), y_shape
    else:
        out_specs = [tile(width), tile(width)]
        out_shape = [y_shape, jax.ShapeDtypeStruct((b, t, width), F32)]
    return pl.pallas_call(
        functools.partial(_rwkv_kernel, has_vres=has_vres, tc=tc, width=width),
        grid=(b, t // tc),
        in_specs=in_specs,
        out_specs=out_specs,
        out_shape=out_shape,
        scratch_shapes=[pltpu.VMEM((8, a_proj), F32),
                        pltpu.VMEM((width // PAIR, PAIR, PAIR), F32),
                        pltpu.VMEM((tc, width), F32)],
        compiler_params=_params("parallel", "arbitrary"),
        name="rwkv",
    )(*args)


def _attn_pool_kernel(q_ref, k_ref, v_ref, bias_ref, u_ref, pw_ref, ps_ref, yb_ref, yc_ref, *, width):
    n = pl.program_id(1)
    start = pl.multiple_of(n * CHUNK, CHUNK)
    q = q_ref[...]
    key_pos = lax.broadcasted_iota(jnp.int32, (PAIR, BAND), 1)
    valid = key_pos >= (PREV_CHUNKS - n) * CHUNK
    first = _lane_lt_head((CHUNK, PAIR))
    for pi in range(width // PAIR):
        ls = slice(pi * PAIR, (pi + 1) * PAIR)
        q2 = q[:, ls]
        zq = jnp.zeros_like(q2)
        qs = jnp.concatenate([jnp.where(first, q2, zq), jnp.where(first, zq, q2)], axis=0)
        k2 = k_ref[pl.ds(start, BAND), ls]
        v2 = v_ref[pl.ds(start, BAND), ls]
        s = jnp.where(valid, _dot_nt(qs, k2) + bias_ref[pi], NEG_INF)
        e = jnp.exp(s - jnp.max(s, axis=-1, keepdims=True))
        o = _dot(e.astype(BF16), v2) / jnp.sum(e, axis=-1, keepdims=True)
        yb_ref[:, ls] = jnp.where(first, o[:CHUNK], o[CHUNK:]).astype(yb_ref.dtype)

    x = u_ref[pl.ds(start, 2 * CHUNK), :]
    sums = []
    acc, span = x, 1
    for win in POOL_WINDOWS:
        while span < win:
            acc = acc + pltpu.roll(acc, span, axis=0)
            span *= 2
        sums.append(acc)
    lane = lax.broadcasted_iota(jnp.int32, (CHUNK, x.shape[1]), 1)
    t1 = (lax.broadcasted_iota(jnp.int32, (CHUNK, x.shape[1]), 0) + start + 1).astype(F32)
    pooled = jnp.zeros((CHUNK, x.shape[1]), F32)
    for gi, win in enumerate(POOL_WINDOWS):
        grp = (lane >= gi * HEAD) & (lane < (gi + 1) * HEAD)
        pooled = jnp.where(grp, sums[gi][CHUNK:] / jnp.minimum(t1, float(win)), pooled)
    pooled = pooled - x[CHUNK:]
    yc_ref[...] = (_dot(pooled.astype(BF16), pw_ref[...]) * ps_ref[...]).astype(yc_ref.dtype)


def _attn_pool(q, kpad, vpad, bias, upad, pw, ps):
    b, t, width = q.shape
    cw = upad.shape[-1]
    full = lambda a: pl.BlockSpec(a.shape, lambda i, j: (0,) * a.ndim)
    seq = lambda a: pl.BlockSpec((None,) + a.shape[1:], lambda i, j: (i, 0, 0))
    tile = lambda w_: pl.BlockSpec((None, CHUNK, w_), lambda i, j: (i, j, 0))
    return pl.pallas_call(
        functools.partial(_attn_pool_kernel, width=width),
        grid=(b, t // CHUNK),
        in_specs=[tile(width), seq(kpad), seq(vpad), full(bias), seq(upad), full(pw), full(ps)],
        out_specs=[tile(width), tile(cw)],
        out_shape=[jax.ShapeDtypeStruct((b, t, width), BF16), jax.ShapeDtypeStruct((b, t, cw), BF16)],
        compiler_params=_params("parallel", "arbitrary"),
        name="attn_pool",
    )(q, kpad, vpad, bias, upad, pw, ps)


def _out_cross_kernel(x_ref, ya_ref, yb_ref, yc_ref, woa_ref, wob_ref, woc_ref, g_ref, wq_ref, qg_ref,
                      k_ref, v_ref, wo_ref, o_ref, att_ref, *, heads):
    x = (x_ref[...] + _dot(ya_ref[...], woa_ref[...]) + _dot(yb_ref[...], wob_ref[...])
         + _dot(yc_ref[...], woc_ref[...]))
    h = _rms(x, g_ref[...]).astype(BF16)
    q = _dot(h, wq_ref[...])
    hd = q.shape[1] // heads
    for hi in range(heads):
        cs = slice(hi * hd, (hi + 1) * hd)
        qh = (_rms(q[:, cs], qg_ref[...]) * (hd ** -0.5)).astype(BF16)
        s = _dot_nt(qh, k_ref[:, cs])
        e = jnp.exp(s - jnp.max(s, axis=-1, keepdims=True))
        oh = _dot(e.astype(BF16), v_ref[:, cs]) / jnp.sum(e, axis=-1, keepdims=True)
        att_ref[:, cs] = oh.astype(BF16)
    o_ref[...] = x + _dot(att_ref[...], wo_ref[...])


def _out_cross(x, ya, yb, yc, woa, wob, woc, g, wq, qg, k, v, wo, *, tm, heads):
    b, t, d = x.shape
    full = lambda a: pl.BlockSpec(a.shape, lambda i, j: (0,) * a.ndim)
    tile = lambda a: pl.BlockSpec((None, tm, a.shape[-1]), lambda i, j: (i, j, 0))
    mem = lambda a: pl.BlockSpec((None,) + a.shape[1:], lambda i, j: (i, 0, 0))
    return pl.pallas_call(
        functools.partial(_out_cross_kernel, heads=heads),
        grid=(b, t // tm),
        in_specs=[tile(x), tile(ya), tile(yb), tile(yc), full(woa), full(wob), full(woc), full(g), full(wq),
                  full(qg), mem(k), mem(v), full(wo)],
        out_specs=tile(x),
        out_shape=jax.ShapeDtypeStruct((b, t, d), F32),
        scratch_shapes=[pltpu.VMEM((tm, d), BF16)],
        compiler_params=_params("parallel", "parallel"),
        name="out_cross",
    )(x, ya, yb, yc, woa, wob, woc, g, wq, qg, k, v, wo)


def _mem_kv_kernel(mem_ref, g_ref, w_ref, kg_ref, k_ref, v_ref, *, heads):
    h = _rms(mem_ref[...], g_ref[...]).astype(BF16)
    kv = _dot(h, w_ref[...])
    d = kv.shape[1] // 2
    hd = d // heads
    for hi in range(heads):
        cs = slice(hi * hd, (hi + 1) * hd)
        k_ref[:, cs] = _rms(kv[:, cs], kg_ref[...]).astype(BF16)
    v_ref[...] = kv[:, d:].astype(BF16)


def _mem_kv(mem, g, wkv, kg, *, heads):
    b, m, d = mem.shape
    depth = wkv.shape[0]
    per_layer = lambda a: pl.BlockSpec((None,) + a.shape[1:], lambda l, i: (l,) + (0,) * (a.ndim - 1))
    out = pl.BlockSpec((None, None, m, d), lambda l, i: (l, i, 0, 0))
    return pl.pallas_call(
        functools.partial(_mem_kv_kernel, heads=heads),
        grid=(depth, b),
        in_specs=[pl.BlockSpec((None, m, d), lambda l, i: (i, 0, 0)), per_layer(g), per_layer(wkv), per_layer(kg)],
        out_specs=[out, out],
        out_shape=[jax.ShapeDtypeStruct((depth, b, m, d), BF16)] * 2,
        compiler_params=_params("parallel", "parallel"),
        name="mem_kv",
    )(mem, g, wkv, kg)


def _segment_ones(width):
    head = np.arange(width) // HEAD
    return jnp.asarray((head[:, None] == head[None, :]).astype(np.float32))


def _rel_bias_band(rel_bias):
    dist = PREV_CHUNKS * CHUNK + np.arange(CHUNK)[:, None] - np.arange(BAND)[None, :]
    rel_idx = np.clip(dist, -(CHUNK - 1), REL_MAX) + (CHUNK - 1)
    bias = rel_bias[:, rel_idx]
    return bias.reshape(rel_bias.shape[0] // 2, 2 * CHUNK, BAND)


def _lora_weight(w_up, a_up, g_up):
    width = w_up.shape[-1]
    z = lambda r: jnp.zeros((r, width), F32)
    return jnp.concatenate([
        jnp.concatenate([w_up, z(LORA_W), z(LORA_W)], axis=1),
        jnp.concatenate([z(LORA_A), a_up, z(LORA_A)], axis=1),
        jnp.concatenate([z(LORA_G), z(LORA_G), g_up], axis=1)], axis=0)


def _pool_weight(pool_w):
    groups, cg, _ = pool_w.shape
    out = jnp.zeros((groups * cg, groups * cg), F32)
    for gi in range(groups):
        out = out.at[gi * cg:(gi + 1) * cg, gi * cg:(gi + 1) * cg].set(pool_w[gi])
    return out


def kernel(x, mem, norm_ffn1, ffn1_wi, ffn1_wo, norm_mix, w_in, w_out, a_mu, a_w0, a_w_up, a_a0, a_a_up, a_g_up,
           a_k_k, a_k_a, a_r_k, a_gn_g, a_gn_b, a_v0, a_v_down, a_v_up, b_q_gain, b_k_gain, b_rel_bias,
           c_pool_w, c_pool_scale, norm_cross, norm_mem, x_wq, x_wkv, x_wo, x_q_gain, x_k_gain,
           norm_ffn2, ffn2_wi, ffn2_wo):
    b, t, d = x.shape
    depth = w_in.shape[0]
    a_width = a_w0.shape[-1]
    a_proj = a_mu.shape[-1]
    b_width = b_rel_bias.shape[1] * HEAD
    c_width = c_pool_scale.shape[-1]
    x_heads = d // x_q_gain.shape[-1]
    n_tok = b * t
    row = lambda a: a.reshape(1, -1)

    seg_a = _segment_ones(a_width)
    seg_b = _segment_ones(b_width)
    mem_k, mem_v = _mem_kv(mem, norm_mem[:, None, :], x_wkv.astype(BF16), x_k_gain[:, None, :], heads=x_heads)

    ffn_tf = ffn1_wo.shape[1] // 2
    v_first = None
    for l in range(depth):
        x = _ffn(x.reshape(n_tok, d), row(norm_ffn1[l]), ffn1_wi[l].astype(BF16), ffn1_wo[l].astype(BF16),
                 tm=512, tf=ffn_tf)

        q_gain = row(jnp.tile(b_q_gain[l], b_width // HEAD)) * (HEAD ** -0.5)
        k_gain = row(jnp.tile(b_k_gain[l], b_width // HEAD))
        pa, q, k, v, u = _inproj(x, row(norm_mix[l]), w_in[l].astype(BF16), seg_b, q_gain, k_gain,
                                 tm=512, a_proj=a_proj, b_width=b_width, c_width=c_width)

        prm = {"mu": row(a_mu[l]), "wl": _lora_weight(a_w_up[l], a_a_up[l], a_g_up[l]), "w0": row(a_w0[l]),
               "a0": row(a_a0[l]), "k_k": row(a_k_k[l]), "k_a": row(a_k_a[l]), "r_k": row(a_r_k[l]),
               "gn_g": row(a_gn_g[l]), "gn_b": row(a_gn_b[l]), "seg": seg_a}
        if l == 0:
            y_a, v_first = _rwkv(pa.reshape(b, t, a_proj), None, prm, tc=256)
        else:
            pad_lanes = LANES - LORA_V
            prm["v0"] = row(a_v0[l - 1])
            prm["v_down"] = jnp.pad(a_v_down[l - 1], ((0, 0), (0, pad_lanes)))
            prm["v_up"] = jnp.pad(a_v_up[l - 1], ((0, pad_lanes), (0, 0)))
            y_a = _rwkv(pa.reshape(b, t, a_proj), v_first, prm, tc=256)

        band_pad = ((0, 0), (PREV_CHUNKS * CHUNK, 0), (0, 0))
        y_b, y_c = _attn_pool(q.reshape(b, t, b_width),
                              jnp.pad(k.reshape(b, t, b_width), band_pad),
                              jnp.pad(v.reshape(b, t, b_width), band_pad),
                              _rel_bias_band(b_rel_bias[l]),
                              jnp.pad(u.reshape(b, t, c_width), ((0, 0), (CHUNK, 0), (0, 0))),
                              _pool_weight(c_pool_w[l]).astype(BF16), row(c_pool_scale[l]))

        wo16 = w_out[l].astype(BF16)
        x = _out_cross(x.reshape(b, t, d), y_a, y_b, y_c,
                       wo16[:a_width], wo16[a_width:a_width + b_width], wo16[a_width + b_width:],
                       row(norm_cross[l]), x_wq[l].astype(BF16), row(x_q_gain[l]),
                       mem_k[l], mem_v[l], x_wo[l].astype(BF16), tm=512, heads=x_heads)

        x = _ffn(x.reshape(n_tok, d), row(norm_ffn2[l]), ffn2_wi[l].astype(BF16), ffn2_wo[l].astype(BF16),
                 tm=512, tf=ffn_tf)
    return x.reshape(b, t, d)
```

```python
import functools

import numpy as np
import jax
import jax.numpy as jnp
from jax import lax
from jax.experimental import pallas as pl
from jax.experimental.pallas import tpu as pltpu

F32 = jnp.float32
BF16 = jnp.bfloat16
HI = lax.Precision.HIGHEST

LANES = 128
HEAD = 64
PAIR = 2 * HEAD
CHUNK = 64
PREV_CHUNKS = 8
BAND = (PREV_CHUNKS + 1) * CHUNK
REL_MAX = 256
POOL_WINDOWS = (2, 4, 8, 16)
LORA_W, LORA_A, LORA_G, LORA_V = 32, 32, 64, 32
RMS_EPS = 1e-6
GN_EPS = 64e-5
NEG_INF = -1e30
VMEM_LIMIT = 56 * 1024 * 1024


def _dot(a, b, precision=None):
    return jnp.dot(a, b, preferred_element_type=F32, precision=precision)


def _dot_nt(a, b):
    return lax.dot_general(a, b, (((1,), (1,)), ((), ())), preferred_element_type=F32)


def _dot_tn(a, b):
    return lax.dot_general(a, b, (((0,), (0,)), ((), ())), preferred_element_type=F32)


def _split(x, terms):
    parts = []
    for _ in range(terms):
        hi = x.astype(BF16)
        parts.append(hi)
        x = x - hi.astype(F32)
    return parts


def _dot_lhs_split(x, m16, terms):
    out = None
    for part in _split(x, terms):
        d = _dot(part, m16)
        out = d if out is None else out + d
    return out


def _dot_x3(x, w_hi, w_lo):
    x_hi, x_lo = _split(x, 2)
    return _dot(x_hi, w_hi) + (_dot(x_lo, w_hi) + _dot(x_hi, w_lo))


def _rms(x, g):
    return x * lax.rsqrt(jnp.mean(x * x, axis=-1, keepdims=True) + RMS_EPS) * g


def _sigmoid(x):
    return 1.0 / (1.0 + jnp.exp(-x))


def _params(*sem):
    return pltpu.CompilerParams(dimension_semantics=sem, vmem_limit_bytes=VMEM_LIMIT)


def _ffn_kernel(x_ref, g_ref, wg_ref, wu_ref, wo_ref, o_ref, h_ref, acc_ref):
    j = pl.program_id(1)

    @pl.when(j == 0)
    def _():
        h_ref[...] = _rms(x_ref[...], g_ref[...]).astype(BF16)
        acc_ref[...] = jnp.zeros_like(acc_ref)

    h = h_ref[...]
    gate = _dot(h, wg_ref[...])
    up = _dot(h, wu_ref[...])
    act = (gate * _sigmoid(gate) * up).astype(BF16)
    acc_ref[...] += _dot(act, wo_ref[...])

    @pl.when(j == pl.num_programs(1) - 1)
    def _():
        o_ref[...] = x_ref[...] + 0.5 * acc_ref[...]


def _ffn(x, g, wi, wo, *, tm, tf):
    n, d = x.shape
    dff = wo.shape[0]
    nf = dff // tf
    return pl.pallas_call(
        _ffn_kernel,
        grid=(n // tm, nf),
        in_specs=[
            pl.BlockSpec((tm, d), lambda i, j: (i, 0)),
            pl.BlockSpec((1, d), lambda i, j: (0, 0)),
            pl.BlockSpec((d, tf), lambda i, j: (0, j)),
            pl.BlockSpec((d, tf), lambda i, j: (0, nf + j)),
            pl.BlockSpec((tf, d), lambda i, j: (j, 0)),
        ],
        out_specs=pl.BlockSpec((tm, d), lambda i, j: (i, 0)),
        out_shape=jax.ShapeDtypeStruct((n, d), F32),
        scratch_shapes=[pltpu.VMEM((tm, d), BF16), pltpu.VMEM((tm, d), F32)],
        compiler_params=_params("parallel", "arbitrary"),
        name="ffn",
    )(x, g, wi, wi, wo)


def _inproj_kernel(x_ref, g_ref, w_ref, seg_ref, qg_ref, kg_ref,
                   pa_ref, q_ref, k_ref, v_ref, u_ref, *, a_proj, b_width):
    h = _rms(x_ref[...], g_ref[...]).astype(BF16)
    p = _dot(h, w_ref[...])
    pa_ref[...] = p[:, :a_proj]
    q = p[:, a_proj:a_proj + b_width]
    k = p[:, a_proj + b_width:a_proj + 2 * b_width]
    seg = seg_ref[...]
    qms = _dot((q * q).astype(BF16), seg) * (1.0 / HEAD)
    kms = _dot((k * k).astype(BF16), seg) * (1.0 / HEAD)
    q_ref[...] = (q * lax.rsqrt(qms + RMS_EPS) * qg_ref[...]).astype(BF16)
    k_ref[...] = (k * lax.rsqrt(kms + RMS_EPS) * kg_ref[...]).astype(BF16)
    v_ref[...] = p[:, a_proj + 2 * b_width:a_proj + 3 * b_width].astype(BF16)
    u_ref[...] = p[:, a_proj + 3 * b_width:]


def _inproj(x, g, w, seg, qg, kg, *, tm, a_proj, b_width, c_width):
    n, d = x.shape
    full = lambda a: pl.BlockSpec(a.shape, lambda i: (0,) * a.ndim)
    row = lambda w_: pl.BlockSpec((tm, w_), lambda i: (i, 0))
    return pl.pallas_call(
        functools.partial(_inproj_kernel, a_proj=a_proj, b_width=b_width),
        grid=(n // tm,),
        in_specs=[row(d), full(g), full(w), full(seg), full(qg), full(kg)],
        out_specs=[row(a_proj), row(b_width), row(b_width), row(b_width), row(c_width)],
        out_shape=[jax.ShapeDtypeStruct((n, a_proj), F32),
                   jax.ShapeDtypeStruct((n, b_width), BF16),
                   jax.ShapeDtypeStruct((n, b_width), BF16),
                   jax.ShapeDtypeStruct((n, b_width), BF16),
                   jax.ShapeDtypeStruct((n, c_width), F32)],
        compiler_params=_params("parallel"),
        name="in_proj",
    )(x, g, w, seg, qg, kg)


def _lane_lt_head(shape):
    return lax.broadcasted_iota(jnp.int32, shape, len(shape) - 1) < HEAD


def _blockdiag(x2):
    first = _lane_lt_head(x2.shape)
    zero = jnp.zeros_like(x2)
    return jnp.concatenate([jnp.where(first, x2, zero), jnp.where(first, zero, x2)], axis=0)


def _rwkv_chunk_maps(insts):
    c = CHUNK
    b16 = lambda t: t.astype(BF16)
    t_idx = lax.broadcasted_iota(jnp.int32, (c, PAIR), 0)
    s_idx = lax.broadcasted_iota(jnp.int32, (c, PAIR), 1) % HEAD
    strict = t_idx > s_idx
    incl = t_idx >= s_idx
    zero = jnp.zeros((c, PAIR), F32)
    eye = jnp.where(t_idx == s_idx, 1.0, 0.0)
    row = lax.broadcasted_iota(jnp.int32, (PAIR, PAIR), 0)
    col = lax.broadcasted_iota(jnp.int32, (PAIR, PAIR), 1)
    same_head = (row < HEAD) == (col < HEAD)
    zero2 = jnp.zeros((PAIR, PAIR), F32)

    s_all = [_dot_nt(b16(jnp.concatenate([qa, qr], axis=0)),
                     b16(jnp.concatenate([_blockdiag(kb), _blockdiag(kk)], axis=0)))
             for qa, qr, kb, kk, _, _, _, _ in insts]
    a_ab = [jnp.where(strict, s[:c, :PAIR], zero) for s in s_all]
    a_ak = [jnp.where(strict, s[:c, PAIR:], zero) for s in s_all]
    a_rb = [jnp.where(incl, s[c:, :PAIR], zero) for s in s_all]
    a_rk = [jnp.where(incl, s[c:, PAIR:], zero) for s in s_all]
    av = [_dot(b16(jnp.concatenate([ak, rk], axis=0)), b16(_blockdiag(inst[6])))
          for ak, rk, inst in zip(a_ak, a_rk, insts)]

    apow = a_ab
    tinv = [eye + a for a in a_ab]
    for _ in range(5):
        apow = [_dot(b16(a), b16(_blockdiag(a))) for a in apow]
        tinv = [t + _dot(b16(t), b16(_blockdiag(a))) for t, a in zip(tinv, apow)]

    r1 = [_dot(b16(t), b16(jnp.concatenate([_blockdiag(inst[0]), _blockdiag(a[:c])], axis=1)))
          for t, a, inst in zip(tinv, av, insts)]
    r2 = [_dot(b16(rb), b16(jnp.concatenate([_blockdiag(r[:, :PAIR]), _blockdiag(r[:, PAIR:])], axis=1)))
          for rb, r in zip(a_rb, r1)]
    out = []
    for r1_i, r2_i, av_i, (qa, qr, kb, kk, kbe, kke, v, wlast_row) in zip(r1, r2, av, insts):
        qa_p, u0 = r1_i[:, :PAIR], r1_i[:, PAIR:]
        m = jnp.where(same_head, _dot_tn(b16(qa_p), b16(kbe)), zero2)
        m = m + jnp.where(row == col, jnp.broadcast_to(wlast_row, (PAIR, PAIR)), zero2)
        n = jnp.where(same_head,
                      _dot_tn(b16(jnp.concatenate([u0, v], axis=0)),
                              b16(jnp.concatenate([kbe, kke], axis=0))), zero2)
        out.append((qr + r2_i[:, :PAIR], r2_i[:, PAIR:] + av_i[c:], m, n))
    return out


def _rwkv_kernel(*refs, has_vres, tc, width):
    if has_vres:
        (pa_ref, mu_ref, wlh_ref, wll_ref, w0_ref, a0_ref, kk_ref, ka_ref, rk_ref, gng_ref, gnb_ref, seg_ref,
         vf_ref, v0_ref, vdh_ref, vdl_ref, vuh_ref, vul_ref, y_ref, prev_ref, s_ref, yraw_ref) = refs
    else:
        (pa_ref, mu_ref, wlh_ref, wll_ref, w0_ref, a0_ref, kk_ref, ka_ref, rk_ref, gng_ref, gnb_ref, seg_ref,
         y_ref, vf_ref, prev_ref, s_ref, yraw_ref) = refs
    n_pairs = width // PAIR
    n_chunks = tc // CHUNK

    @pl.when(pl.program_id(1) == 0)
    def _():
        prev_ref[...] = jnp.zeros_like(prev_ref)
        s_ref[...] = jnp.zeros_like(s_ref)

    p = pa_ref[...]
    row = lax.broadcasted_iota(jnp.int32, p.shape, 0)
    shifted = jnp.where(row == 0, jnp.broadcast_to(prev_ref[0:1, :], p.shape), pltpu.roll(p, 1, axis=0))
    prev_ref[0:1, :] = p[tc - 1:tc, :]
    p = p + mu_ref[...] * (shifted - p)

    r = p[:, :width]
    k = p[:, width:2 * width]
    v = p[:, 2 * width:3 * width]
    lo_in = p[:, 3 * width:]
    lane = lax.broadcasted_iota(jnp.int32, lo_in.shape, 1)
    lo_act = jnp.where(lane < LORA_W, jnp.tanh(lo_in),
                       jnp.where(lane < LORA_W + LORA_A, lo_in, _sigmoid(lo_in)))
    lo = _dot_x3(lo_act, wlh_ref[...], wll_ref[...])
    wz = w0_ref[...] + lo[:, :width]
    w = -(jnp.maximum(-wz, 0.0) + jnp.log(1.0 + jnp.exp(-jnp.abs(wz)))) - 0.5
    lw = -jnp.exp(w)
    a = _sigmoid(a0_ref[...] + lo[:, width:2 * width])
    g = lo[:, 2 * width:]

    if has_vres:
        v_lo = _dot_x3(v, vdh_ref[...], vdl_ref[...])
        gate = _sigmoid(v0_ref[...] + _dot_x3(v_lo, vuh_ref[...], vul_ref[...]))
        v = v + (vf_ref[...] - v) * gate
    else:
        vf_ref[...] = v

    seg = seg_ref[...]
    kkx = k * kk_ref[...]
    kk = kkx / jnp.maximum(jnp.sqrt(_dot((kkx * kkx).astype(BF16), seg)), 1e-12)
    kmod = k * (1.0 + (a - 1.0) * ka_ref[...])

    ti = lax.broadcasted_iota(jnp.int32, (tc, tc), 0)
    tj = lax.broadcasted_iota(jnp.int32, (tc, tc), 1)
    tri = jnp.where(((ti // CHUNK) == (tj // CHUNK)) & (tj <= ti), 1.0, 0.0).astype(BF16)
    cum = None
    for part in _split(lw, 3):
        d = _dot(tri, part)
        cum = d if cum is None else cum + d
    tot = jnp.concatenate(
        [jnp.broadcast_to(cum[(ci + 1) * CHUNK - 1:(ci + 1) * CHUNK, :], (CHUNK, width)) for ci in range(n_chunks)],
        axis=0)
    w_inc = jnp.exp(cum)
    w_exc = jnp.exp(cum - lw)
    w_inv = jnp.exp(-cum)
    w_end = jnp.exp(tot - cum)
    w_tot = jnp.exp(tot)

    kka = kk * a
    qa_all = -kk * w_exc
    qr_all = r * w_inc
    kb_all = kka * w_inv
    kk_all = kmod * w_inv
    kbe_all = kka * w_end
    kke_all = kmod * w_end

    insts = []
    for ci in range(n_chunks):
        rs = slice(ci * CHUNK, (ci + 1) * CHUNK)
        for pi in range(n_pairs):
            ls = slice(pi * PAIR, (pi + 1) * PAIR)
            insts.append((qa_all[rs, ls], qr_all[rs, ls], kb_all[rs, ls], kk_all[rs, ls],
                          kbe_all[rs, ls], kke_all[rs, ls], v[rs, ls], w_tot[ci * CHUNK:ci * CHUNK + 1, ls]))
    maps = _rwkv_chunk_maps(insts)
    states = [s_ref[pi] for pi in range(n_pairs)]
    for ci in range(n_chunks):
        for pi in range(n_pairs):
            qr_p, y0, m, n = maps[ci * n_pairs + pi]
            s16 = states[pi].astype(BF16)
            yraw_ref[ci * CHUNK:(ci + 1) * CHUNK, pi * PAIR:(pi + 1) * PAIR] = _dot_nt(qr_p.astype(BF16), s16) + y0
            states[pi] = _dot(s16, m.astype(BF16)) + n
    for pi in range(n_pairs):
        s_ref[pi] = states[pi]

    y = yraw_ref[...]
    mean = _dot_lhs_split(y, seg, 2) * (1.0 / HEAD)
    yc = y - mean
    var = _dot_lhs_split(yc * yc, seg, 2) * (1.0 / HEAD)
    yn = yc * lax.rsqrt(var + GN_EPS) * gng_ref[...] + gnb_ref[...]
    bonus = _dot((r * kmod * rk_ref[...]).astype(BF16), seg) * v
    y_ref[...] = ((yn + bonus) * g).astype(y_ref.dtype)


def _rwkv(pa, vfirst, prm, *, tc):
    b, t, a_proj = pa.shape
    width = prm["w0"].shape[-1]
    has_vres = vfirst is not None
    full = lambda a: pl.BlockSpec(a.shape, lambda i, j: (0,) * a.ndim)
    tile = lambda w_: pl.BlockSpec((None, tc, w_), lambda i, j: (i, j, 0))
    names = ["mu", "wl_hi", "wl_lo", "w0", "a0", "k_k", "k_a", "r_k", "gn_g", "gn_b", "seg"]
    args = [pa] + [prm[nm] for nm in names]
    in_specs = [tile(a_proj)] + [full(prm[nm]) for nm in names]
    y_shape = jax.ShapeDtypeStruct((b, t, width), BF16)
    if has_vres:
        extra = [prm["v0"], prm["vd_hi"], prm["vd_lo"], prm["vu_hi"], prm["vu_lo"]]
        args += [vfirst] + extra
        in_specs += [tile(width)] + [full(a) for a in extra]
        out_specs, out_shape = tile(width), y_shape
    else:
        out_specs = [tile(width), tile(width)]
        out_shape = [y_shape, jax.ShapeDtypeStruct((b, t, width), F32)]
    return pl.pallas_call(
        functools.partial(_rwkv_kernel, has_vres=has_vres, tc=tc, width=width),
        grid=(b, t // tc),
        in_specs=in_specs,
        out_specs=out_specs,
        out_shape=out_shape,
        scratch_shapes=[pltpu.VMEM((8, a_proj), F32),
                        pltpu.VMEM((width // PAIR, PAIR, PAIR), F32),
                        pltpu.VMEM((tc, width), F32)],
        compiler_params=_params("parallel", "arbitrary"),
        name="rwkv",
    )(*args)


def _attn_pool_kernel(q_ref, k_ref, v_ref, bias_ref, u_ref, pw_ref, ps_ref, yb_ref, yc_ref, *, width):
    n = pl.program_id(1)
    start = pl.multiple_of(n * CHUNK, CHUNK)
    q = q_ref[...]
    key_pos = lax.broadcasted_iota(jnp.int32, (PAIR, BAND), 1)
    valid = key_pos >= (PREV_CHUNKS - n) * CHUNK
    first = _lane_lt_head((CHUNK, PAIR))
    for pi in range(width // PAIR):
        ls = slice(pi * PAIR, (pi + 1) * PAIR)
        q2 = q[:, ls]
        zq = jnp.zeros_like(q2)
        qs = jnp.concatenate([jnp.where(first, q2, zq), jnp.where(first, zq, q2)], axis=0)
        k2 = k_ref[pl.ds(start, BAND), ls]
        v2 = v_ref[pl.ds(start, BAND), ls]
        s = jnp.where(valid, _dot_nt(qs, k2) + bias_ref[pi], NEG_INF)
        e = jnp.exp(s - jnp.max(s, axis=-1, keepdims=True))
        o = _dot(e.astype(BF16), v2) / jnp.sum(e, axis=-1, keepdims=True)
        yb_ref[:, ls] = jnp.where(first, o[:CHUNK], o[CHUNK:]).astype(yb_ref.dtype)

    x = u_ref[pl.ds(start, 2 * CHUNK), :]
    sums = []
    acc, span = x, 1
    for win in POOL_WINDOWS:
        while span < win:
            acc = acc + pltpu.roll(acc, span, axis=0)
            span *= 2
        sums.append(acc)
    lane = lax.broadcasted_iota(jnp.int32, (CHUNK, x.shape[1]), 1)
    t1 = (lax.broadcasted_iota(jnp.int32, (CHUNK, x.shape[1]), 0) + start + 1).astype(F32)
    pooled = jnp.zeros((CHUNK, x.shape[1]), F32)
    for gi, win in enumerate(POOL_WINDOWS):
        grp = (lane >= gi * HEAD) & (lane < (gi + 1) * HEAD)
        pooled = jnp.where(grp, sums[gi][CHUNK:] / jnp.minimum(t1, float(win)), pooled)
    pooled = pooled - x[CHUNK:]
    yc_ref[...] = (_dot(pooled.astype(BF16), pw_ref[...]) * ps_ref[...]).astype(yc_ref.dtype)


def _attn_pool(q, kpad, vpad, bias, upad, pw, ps):
    b, t, width = q.shape
    cw = upad.shape[-1]
    full = lambda a: pl.BlockSpec(a.shape, lambda i, j: (0,) * a.ndim)
    seq = lambda a: pl.BlockSpec((None,) + a.shape[1:], lambda i, j: (i, 0, 0))
    tile = lambda w_: pl.BlockSpec((None, CHUNK, w_), lambda i, j: (i, j, 0))
    return pl.pallas_call(
        functools.partial(_attn_pool_kernel, width=width),
        grid=(b, t // CHUNK),
        in_specs=[tile(width), seq(kpad), seq(vpad), full(bias), seq(upad), full(pw), full(ps)],
        out_specs=[tile(width), tile(cw)],
        out_shape=[jax.ShapeDtypeStruct((b, t, width), BF16), jax.ShapeDtypeStruct((b, t, cw), BF16)],
        compiler_params=_params("parallel", "arbitrary"),
        name="attn_pool",
    )(q, kpad, vpad, bias, upad, pw, ps)


def _out_cross_kernel(x_ref, ya_ref, yb_ref, yc_ref, woa_ref, wob_ref, woc_ref, g_ref, wq_ref, qg_ref,
                      k_ref, v_ref, wo_ref, o_ref, att_ref, *, heads):
    x = (x_ref[...] + _dot(ya_ref[...], woa_ref[...]) + _dot(yb_ref[...], wob_ref[...])
         + _dot(yc_ref[...], woc_ref[...]))
    h = _rms(x, g_ref[...]).astype(BF16)
    q = _dot(h, wq_ref[...])
    hd = q.shape[1] // heads
    for hi in range(heads):
        cs = slice(hi * hd, (hi + 1) * hd)
        qh = (_rms(q[:, cs], qg_ref[...]) * (hd ** -0.5)).astype(BF16)
        s = _dot_nt(qh, k_ref[:, cs])
        e = jnp.exp(s - jnp.max(s, axis=-1, keepdims=True))
        oh = _dot(e.astype(BF16), v_ref[:, cs]) / jnp.sum(e, axis=-1, keepdims=True)
        att_ref[:, cs] = oh.astype(BF16)
    o_ref[...] = x + _dot(att_ref[...], wo_ref[...])


def _out_cross(x, ya, yb, yc, woa, wob, woc, g, wq, qg, k, v, wo, *, tm, heads):
    b, t, d = x.shape
    full = lambda a: pl.BlockSpec(a.shape, lambda i, j: (0,) * a.ndim)
    tile = lambda a: pl.BlockSpec((None, tm, a.shape[-1]), lambda i, j: (i, j, 0))
    mem = lambda a: pl.BlockSpec((None,) + a.shape[1:], lambda i, j: (i, 0, 0))
    return pl.pallas_call(
        functools.partial(_out_cross_kernel, heads=heads),
        grid=(b, t // tm),
        in_specs=[tile(x), tile(ya), tile(yb), tile(yc), full(woa), full(wob), full(woc), full(g), full(wq),
                  full(qg), mem(k), mem(v), full(wo)],
        out_specs=tile(x),
        out_shape=jax.ShapeDtypeStruct((b, t, d), F32),
        scratch_shapes=[pltpu.VMEM((tm, d), BF16)],
        compiler_params=_params("parallel", "parallel"),
        name="out_cross",
    )(x, ya, yb, yc, woa, wob, woc, g, wq, qg, k, v, wo)


def _mem_kv_kernel(mem_ref, g_ref, w_ref, kg_ref, k_ref, v_ref, *, heads):
    h = _rms(mem_ref[...], g_ref[...]).astype(BF16)
    kv = _dot(h, w_ref[...])
    d = kv.shape[1] // 2
    hd = d // heads
    for hi in range(heads):
        cs = slice(hi * hd, (hi + 1) * hd)
        k_ref[:, cs] = _rms(kv[:, cs], kg_ref[...]).astype(BF16)
    v_ref[...] = kv[:, d:].astype(BF16)


def _mem_kv(mem, g, wkv, kg, *, heads):
    b, m, d = mem.shape
    depth = wkv.shape[0]
    per_layer = lambda a: pl.BlockSpec((None,) + a.shape[1:], lambda l, i: (l,) + (0,) * (a.ndim - 1))
    out = pl.BlockSpec((None, None, m, d), lambda l, i: (l, i, 0, 0))
    return pl.pallas_call(
        functools.partial(_mem_kv_kernel, heads=heads),
        grid=(depth, b),
        in_specs=[pl.BlockSpec((None, m, d), lambda l, i: (i, 0, 0)), per_layer(g), per_layer(wkv), per_layer(kg)],
        out_specs=[out, out],
        out_shape=[jax.ShapeDtypeStruct((depth, b, m, d), BF16)] * 2,
        compiler_params=_params("parallel", "parallel"),
        name="mem_kv",
    )(mem, g, wkv, kg)


def _segment_ones(width):
    head = np.arange(width) // HEAD
    return jnp.asarray((head[:, None] == head[None, :]).astype(np.float32)).astype(BF16)


def _split_weight(w):
    hi = w.astype(BF16)
    return hi, (w - hi.astype(F32)).astype(BF16)


def _rel_bias_band(rel_bias):
    heads, n_rel = rel_bias.shape
    ext_len = BAND + CHUNK - 1
    ext = jnp.concatenate([rel_bias, jnp.broadcast_to(rel_bias[:, -1:], (heads, ext_len - n_rel))], axis=1)
    rev = ext[:, ::-1]
    bias = jnp.stack([rev[:, CHUNK - 1 - i:CHUNK - 1 - i + BAND] for i in range(CHUNK)], axis=1)
    return bias.reshape(heads // 2, 2 * CHUNK, BAND)


def _lora_weight(w_up, a_up, g_up):
    width = w_up.shape[-1]
    z = lambda r: jnp.zeros((r, width), F32)
    return jnp.concatenate([
        jnp.concatenate([w_up, z(LORA_W), z(LORA_W)], axis=1),
        jnp.concatenate([z(LORA_A), a_up, z(LORA_A)], axis=1),
        jnp.concatenate([z(LORA_G), z(LORA_G), g_up], axis=1)], axis=0)


def _pool_weight(pool_w):
    groups, cg, _ = pool_w.shape
    out = jnp.zeros((groups * cg, groups * cg), F32)
    for gi in range(groups):
        out = out.at[gi * cg:(gi + 1) * cg, gi * cg:(gi + 1) * cg].set(pool_w[gi])
    return out


def _attn_pool_layer(q, k, v, u, rel_bias, pool_w, pool_scale):
    band_pad = ((0, 0), (PREV_CHUNKS * CHUNK, 0), (0, 0))
    return _attn_pool(q, jnp.pad(k, band_pad), jnp.pad(v, band_pad), _rel_bias_band(rel_bias),
                      jnp.pad(u, ((0, 0), (CHUNK, 0), (0, 0))),
                      _pool_weight(pool_w).astype(BF16), pool_scale.reshape(1, -1))


def _rwkv_layer(pa, v_first, l, a_mu, a_w0, a_w_up, a_a0, a_a_up, a_g_up, a_k_k, a_k_a, a_r_k, a_gn_g, a_gn_b,
                a_v0, a_v_down, a_v_up, seg, tc=256):
    row = lambda a: a.reshape(1, -1)
    wl_hi, wl_lo = _split_weight(_lora_weight(a_w_up[l], a_a_up[l], a_g_up[l]))
    prm = {"mu": row(a_mu[l]), "wl_hi": wl_hi, "wl_lo": wl_lo, "w0": row(a_w0[l]),
           "a0": row(a_a0[l]), "k_k": row(a_k_k[l]), "k_a": row(a_k_a[l]), "r_k": row(a_r_k[l]),
           "gn_g": row(a_gn_g[l]), "gn_b": row(a_gn_b[l]), "seg": seg}
    if l == 0:
        return _rwkv(pa, None, prm, tc=tc)
    pad_lanes = LANES - LORA_V
    prm["v0"] = row(a_v0[l - 1])
    prm["vd_hi"], prm["vd_lo"] = _split_weight(jnp.pad(a_v_down[l - 1], ((0, 0), (0, pad_lanes))))
    prm["vu_hi"], prm["vu_lo"] = _split_weight(jnp.pad(a_v_up[l - 1], ((0, pad_lanes), (0, 0))))
    return _rwkv(pa, v_first, prm, tc=tc), v_first


def kernel(x, mem, norm_ffn1, ffn1_wi, ffn1_wo, norm_mix, w_in, w_out, a_mu, a_w0, a_w_up, a_a0, a_a_up, a_g_up,
           a_k_k, a_k_a, a_r_k, a_gn_g, a_gn_b, a_v0, a_v_down, a_v_up, b_q_gain, b_k_gain, b_rel_bias,
           c_pool_w, c_pool_scale, norm_cross, norm_mem, x_wq, x_wkv, x_wo, x_q_gain, x_k_gain,
           norm_ffn2, ffn2_wi, ffn2_wo):
    b, t, d = x.shape
    depth = w_in.shape[0]
    a_width = a_w0.shape[-1]
    a_proj = a_mu.shape[-1]
    b_width = b_rel_bias.shape[1] * HEAD
    c_width = c_pool_scale.shape[-1]
    x_heads = d // x_q_gain.shape[-1]
    n_tok = b * t
    row = lambda a: a.reshape(1, -1)

    seg_a = _segment_ones(a_width)
    seg_b = _segment_ones(b_width)
    mem_k, mem_v = _mem_kv(mem, norm_mem[:, None, :], x_wkv.astype(BF16), x_k_gain[:, None, :], heads=x_heads)

    ffn_tf = ffn1_wo.shape[1] // 2
    v_first = None
    for l in range(depth):
        x = _ffn(x.reshape(n_tok, d), row(norm_ffn1[l]), ffn1_wi[l].astype(BF16), ffn1_wo[l].astype(BF16),
                 tm=512, tf=ffn_tf)

        q_gain = row(jnp.tile(b_q_gain[l], b_width // HEAD)) * (HEAD ** -0.5)
        k_gain = row(jnp.tile(b_k_gain[l], b_width // HEAD))
        pa, q, k, v, u = _inproj(x, row(norm_mix[l]), w_in[l].astype(BF16), seg_b, q_gain, k_gain,
                                 tm=512, a_proj=a_proj, b_width=b_width, c_width=c_width)

        y_a, v_first = _rwkv_layer(pa.reshape(b, t, a_proj), v_first, l, a_mu, a_w0, a_w_up, a_a0, a_a_up, a_g_up,
                                   a_k_k, a_k_a, a_r_k, a_gn_g, a_gn_b, a_v0, a_v_down, a_v_up, seg_a)

        y_b, y_c = _attn_pool_layer(q.reshape(b, t, b_width), k.reshape(b, t, b_width), v.reshape(b, t, b_width),
                                    u.reshape(b, t, c_width), b_rel_bias[l], c_pool_w[l], c_pool_scale[l])

        wo16 = w_out[l].astype(BF16)
        x = _out_cross(x.reshape(b, t, d), y_a, y_b, y_c,
                       wo16[:a_width], wo16[a_width:a_width + b_width], wo16[a_width + b_width:],
                       row(norm_cross[l]), x_wq[l].astype(BF16), row(x_q_gain[l]),
                       mem_k[l], mem_v[l], x_wo[l].astype(BF16), tm=512, heads=x_heads)

        x = _ffn(x.reshape(n_tok, d), row(norm_ffn2[l]), ffn2_wi[l].astype(BF16), ffn2_wo[l].astype(BF16),
                 tm=512, tf=ffn_tf)
    return x.reshape(b, t, d)
```

```python
import functools

import numpy as np
import jax
import jax.numpy as jnp
from jax import lax
from jax.experimental import pallas as pl
from jax.experimental.pallas import tpu as pltpu

F32 = jnp.float32
BF16 = jnp.bfloat16
HI = lax.Precision.HIGHEST

LANES = 128
HEAD = 64
PAIR = 2 * HEAD
CHUNK = 64
PREV_CHUNKS = 8
BAND = (PREV_CHUNKS + 1) * CHUNK
REL_MAX = 256
POOL_WINDOWS = (2, 4, 8, 16)
LORA_W, LORA_A, LORA_G, LORA_V = 32, 32, 64, 32
RMS_EPS = 1e-6
GN_EPS = 64e-5
NEG_INF = -1e30
VMEM_LIMIT = 56 * 1024 * 1024
FFN_TM = 512
ROW_TM = 512
RWKV_TC = 256
ATTN_QB = 4


def _dot(a, b, precision=None):
    return jnp.dot(a, b, preferred_element_type=F32, precision=precision)


def _dot_nt(a, b):
    return lax.dot_general(a, b, (((1,), (1,)), ((), ())), preferred_element_type=F32)


def _dot_tn(a, b):
    return lax.dot_general(a, b, (((0,), (0,)), ((), ())), preferred_element_type=F32)


def _split(x, terms):
    parts = []
    for _ in range(terms):
        hi = x.astype(BF16)
        parts.append(hi)
        x = x - hi.astype(F32)
    return parts


def _dot_lhs_split(x, m16, terms):
    out = None
    for part in _split(x, terms):
        d = _dot(part, m16)
        out = d if out is None else out + d
    return out


def _dot_x3(x, w_hi, w_lo):
    x_hi, x_lo = _split(x, 2)
    return _dot(x_hi, w_hi) + (_dot(x_lo, w_hi) + _dot(x_hi, w_lo))


def _rms(x, g):
    return x * lax.rsqrt(jnp.mean(x * x, axis=-1, keepdims=True) + RMS_EPS) * g


def _sigmoid(x):
    return 1.0 / (1.0 + jnp.exp(-x))


def _params(*sem):
    return pltpu.CompilerParams(dimension_semantics=sem, vmem_limit_bytes=VMEM_LIMIT)


def _layer_spec(a, l):
    zeros = (0,) * (a.ndim - 1)
    return pl.BlockSpec((None,) + a.shape[1:], lambda *_: (l,) + zeros)


def _ffn_kernel(x_ref, g_ref, wg_ref, wu_ref, wo_ref, o_ref, h_ref, acc_ref):
    j = pl.program_id(1)

    @pl.when(j == 0)
    def _():
        h_ref[...] = _rms(x_ref[...], g_ref[...]).astype(BF16)
        acc_ref[...] = jnp.zeros_like(acc_ref)

    h = h_ref[...]
    gate = _dot(h, wg_ref[...])
    up = _dot(h, wu_ref[...])
    act = (gate * _sigmoid(gate) * up).astype(BF16)
    acc_ref[...] += _dot(act, wo_ref[...])

    @pl.when(j == pl.num_programs(1) - 1)
    def _():
        o_ref[...] = x_ref[...] + 0.5 * acc_ref[...]


def _ffn(x, g, wi, wo, l, *, tm, tf):
    n, d = x.shape
    dff = wo.shape[1]
    nf = dff // tf
    return pl.pallas_call(
        _ffn_kernel,
        grid=(n // tm, nf),
        in_specs=[
            pl.BlockSpec((tm, d), lambda i, j: (i, 0)),
            _layer_spec(g, l),
            pl.BlockSpec((None, d, tf), lambda i, j: (l, 0, j)),
            pl.BlockSpec((None, d, tf), lambda i, j: (l, 0, nf + j)),
            pl.BlockSpec((None, tf, d), lambda i, j: (l, j, 0)),
        ],
        out_specs=pl.BlockSpec((tm, d), lambda i, j: (i, 0)),
        out_shape=jax.ShapeDtypeStruct((n, d), F32),
        scratch_shapes=[pltpu.VMEM((tm, d), BF16), pltpu.VMEM((tm, d), F32)],
        compiler_params=_params("parallel", "arbitrary"),
        name="ffn",
    )(x, g, wi, wi, wo)


def _inproj_kernel(x_ref, g_ref, w_ref, seg_ref, qg_ref, kg_ref,
                   pa_ref, q_ref, k_ref, v_ref, u_ref, *, a_proj, b_width):
    h = _rms(x_ref[...], g_ref[...]).astype(BF16)
    p = _dot(h, w_ref[...])
    pa_ref[...] = p[:, :a_proj]
    q = p[:, a_proj:a_proj + b_width]
    k = p[:, a_proj + b_width:a_proj + 2 * b_width]
    seg = seg_ref[...]
    qms = _dot((q * q).astype(BF16), seg) * (1.0 / HEAD)
    kms = _dot((k * k).astype(BF16), seg) * (1.0 / HEAD)
    q_ref[...] = (q * lax.rsqrt(qms + RMS_EPS) * qg_ref[...]).astype(BF16)
    k_ref[...] = (k * lax.rsqrt(kms + RMS_EPS) * kg_ref[...]).astype(BF16)
    v_ref[...] = p[:, a_proj + 2 * b_width:a_proj + 3 * b_width].astype(BF16)
    u_ref[...] = p[:, a_proj + 3 * b_width:]


def _inproj(x, g, w, seg, qg, kg, l, *, tm, a_proj, b_width, c_width):
    n, d = x.shape
    full = lambda a: pl.BlockSpec(a.shape, lambda i: (0,) * a.ndim)
    row = lambda w_: pl.BlockSpec((tm, w_), lambda i: (i, 0))
    lay = lambda a: _layer_spec(a, l)
    return pl.pallas_call(
        functools.partial(_inproj_kernel, a_proj=a_proj, b_width=b_width),
        grid=(n // tm,),
        in_specs=[row(d), lay(g), lay(w), full(seg), lay(qg), lay(kg)],
        out_specs=[row(a_proj), row(b_width), row(b_width), row(b_width), row(c_width)],
        out_shape=[jax.ShapeDtypeStruct((n, a_proj), F32),
                   jax.ShapeDtypeStruct((n, b_width), BF16),
                   jax.ShapeDtypeStruct((n, b_width), BF16),
                   jax.ShapeDtypeStruct((n, b_width), BF16),
                   jax.ShapeDtypeStruct((n, c_width), F32)],
        compiler_params=_params("parallel"),
        name="in_proj",
    )(x, g, w, seg, qg, kg)


def _lane_lt_head(shape):
    return lax.broadcasted_iota(jnp.int32, shape, len(shape) - 1) < HEAD


def _blockdiag(x2):
    first = _lane_lt_head(x2.shape)
    zero = jnp.zeros_like(x2)
    return jnp.concatenate([jnp.where(first, x2, zero), jnp.where(first, zero, x2)], axis=0)


def _rwkv_chunk_maps(insts):
    c = CHUNK
    b16 = lambda t: t.astype(BF16)
    t_idx = lax.broadcasted_iota(jnp.int32, (c, PAIR), 0)
    s_idx = lax.broadcasted_iota(jnp.int32, (c, PAIR), 1) % HEAD
    strict = t_idx > s_idx
    incl = t_idx >= s_idx
    zero = jnp.zeros((c, PAIR), F32)
    eye = jnp.where(t_idx == s_idx, 1.0, 0.0)
    row = lax.broadcasted_iota(jnp.int32, (PAIR, PAIR), 0)
    col = lax.broadcasted_iota(jnp.int32, (PAIR, PAIR), 1)
    same_head = (row < HEAD) == (col < HEAD)
    zero2 = jnp.zeros((PAIR, PAIR), F32)

    s_all = [_dot_nt(b16(jnp.concatenate([qa, qr], axis=0)),
                     b16(jnp.concatenate([_blockdiag(kb), _blockdiag(kk)], axis=0)))
             for qa, qr, kb, kk, _, _, _, _ in insts]
    a_ab = [jnp.where(strict, s[:c, :PAIR], zero) for s in s_all]
    a_ak = [jnp.where(strict, s[:c, PAIR:], zero) for s in s_all]
    a_rb = [jnp.where(incl, s[c:, :PAIR], zero) for s in s_all]
    a_rk = [jnp.where(incl, s[c:, PAIR:], zero) for s in s_all]
    av = [_dot(b16(jnp.concatenate([ak, rk], axis=0)), b16(_blockdiag(inst[6])))
          for ak, rk, inst in zip(a_ak, a_rk, insts)]

    apow = a_ab
    tinv = [eye + a for a in a_ab]
    for _ in range(5):
        apow = [_dot(b16(a), b16(_blockdiag(a))) for a in apow]
        tinv = [t + _dot(b16(t), b16(_blockdiag(a))) for t, a in zip(tinv, apow)]

    r1 = [_dot(b16(t), b16(jnp.concatenate([_blockdiag(inst[0]), _blockdiag(a[:c])], axis=1)))
          for t, a, inst in zip(tinv, av, insts)]
    r2 = [_dot(b16(rb), b16(jnp.concatenate([_blockdiag(r[:, :PAIR]), _blockdiag(r[:, PAIR:])], axis=1)))
          for rb, r in zip(a_rb, r1)]
    out = []
    for r1_i, r2_i, av_i, (qa, qr, kb, kk, kbe, kke, v, wlast_row) in zip(r1, r2, av, insts):
        qa_p, u0 = r1_i[:, :PAIR], r1_i[:, PAIR:]
        m = jnp.where(same_head, _dot_tn(b16(qa_p), b16(kbe)), zero2)
        m = m + jnp.where(row == col, jnp.broadcast_to(wlast_row, (PAIR, PAIR)), zero2)
        n = jnp.where(same_head,
                      _dot_tn(b16(jnp.concatenate([u0, v], axis=0)),
                              b16(jnp.concatenate([kbe, kke], axis=0))), zero2)
        out.append((qr + r2_i[:, :PAIR], r2_i[:, PAIR:] + av_i[c:], m, n))
    return out


def _rwkv_kernel(*refs, has_vres, tc, width):
    if has_vres:
        (pa_ref, mu_ref, wlh_ref, wll_ref, w0_ref, a0_ref, kk_ref, ka_ref, rk_ref, gng_ref, gnb_ref, seg_ref,
         vf_ref, v0_ref, vdh_ref, vdl_ref, vuh_ref, vul_ref, y_ref, prev_ref, s_ref, yraw_ref) = refs
    else:
        (pa_ref, mu_ref, wlh_ref, wll_ref, w0_ref, a0_ref, kk_ref, ka_ref, rk_ref, gng_ref, gnb_ref, seg_ref,
         y_ref, vf_ref, prev_ref, s_ref, yraw_ref) = refs
    n_pairs = width // PAIR
    n_chunks = tc // CHUNK

    @pl.when(pl.program_id(1) == 0)
    def _():
        prev_ref[...] = jnp.zeros_like(prev_ref)
        s_ref[...] = jnp.zeros_like(s_ref)

    p = pa_ref[...]
    row = lax.broadcasted_iota(jnp.int32, p.shape, 0)
    shifted = jnp.where(row == 0, jnp.broadcast_to(prev_ref[0:1, :], p.shape), pltpu.roll(p, 1, axis=0))
    prev_ref[0:1, :] = p[tc - 1:tc, :]
    p = p + mu_ref[...] * (shifted - p)

    r = p[:, :width]
    k = p[:, width:2 * width]
    v = p[:, 2 * width:3 * width]
    lo_in = p[:, 3 * width:]
    lane = lax.broadcasted_iota(jnp.int32, lo_in.shape, 1)
    lo_act = jnp.where(lane < LORA_W, jnp.tanh(lo_in),
                       jnp.where(lane < LORA_W + LORA_A, lo_in, _sigmoid(lo_in)))
    lo = _dot_x3(lo_act, wlh_ref[...], wll_ref[...])
    wz = w0_ref[...] + lo[:, :width]
    w = -(jnp.maximum(-wz, 0.0) + jnp.log(1.0 + jnp.exp(-jnp.abs(wz)))) - 0.5
    lw = -jnp.exp(w)
    a = _sigmoid(a0_ref[...] + lo[:, width:2 * width])
    g = lo[:, 2 * width:]

    if has_vres:
        v_lo = _dot_x3(v, vdh_ref[...], vdl_ref[...])
        gate = _sigmoid(v0_ref[...] + _dot_x3(v_lo, vuh_ref[...], vul_ref[...]))
        v = v + (vf_ref[...] - v) * gate
    else:
        vf_ref[...] = v

    seg = seg_ref[...]
    kkx = k * kk_ref[...]
    kk = kkx / jnp.maximum(jnp.sqrt(_dot((kkx * kkx).astype(BF16), seg)), 1e-12)
    kmod = k * (1.0 + (a - 1.0) * ka_ref[...])

    ti = lax.broadcasted_iota(jnp.int32, (tc, tc), 0)
    tj = lax.broadcasted_iota(jnp.int32, (tc, tc), 1)
    tri = jnp.where(((ti // CHUNK) == (tj // CHUNK)) & (tj <= ti), 1.0, 0.0).astype(BF16)
    cum = None
    for part in _split(lw, 3):
        d = _dot(tri, part)
        cum = d if cum is None else cum + d
    tot = jnp.concatenate(
        [jnp.broadcast_to(cum[(ci + 1) * CHUNK - 1:(ci + 1) * CHUNK, :], (CHUNK, width)) for ci in range(n_chunks)],
        axis=0)
    w_inc = jnp.exp(cum)
    w_exc = jnp.exp(cum - lw)
    w_inv = jnp.exp(-cum)
    w_end = jnp.exp(tot - cum)
    w_tot = jnp.exp(tot)

    kka = kk * a
    qa_all = -kk * w_exc
    qr_all = r * w_inc
    kb_all = kka * w_inv
    kk_all = kmod * w_inv
    kbe_all = kka * w_end
    kke_all = kmod * w_end

    insts = []
    for ci in range(n_chunks):
        rs = slice(ci * CHUNK, (ci + 1) * CHUNK)
        for pi in range(n_pairs):
            ls = slice(pi * PAIR, (pi + 1) * PAIR)
            insts.append((qa_all[rs, ls], qr_all[rs, ls], kb_all[rs, ls], kk_all[rs, ls],
                          kbe_all[rs, ls], kke_all[rs, ls], v[rs, ls], w_tot[ci * CHUNK:ci * CHUNK + 1, ls]))
    maps = _rwkv_chunk_maps(insts)
    states = [s_ref[pi] for pi in range(n_pairs)]
    for ci in range(n_chunks):
        for pi in range(n_pairs):
            qr_p, y0, m, n = maps[ci * n_pairs + pi]
            s16 = states[pi].astype(BF16)
            yraw_ref[ci * CHUNK:(ci + 1) * CHUNK, pi * PAIR:(pi + 1) * PAIR] = _dot_nt(qr_p.astype(BF16), s16) + y0
            states[pi] = _dot(s16, m.astype(BF16)) + n
    for pi in range(n_pairs):
        s_ref[pi] = states[pi]

    y = yraw_ref[...]
    mean = _dot_lhs_split(y, seg, 2) * (1.0 / HEAD)
    yc = y - mean
    var = _dot_lhs_split(yc * yc, seg, 2) * (1.0 / HEAD)
    yn = yc * lax.rsqrt(var + GN_EPS) * gng_ref[...] + gnb_ref[...]
    bonus = _dot((r * kmod * rk_ref[...]).astype(BF16), seg) * v
    y_ref[...] = ((yn + bonus) * g).astype(y_ref.dtype)


def _rwkv(pa, vfirst, prm, l, *, tc):
    b, t, a_proj = pa.shape
    width = prm["w0"].shape[-1]
    has_vres = vfirst is not None
    full = lambda a: pl.BlockSpec(a.shape, lambda i, j: (0,) * a.ndim)
    tile = lambda w_: pl.BlockSpec((None, tc, w_), lambda i, j: (i, j, 0))
    names = ["mu", "wl_hi", "wl_lo", "w0", "a0", "k_k", "k_a", "r_k", "gn_g", "gn_b"]
    args = [pa] + [prm[nm] for nm in names] + [prm["seg"]]
    in_specs = [tile(a_proj)] + [_layer_spec(prm[nm], l) for nm in names] + [full(prm["seg"])]
    y_shape = jax.ShapeDtypeStruct((b, t, width), BF16)
    if has_vres:
        extra = [prm["v0"], prm["vd_hi"], prm["vd_lo"], prm["vu_hi"], prm["vu_lo"]]
        args += [vfirst] + extra
        in_specs += [tile(width)] + [_layer_spec(a, l - 1) for a in extra]
        out_specs, out_shape = tile(width), y_shape
    else:
        out_specs = [tile(width), tile(width)]
        out_shape = [y_shape, jax.ShapeDtypeStruct((b, t, width), F32)]
    return pl.pallas_call(
        functools.partial(_rwkv_kernel, has_vres=has_vres, tc=tc, width=width),
        grid=(b, t // tc),
        in_specs=in_specs,
        out_specs=out_specs,
        out_shape=out_shape,
        scratch_shapes=[pltpu.VMEM((8, a_proj), F32),
                        pltpu.VMEM((width // PAIR, PAIR, PAIR), F32),
                        pltpu.VMEM((tc, width), F32)],
        compiler_params=_params("parallel", "arbitrary"),
        name="rwkv",
    )(*args)


def _attn_pool_kernel(q_ref, k_ref, v_ref, bias_ref, u_ref, pw_ref, ps_ref, yb_ref, yc_ref,
                      kpad_ref, vpad_ref, upad_ref, *, width, qb):
    step = pl.program_id(1)
    rows = qb * CHUNK
    win_rows = (PREV_CHUNKS + qb) * CHUNK
    seq = k_ref.shape[0]

    @pl.when(step == 0)
    def _():
        kpad_ref[:PREV_CHUNKS * CHUNK, :] = jnp.zeros((PREV_CHUNKS * CHUNK, width), BF16)
        vpad_ref[:PREV_CHUNKS * CHUNK, :] = jnp.zeros((PREV_CHUNKS * CHUNK, width), BF16)
        kpad_ref[PREV_CHUNKS * CHUNK:, :] = k_ref[...]
        vpad_ref[PREV_CHUNKS * CHUNK:, :] = v_ref[...]
        upad_ref[:CHUNK, :] = jnp.zeros((CHUNK, upad_ref.shape[1]), F32)
        upad_ref[CHUNK:, :] = u_ref[...]

    start = pl.multiple_of(step * rows, rows)
    q = q_ref[...]
    slot = lax.broadcasted_iota(jnp.int32, (1, win_rows), 1)
    before_start = jnp.where(slot >= PREV_CHUNKS * CHUNK - start, 0.0, NEG_INF)
    first = _lane_lt_head((rows, PAIR))
    n_pairs = width // PAIR
    scores = []
    for pi in range(n_pairs):
        ls = slice(pi * PAIR, (pi + 1) * PAIR)
        q2 = q[:, ls]
        zq = jnp.zeros_like(q2)
        qs = jnp.concatenate([jnp.where(first, q2, zq), jnp.where(first, zq, q2)], axis=0)
        scores.append(_dot_nt(qs, kpad_ref[pl.ds(start, win_rows), ls]) + bias_ref[pi] + before_start)
    probs, sums = [], []
    for s in scores:
        e = jnp.exp(s - jnp.max(s, axis=-1, keepdims=True))
        sums.append(jnp.sum(e, axis=-1, keepdims=True))
        probs.append(e.astype(BF16))
    for pi in range(n_pairs):
        ls = slice(pi * PAIR, (pi + 1) * PAIR)
        o = _dot(probs[pi], vpad_ref[pl.ds(start, win_rows), ls]) / sums[pi]
        yb_ref[:, ls] = jnp.where(first, o[:rows], o[rows:]).astype(yb_ref.dtype)

    x = upad_ref[pl.ds(start, rows + CHUNK), :]
    cw = x.shape[1]
    acc, span, win_sums = x, 1, []
    for win in POOL_WINDOWS:
        while span < win:
            acc = acc + pltpu.roll(acc, span, axis=0)
            span *= 2
        win_sums.append(acc)
    lane = lax.broadcasted_iota(jnp.int32, (rows, cw), 1)
    t1 = (lax.broadcasted_iota(jnp.int32, (rows, cw), 0) + start + 1).astype(F32)
    pooled = jnp.zeros((rows, cw), F32)
    for gi, win in enumerate(POOL_WINDOWS):
        grp = (lane >= gi * HEAD) & (lane < (gi + 1) * HEAD)
        pooled = jnp.where(grp, win_sums[gi][CHUNK:] / jnp.minimum(t1, float(win)), pooled)
    pooled = pooled - x[CHUNK:]
    yc_ref[...] = (_dot(pooled.astype(BF16), pw_ref[...]) * ps_ref[...]).astype(yc_ref.dtype)


def _attn_pool(q, k, v, bias, u, pw, ps, l, *, qb):
    b, t, width = q.shape
    cw = u.shape[-1]
    rows = qb * CHUNK
    lay = lambda a: _layer_spec(a, l)
    seq = lambda a: pl.BlockSpec((None,) + a.shape[1:], lambda i, j: (i, 0, 0))
    tile = lambda w_: pl.BlockSpec((None, rows, w_), lambda i, j: (i, j, 0))
    return pl.pallas_call(
        functools.partial(_attn_pool_kernel, width=width, qb=qb),
        grid=(b, t // rows),
        in_specs=[tile(width), seq(k), seq(v), lay(bias), seq(u), lay(pw), lay(ps)],
        out_specs=[tile(width), tile(cw)],
        out_shape=[jax.ShapeDtypeStruct((b, t, width), BF16), jax.ShapeDtypeStruct((b, t, cw), BF16)],
        scratch_shapes=[pltpu.VMEM((t + PREV_CHUNKS * CHUNK, width), BF16),
                        pltpu.VMEM((t + PREV_CHUNKS * CHUNK, width), BF16),
                        pltpu.VMEM((t + CHUNK, cw), F32)],
        compiler_params=_params("parallel", "arbitrary"),
        name="attn_pool",
    )(q, k, v, bias, u, pw, ps)


def _out_cross_kernel(x_ref, ya_ref, yb_ref, yc_ref, woa_ref, wob_ref, woc_ref, g_ref, wq_ref, qg_ref,
                      k_ref, v_ref, wo_ref, o_ref, att_ref, *, heads):
    x = (x_ref[...] + _dot(ya_ref[...], woa_ref[...]) + _dot(yb_ref[...], wob_ref[...])
         + _dot(yc_ref[...], woc_ref[...]))
    h = _rms(x, g_ref[...]).astype(BF16)
    q = _dot(h, wq_ref[...])
    hd = q.shape[1] // heads
    for hi in range(heads):
        cs = slice(hi * hd, (hi + 1) * hd)
        qh = (_rms(q[:, cs], qg_ref[...]) * (hd ** -0.5)).astype(BF16)
        s = _dot_nt(qh, k_ref[:, cs])
        e = jnp.exp(s - jnp.max(s, axis=-1, keepdims=True))
        oh = _dot(e.astype(BF16), v_ref[:, cs]) / jnp.sum(e, axis=-1, keepdims=True)
        att_ref[:, cs] = oh.astype(BF16)
    o_ref[...] = x + _dot(att_ref[...], wo_ref[...])


def _out_cross(x, ya, yb, yc, w_out, g, wq, qg, k, v, wo, l, *, tm, heads):
    b, t, d = x.shape
    lay = lambda a: _layer_spec(a, l)
    tile = lambda a: pl.BlockSpec((None, tm, a.shape[-1]), lambda i, j: (i, j, 0))
    mem = lambda a: pl.BlockSpec((None, None) + a.shape[2:], lambda i, j: (l, i, 0, 0))
    wa, wb, wc = ya.shape[-1], yb.shape[-1], yc.shape[-1]
    assert wa == wb and (wa + wb) % wc == 0, "row blocks of w_out must be block-aligned"
    w_rows = lambda rows, blk: pl.BlockSpec((None, rows, d), lambda i, j: (l, blk, 0))
    return pl.pallas_call(
        functools.partial(_out_cross_kernel, heads=heads),
        grid=(b, t // tm),
        in_specs=[tile(x), tile(ya), tile(yb), tile(yc), w_rows(wa, 0), w_rows(wb, 1), w_rows(wc, (wa + wb) // wc),
                  lay(g), lay(wq), lay(qg), mem(k), mem(v), lay(wo)],
        out_specs=tile(x),
        out_shape=jax.ShapeDtypeStruct((b, t, d), F32),
        scratch_shapes=[pltpu.VMEM((tm, d), BF16)],
        compiler_params=_params("parallel", "parallel"),
        name="out_cross",
    )(x, ya, yb, yc, w_out, w_out, w_out, g, wq, qg, k, v, wo)


def _mem_kv_kernel(mem_ref, g_ref, w_ref, kg_ref, k_ref, v_ref, *, heads):
    h = _rms(mem_ref[...], g_ref[...]).astype(BF16)
    kv = _dot(h, w_ref[...])
    d = kv.shape[1] // 2
    hd = d // heads
    for hi in range(heads):
        cs = slice(hi * hd, (hi + 1) * hd)
        k_ref[:, cs] = _rms(kv[:, cs], kg_ref[...]).astype(BF16)
    v_ref[...] = kv[:, d:].astype(BF16)


def _mem_kv(mem, g, wkv, kg, *, heads):
    b, m, d = mem.shape
    depth = wkv.shape[0]
    per_layer = lambda a: pl.BlockSpec((None,) + a.shape[1:], lambda l, i: (l,) + (0,) * (a.ndim - 1))
    out = pl.BlockSpec((None, None, m, d), lambda l, i: (l, i, 0, 0))
    return pl.pallas_call(
        functools.partial(_mem_kv_kernel, heads=heads),
        grid=(depth, b),
        in_specs=[pl.BlockSpec((None, m, d), lambda l, i: (i, 0, 0)), per_layer(g), per_layer(wkv), per_layer(kg)],
        out_specs=[out, out],
        out_shape=[jax.ShapeDtypeStruct((depth, b, m, d), BF16)] * 2,
        compiler_params=_params("parallel", "parallel"),
        name="mem_kv",
    )(mem, g, wkv, kg)


def _segment_ones(width):
    head = np.arange(width) // HEAD
    return jnp.asarray((head[:, None] == head[None, :]).astype(np.float32)).astype(BF16)


def _split_weight(w):
    hi = w.astype(BF16)
    return hi, (w - hi.astype(F32)).astype(BF16)


def _rel_bias_band(rel_bias):
    heads, n_rel = rel_bias.shape
    ext_len = BAND + CHUNK - 1
    ext = jnp.concatenate([rel_bias, jnp.broadcast_to(rel_bias[:, -1:], (heads, ext_len - n_rel))], axis=1)
    rev = ext[:, ::-1]
    return jnp.stack([rev[:, CHUNK - 1 - i:CHUNK - 1 - i + BAND] for i in range(CHUNK)], axis=1)


def _rel_bias_window(rel_bias, qb):
    band = _rel_bias_band(rel_bias)
    heads = band.shape[0]
    per_chunk = [jnp.pad(band, ((0, 0), (0, 0), (qi * CHUNK, (qb - 1 - qi) * CHUNK)), constant_values=NEG_INF)
                 for qi in range(qb)]
    win = jnp.stack(per_chunk, axis=1)
    return win.reshape(heads // 2, 2 * qb * CHUNK, (PREV_CHUNKS + qb) * CHUNK)


def _lora_weight(w_up, a_up, g_up):
    depth, _, width = w_up.shape
    z = lambda r: jnp.zeros((depth, r, width), F32)
    return jnp.concatenate([
        jnp.concatenate([w_up, z(LORA_W), z(LORA_W)], axis=2),
        jnp.concatenate([z(LORA_A), a_up, z(LORA_A)], axis=2),
        jnp.concatenate([z(LORA_G), z(LORA_G), g_up], axis=2)], axis=1)


def _pool_weight(pool_w):
    depth, groups, cg, _ = pool_w.shape
    rows = []
    for gi in range(groups):
        blocks = [pool_w[:, gi] if gj == gi else jnp.zeros((depth, cg, cg), pool_w.dtype) for gj in range(groups)]
        rows.append(jnp.concatenate(blocks, axis=2))
    return jnp.concatenate(rows, axis=1)


def _stack_rows(a):
    return a.reshape(a.shape[0], 1, -1)


def _attn_pool_params(rel_bias, pool_w, pool_scale, qb):
    return {"bias": jax.vmap(functools.partial(_rel_bias_window, qb=qb))(rel_bias),
            "pw": _pool_weight(pool_w).astype(BF16), "ps": _stack_rows(pool_scale)}


def _rwkv_params(a_mu, a_w0, a_w_up, a_a0, a_a_up, a_g_up, a_k_k, a_k_a, a_r_k, a_gn_g, a_gn_b,
                 a_v0, a_v_down, a_v_up):
    wl_hi, wl_lo = _split_weight(_lora_weight(a_w_up, a_a_up, a_g_up))
    pad_lanes = LANES - LORA_V
    vd_hi, vd_lo = _split_weight(jnp.pad(a_v_down, ((0, 0), (0, 0), (0, pad_lanes))))
    vu_hi, vu_lo = _split_weight(jnp.pad(a_v_up, ((0, 0), (0, pad_lanes), (0, 0))))
    return {"mu": _stack_rows(a_mu), "wl_hi": wl_hi, "wl_lo": wl_lo, "w0": _stack_rows(a_w0),
            "a0": _stack_rows(a_a0), "k_k": _stack_rows(a_k_k), "k_a": _stack_rows(a_k_a),
            "r_k": _stack_rows(a_r_k), "gn_g": _stack_rows(a_gn_g), "gn_b": _stack_rows(a_gn_b),
            "seg": _segment_ones(a_w0.shape[-1]), "v0": _stack_rows(a_v0),
            "vd_hi": vd_hi, "vd_lo": vd_lo, "vu_hi": vu_hi, "vu_lo": vu_lo}


def kernel(x, mem, norm_ffn1, ffn1_wi, ffn1_wo, norm_mix, w_in, w_out, a_mu, a_w0, a_w_up, a_a0, a_a_up, a_g_up,
           a_k_k, a_k_a, a_r_k, a_gn_g, a_gn_b, a_v0, a_v_down, a_v_up, b_q_gain, b_k_gain, b_rel_bias,
           c_pool_w, c_pool_scale, norm_cross, norm_mem, x_wq, x_wkv, x_wo, x_q_gain, x_k_gain,
           norm_ffn2, ffn2_wi, ffn2_wo):
    b, t, d = x.shape
    depth = w_in.shape[0]
    a_proj = a_mu.shape[-1]
    b_width = b_rel_bias.shape[1] * HEAD
    c_width = c_pool_scale.shape[-1]
    x_heads = d // x_q_gain.shape[-1]
    n_tok = b * t
    b16 = lambda a: a.astype(BF16)

    ffn1 = (_stack_rows(norm_ffn1), b16(ffn1_wi), b16(ffn1_wo))
    ffn2 = (_stack_rows(norm_ffn2), b16(ffn2_wi), b16(ffn2_wo))
    w_in16, w_out16, wq16, wo16 = b16(w_in), b16(w_out), b16(x_wq), b16(x_wo)
    q_gain = _stack_rows(jnp.tile(b_q_gain, (1, b_width // HEAD))) * (HEAD ** -0.5)
    k_gain = _stack_rows(jnp.tile(b_k_gain, (1, b_width // HEAD)))
    seg_b = _segment_ones(b_width)
    rwkv_prm = _rwkv_params(a_mu, a_w0, a_w_up, a_a0, a_a_up, a_g_up, a_k_k, a_k_a, a_r_k, a_gn_g, a_gn_b,
                            a_v0, a_v_down, a_v_up)
    ap_prm = _attn_pool_params(b_rel_bias, c_pool_w, c_pool_scale, ATTN_QB)
    mem_k, mem_v = _mem_kv(mem, _stack_rows(norm_mem), b16(x_wkv), _stack_rows(x_k_gain), heads=x_heads)

    ffn_tf = ffn1_wo.shape[1] // 2
    seq = lambda a: a.reshape(b, t, a.shape[-1])
    x = x.reshape(n_tok, d)
    v_first = None
    for l in range(depth):
        x = _ffn(x, *ffn1, l, tm=FFN_TM, tf=ffn_tf)
        pa, q, k, v, u = _inproj(x, _stack_rows(norm_mix), w_in16, seg_b, q_gain, k_gain, l,
                                 tm=ROW_TM, a_proj=a_proj, b_width=b_width, c_width=c_width)
        if l == 0:
            y_a, v_first = _rwkv(seq(pa), None, rwkv_prm, l, tc=RWKV_TC)
        else:
            y_a = _rwkv(seq(pa), v_first, rwkv_prm, l, tc=RWKV_TC)
        y_b, y_c = _attn_pool(seq(q), seq(k), seq(v), ap_prm["bias"], seq(u), ap_prm["pw"], ap_prm["ps"], l,
                              qb=ATTN_QB)
        x = _out_cross(seq(x), y_a, y_b, y_c, w_out16, _stack_rows(norm_cross), wq16, _stack_rows(x_q_gain),
                       mem_k, mem_v, wo16, l, tm=ROW_TM, heads=x_heads)
        x = _ffn(x.reshape(n_tok, d), *ffn2, l, tm=FFN_TM, tf=ffn_tf)
    return x.reshape(b, t, d)
```

```python
import functools

import numpy as np
import jax
import jax.numpy as jnp
from jax import lax
from jax.experimental import pallas as pl
from jax.experimental.pallas import tpu as pltpu

F32 = jnp.float32
BF16 = jnp.bfloat16

LANES = 128
HEAD = 64
PAIR = 2 * HEAD
CHUNK = 64
INV_BLOCK = 8
PREV_CHUNKS = 8
BAND = (PREV_CHUNKS + 1) * CHUNK
REL_MAX = 256
POOL_WINDOWS = (2, 4, 8, 16)
LORA_W, LORA_A, LORA_G, LORA_V = 32, 32, 64, 32
RMS_EPS = 1e-6
GN_EPS = 64e-5
NEG_INF = -1e30
VMEM_LIMIT = 56 * 1024 * 1024
FFN_TM = 1024
FFN_TF = 256
ROW_TM = 512
RWKV_TC = 512
ATTN_QB = 4


def _dot(a, b, precision=None):
    return jnp.dot(a, b, preferred_element_type=F32, precision=precision)


def _dot_nt(a, b):
    return lax.dot_general(a, b, (((1,), (1,)), ((), ())), preferred_element_type=F32)


def _dot_tn(a, b):
    return lax.dot_general(a, b, (((0,), (0,)), ((), ())), preferred_element_type=F32)


def _split(x, terms):
    parts = []
    for _ in range(terms):
        hi = x.astype(BF16)
        parts.append(hi)
        x = x - hi.astype(F32)
    return parts


def _dot_x3(x, w_hi, w_lo):
    x_hi, x_lo = _split(x, 2)
    return _dot(x_hi, w_hi) + (_dot(x_lo, w_hi) + _dot(x_hi, w_lo))


def _rms(x, g):
    return x * lax.rsqrt(jnp.mean(x * x, axis=-1, keepdims=True) + RMS_EPS) * g


def _sigmoid(x):
    return 1.0 / (1.0 + jnp.exp(-x))


def _params(*sem):
    return pltpu.CompilerParams(dimension_semantics=sem, vmem_limit_bytes=VMEM_LIMIT)


def _layer_spec(a, l):
    zeros = (0,) * (a.ndim - 1)
    return pl.BlockSpec((None,) + a.shape[1:], lambda *_: (l,) + zeros)


def _ffn_kernel(x_ref, g_ref, wi_ref, wo_ref, o_ref, *, tf):
    x = x_ref[...]
    h = _rms(x, g_ref[...]).astype(BF16)
    dff = wo_ref.shape[0]
    acc = None
    for c0 in range(0, dff, tf):
        gate = _dot(h, wi_ref[:, c0:c0 + tf])
        up = _dot(h, wi_ref[:, dff + c0:dff + c0 + tf])
        act = (gate * _sigmoid(gate) * up).astype(BF16)
        part = _dot(act, wo_ref[c0:c0 + tf, :])
        acc = part if acc is None else acc + part
    o_ref[...] = x + 0.5 * acc


def _ffn(x, g, wi, wo, l, *, tm, tf):
    n, d = x.shape
    resident = lambda a: pl.BlockSpec((None,) + a.shape[1:], lambda i: (l, 0, 0), pipeline_mode=pl.Buffered(1))
    return pl.pallas_call(
        functools.partial(_ffn_kernel, tf=tf),
        grid=(n // tm,),
        in_specs=[pl.BlockSpec((tm, d), lambda i: (i, 0)), _layer_spec(g, l), resident(wi), resident(wo)],
        out_specs=pl.BlockSpec((tm, d), lambda i: (i, 0)),
        out_shape=jax.ShapeDtypeStruct((n, d), F32),
        compiler_params=_params("parallel"),
        name="ffn",
    )(x, g, wi, wo)


def _inproj_kernel(x_ref, g_ref, w_ref, seg_ref, qg_ref, kg_ref,
                   pa_ref, q_ref, k_ref, v_ref, u_ref, *, a_proj, b_width):
    h = _rms(x_ref[...], g_ref[...]).astype(BF16)
    p = _dot(h, w_ref[...])
    pa_ref[...] = p[:, :a_proj]
    q = p[:, a_proj:a_proj + b_width]
    k = p[:, a_proj + b_width:a_proj + 2 * b_width]
    seg = seg_ref[...]
    qms = _dot((q * q).astype(BF16), seg) * (1.0 / HEAD)
    kms = _dot((k * k).astype(BF16), seg) * (1.0 / HEAD)
    q_ref[...] = (q * lax.rsqrt(qms + RMS_EPS) * qg_ref[...]).astype(BF16)
    k_ref[...] = (k * lax.rsqrt(kms + RMS_EPS) * kg_ref[...]).astype(BF16)
    v_ref[...] = p[:, a_proj + 2 * b_width:a_proj + 3 * b_width].astype(BF16)
    u_ref[...] = p[:, a_proj + 3 * b_width:]


def _inproj(x, g, w, seg, qg, kg, l, *, tm, a_proj, b_width, c_width):
    n, d = x.shape
    full = lambda a: pl.BlockSpec(a.shape, lambda i: (0,) * a.ndim)
    row = lambda w_: pl.BlockSpec((tm, w_), lambda i: (i, 0))
    lay = lambda a: _layer_spec(a, l)
    return pl.pallas_call(
        functools.partial(_inproj_kernel, a_proj=a_proj, b_width=b_width),
        grid=(n // tm,),
        in_specs=[row(d), lay(g), lay(w), full(seg), lay(qg), lay(kg)],
        out_specs=[row(a_proj), row(b_width), row(b_width), row(b_width), row(c_width)],
        out_shape=[jax.ShapeDtypeStruct((n, a_proj), F32),
                   jax.ShapeDtypeStruct((n, b_width), BF16),
                   jax.ShapeDtypeStruct((n, b_width), BF16),
                   jax.ShapeDtypeStruct((n, b_width), BF16),
                   jax.ShapeDtypeStruct((n, c_width), F32)],
        compiler_params=_params("parallel"),
        name="in_proj",
    )(x, g, w, seg, qg, kg)


def _lane_lt_head(shape):
    return lax.broadcasted_iota(jnp.int32, shape, len(shape) - 1) < HEAD


def _head_sum(x):
    out = []
    for ls in range(0, x.shape[1], PAIR):
        xs = x[:, ls:ls + PAIR]
        first = _lane_lt_head(xs.shape)
        sum_a = jnp.sum(jnp.where(first, xs, 0.0), axis=-1, keepdims=True)
        sum_b = jnp.sum(jnp.where(first, 0.0, xs), axis=-1, keepdims=True)
        out.append(jnp.where(first, sum_a, sum_b))
    return jnp.concatenate(out, axis=1)


def _blockdiag(x2):
    first = _lane_lt_head(x2.shape)
    zero = jnp.zeros_like(x2)
    return jnp.concatenate([jnp.where(first, x2, zero), jnp.where(first, zero, x2)], axis=0)


def _rwkv_chunk_maps(insts):
    c = CHUNK
    b16 = lambda t: t.astype(BF16)
    t_idx = lax.broadcasted_iota(jnp.int32, (c, PAIR), 0)
    s_idx = lax.broadcasted_iota(jnp.int32, (c, PAIR), 1) % HEAD
    strict = t_idx > s_idx
    incl = t_idx >= s_idx
    zero = jnp.zeros((c, PAIR), F32)
    eye = jnp.where(t_idx == s_idx, 1.0, 0.0)
    row = lax.broadcasted_iota(jnp.int32, (PAIR, PAIR), 0)
    col = lax.broadcasted_iota(jnp.int32, (PAIR, PAIR), 1)
    same_head = (row < HEAD) == (col < HEAD)
    zero2 = jnp.zeros((PAIR, PAIR), F32)

    s_all = [_dot_nt(b16(jnp.concatenate([qa, qr], axis=0)),
                     b16(jnp.concatenate([_blockdiag(kb), _blockdiag(kk)], axis=0)))
             for qa, qr, kb, kk, _, _, _, _ in insts]
    a_ab = [jnp.where(strict, s[:c, :PAIR], zero) for s in s_all]
    a_ak = [jnp.where(strict, s[:c, PAIR:], zero) for s in s_all]
    a_rb = [jnp.where(incl, s[c:, :PAIR], zero) for s in s_all]
    a_rk = [jnp.where(incl, s[c:, PAIR:], zero) for s in s_all]
    av = [_dot(b16(jnp.concatenate([ak, rk], axis=0)), b16(_blockdiag(inst[6])))
          for ak, rk, inst in zip(a_ak, a_rk, insts)]

    same_block = lambda blk: (t_idx // blk) == (s_idx // blk)
    a0 = [jnp.where(same_block(INV_BLOCK), a, zero) for a in a_ab]
    a2 = [_dot(b16(a), b16(_blockdiag(a))) for a in a0]
    p1 = [eye + a for a in a0]
    st = [_dot(b16(jnp.concatenate([sq, p], axis=0)), b16(_blockdiag(sq))) for sq, p in zip(a2, p1)]
    p2 = [p + s[c:] for p, s in zip(p1, st)]
    tinv = [p + _dot(b16(p), b16(_blockdiag(s[:c]))) for p, s in zip(p2, st)]
    blk = INV_BLOCK
    while blk < c:
        off_diag = same_block(2 * blk) & jnp.logical_not(same_block(blk))
        x1 = [_dot(b16(jnp.where(off_diag, a, zero)), b16(_blockdiag(t))) for a, t in zip(a_ab, tinv)]
        tinv = [t + _dot(b16(t), b16(_blockdiag(x))) for t, x in zip(tinv, x1)]
        blk *= 2

    r1 = [_dot(b16(t), b16(jnp.concatenate([_blockdiag(inst[0]), _blockdiag(a[:c])], axis=1)))
          for t, a, inst in zip(tinv, av, insts)]
    r2 = [_dot(b16(rb), b16(jnp.concatenate([_blockdiag(r[:, :PAIR]), _blockdiag(r[:, PAIR:])], axis=1)))
          for rb, r in zip(a_rb, r1)]
    out = []
    for r1_i, r2_i, av_i, (qa, qr, kb, kk, kbe, kke, v, wlast_row) in zip(r1, r2, av, insts):
        qa_p, u0 = r1_i[:, :PAIR], r1_i[:, PAIR:]
        m = jnp.where(same_head, _dot_tn(b16(qa_p), b16(kbe)), zero2)
        m = m + jnp.where(row == col, jnp.broadcast_to(wlast_row, (PAIR, PAIR)), zero2)
        n = jnp.where(same_head,
                      _dot_tn(b16(jnp.concatenate([u0, v], axis=0)),
                              b16(jnp.concatenate([kbe, kke], axis=0))), zero2)
        out.append((qr + r2_i[:, :PAIR], r2_i[:, PAIR:] + av_i[c:], m, n))
    return out


def _rwkv_kernel(*refs, has_vres, tc, width):
    if has_vres:
        (pa_ref, mu_ref, wlh_ref, wll_ref, w0_ref, a0_ref, kk_ref, ka_ref, rk_ref, gng_ref, gnb_ref,
         vf_ref, v0_ref, vdh_ref, vdl_ref, vuh_ref, vul_ref, y_ref, prev_ref, s_ref, yraw_ref) = refs
    else:
        (pa_ref, mu_ref, wlh_ref, wll_ref, w0_ref, a0_ref, kk_ref, ka_ref, rk_ref, gng_ref, gnb_ref,
         y_ref, vf_ref, prev_ref, s_ref, yraw_ref) = refs
    n_pairs = width // PAIR
    n_chunks = tc // CHUNK

    @pl.when(pl.program_id(1) == 0)
    def _():
        prev_ref[...] = jnp.zeros_like(prev_ref)
        s_ref[...] = jnp.zeros_like(s_ref)

    p = pa_ref[...]
    row = lax.broadcasted_iota(jnp.int32, p.shape, 0)
    shifted = jnp.where(row == 0, jnp.broadcast_to(prev_ref[0:1, :], p.shape), pltpu.roll(p, 1, axis=0))
    prev_ref[0:1, :] = p[tc - 1:tc, :]
    p = p + mu_ref[...] * (shifted - p)

    r = p[:, :width]
    k = p[:, width:2 * width]
    v = p[:, 2 * width:3 * width]
    lo_in = p[:, 3 * width:]
    lane = lax.broadcasted_iota(jnp.int32, lo_in.shape, 1)
    lo_act = jnp.where(lane < LORA_W, jnp.tanh(lo_in),
                       jnp.where(lane < LORA_W + LORA_A, lo_in, _sigmoid(lo_in)))
    lo = _dot_x3(lo_act, wlh_ref[...], wll_ref[...])
    wz = w0_ref[...] + lo[:, :width]
    w = -(jnp.maximum(-wz, 0.0) + jnp.log(1.0 + jnp.exp(-jnp.abs(wz)))) - 0.5
    lw = -jnp.exp(w)
    a = _sigmoid(a0_ref[...] + lo[:, width:2 * width])
    g = lo[:, 2 * width:]

    if has_vres:
        v_lo = _dot_x3(v, vdh_ref[...], vdl_ref[...])
        gate = _sigmoid(v0_ref[...] + _dot_x3(v_lo, vuh_ref[...], vul_ref[...]))
        v = v + (vf_ref[...] - v) * gate
    else:
        vf_ref[...] = v

    kkx = k * kk_ref[...]
    kk = kkx / jnp.maximum(jnp.sqrt(_head_sum(kkx * kkx)), 1e-12)
    kmod = k * (1.0 + (a - 1.0) * ka_ref[...])

    ti = lax.broadcasted_iota(jnp.int32, (tc, tc), 0)
    tj = lax.broadcasted_iota(jnp.int32, (tc, tc), 1)
    tri = jnp.where(((ti // CHUNK) == (tj // CHUNK)) & (tj <= ti), 1.0, 0.0).astype(BF16)
    cum = None
    for part in _split(lw, 3):
        d = _dot(tri, part)
        cum = d if cum is None else cum + d
    tot = jnp.concatenate(
        [jnp.broadcast_to(cum[(ci + 1) * CHUNK - 1:(ci + 1) * CHUNK, :], (CHUNK, width)) for ci in range(n_chunks)],
        axis=0)
    w_inc = jnp.exp(cum)
    w_exc = jnp.exp(cum - lw)
    w_inv = jnp.exp(-cum)
    w_end = jnp.exp(tot - cum)
    w_tot = jnp.exp(tot)

    kka = kk * a
    qa_all = -kk * w_exc
    qr_all = r * w_inc
    kb_all = kka * w_inv
    kk_all = kmod * w_inv
    kbe_all = kka * w_end
    kke_all = kmod * w_end

    insts = []
    for ci in range(n_chunks):
        rs = slice(ci * CHUNK, (ci + 1) * CHUNK)
        for pi in range(n_pairs):
            ls = slice(pi * PAIR, (pi + 1) * PAIR)
            insts.append((qa_all[rs, ls], qr_all[rs, ls], kb_all[rs, ls], kk_all[rs, ls],
                          kbe_all[rs, ls], kke_all[rs, ls], v[rs, ls], w_tot[ci * CHUNK:ci * CHUNK + 1, ls]))
    maps = _rwkv_chunk_maps(insts)
    states = [s_ref[pi] for pi in range(n_pairs)]
    for ci in range(n_chunks):
        for pi in range(n_pairs):
            qr_p, y0, m, n = maps[ci * n_pairs + pi]
            s16 = states[pi].astype(BF16)
            yraw_ref[ci * CHUNK:(ci + 1) * CHUNK, pi * PAIR:(pi + 1) * PAIR] = _dot_nt(qr_p.astype(BF16), s16) + y0
            states[pi] = _dot(s16, m.astype(BF16)) + n
    for pi in range(n_pairs):
        s_ref[pi] = states[pi]

    y = yraw_ref[...]
    mean = _head_sum(y) * (1.0 / HEAD)
    yc = y - mean
    var = _head_sum(yc * yc) * (1.0 / HEAD)
    yn = yc * lax.rsqrt(var + GN_EPS) * gng_ref[...] + gnb_ref[...]
    bonus = _head_sum(r * kmod * rk_ref[...]) * v
    y_ref[...] = ((yn + bonus) * g).astype(y_ref.dtype)


def _rwkv(pa, vfirst, prm, l, *, tc):
    b, t, a_proj = pa.shape
    width = prm["w0"].shape[-1]
    has_vres = vfirst is not None
    tile = lambda w_: pl.BlockSpec((None, tc, w_), lambda i, j: (i, j, 0))
    names = ["mu", "wl_hi", "wl_lo", "w0", "a0", "k_k", "k_a", "r_k", "gn_g", "gn_b"]
    args = [pa] + [prm[nm] for nm in names]
    in_specs = [tile(a_proj)] + [_layer_spec(prm[nm], l) for nm in names]
    y_shape = jax.ShapeDtypeStruct((b, t, width), BF16)
    if has_vres:
        extra = [prm["v0"], prm["vd_hi"], prm["vd_lo"], prm["vu_hi"], prm["vu_lo"]]
        args += [vfirst] + extra
        in_specs += [tile(width)] + [_layer_spec(a, l - 1) for a in extra]
        out_specs, out_shape = tile(width), y_shape
    else:
        out_specs = [tile(width), tile(width)]
        out_shape = [y_shape, jax.ShapeDtypeStruct((b, t, width), F32)]
    return pl.pallas_call(
        functools.partial(_rwkv_kernel, has_vres=has_vres, tc=tc, width=width),
        grid=(b, t // tc),
        in_specs=in_specs,
        out_specs=out_specs,
        out_shape=out_shape,
        scratch_shapes=[pltpu.VMEM((8, a_proj), F32),
                        pltpu.VMEM((width // PAIR, PAIR, PAIR), F32),
                        pltpu.VMEM((tc, width), F32)],
        compiler_params=_params("parallel", "arbitrary"),
        name="rwkv",
    )(*args)


def _attn_pool_kernel(q_ref, k_ref, v_ref, bias_ref, u_ref, pw_ref, ps_ref, yb_ref, yc_ref,
                      kpad_ref, vpad_ref, upad_ref, *, width, qb):
    step = pl.program_id(1)
    rows = qb * CHUNK
    win_rows = (PREV_CHUNKS + qb) * CHUNK
    seq = k_ref.shape[0]

    @pl.when(step == 0)
    def _():
        kpad_ref[:PREV_CHUNKS * CHUNK, :] = jnp.zeros((PREV_CHUNKS * CHUNK, width), BF16)
        vpad_ref[:PREV_CHUNKS * CHUNK, :] = jnp.zeros((PREV_CHUNKS * CHUNK, width), BF16)
        kpad_ref[PREV_CHUNKS * CHUNK:, :] = k_ref[...]
        vpad_ref[PREV_CHUNKS * CHUNK:, :] = v_ref[...]
        upad_ref[:CHUNK, :] = jnp.zeros((CHUNK, upad_ref.shape[1]), F32)
        upad_ref[CHUNK:, :] = u_ref[...]

    start = pl.multiple_of(step * rows, rows)
    q = q_ref[...]
    slot = lax.broadcasted_iota(jnp.int32, (1, win_rows), 1)
    before_start = jnp.where(slot >= PREV_CHUNKS * CHUNK - start, 0.0, NEG_INF)
    first = _lane_lt_head((rows, PAIR))
    n_pairs = width // PAIR
    scores = []
    for pi in range(n_pairs):
        ls = slice(pi * PAIR, (pi + 1) * PAIR)
        q2 = q[:, ls]
        zq = jnp.zeros_like(q2)
        qs = jnp.concatenate([jnp.where(first, q2, zq), jnp.where(first, zq, q2)], axis=0)
        scores.append(_dot_nt(qs, kpad_ref[pl.ds(start, win_rows), ls]) + bias_ref[pi] + before_start)
    probs, sums = [], []
    for s in scores:
        e = jnp.exp(s - jnp.max(s, axis=-1, keepdims=True))
        sums.append(jnp.sum(e, axis=-1, keepdims=True))
        probs.append(e.astype(BF16))
    for pi in range(n_pairs):
        ls = slice(pi * PAIR, (pi + 1) * PAIR)
        o = _dot(probs[pi], vpad_ref[pl.ds(start, win_rows), ls]) / sums[pi]
        yb_ref[:, ls] = jnp.where(first, o[:rows], o[rows:]).astype(yb_ref.dtype)

    x = upad_ref[pl.ds(start, rows + CHUNK), :]
    cw = x.shape[1]
    acc, span, win_sums = x, 1, []
    for win in POOL_WINDOWS:
        while span < win:
            acc = acc + pltpu.roll(acc, span, axis=0)
            span *= 2
        win_sums.append(acc)
    lane = lax.broadcasted_iota(jnp.int32, (rows, cw), 1)
    t1 = (lax.broadcasted_iota(jnp.int32, (rows, cw), 0) + start + 1).astype(F32)
    pooled = jnp.zeros((rows, cw), F32)
    for gi, win in enumerate(POOL_WINDOWS):
        grp = (lane >= gi * HEAD) & (lane < (gi + 1) * HEAD)
        pooled = jnp.where(grp, win_sums[gi][CHUNK:] / jnp.minimum(t1, float(win)), pooled)
    pooled = pooled - x[CHUNK:]
    yc_ref[...] = (_dot(pooled.astype(BF16), pw_ref[...]) * ps_ref[...]).astype(yc_ref.dtype)


def _attn_pool(q, k, v, bias, u, pw, ps, l, *, qb):
    b, t, width = q.shape
    cw = u.shape[-1]
    rows = qb * CHUNK
    lay = lambda a: _layer_spec(a, l)
    seq = lambda a: pl.BlockSpec((None,) + a.shape[1:], lambda i, j: (i, 0, 0))
    tile = lambda w_: pl.BlockSpec((None, rows, w_), lambda i, j: (i, j, 0))
    return pl.pallas_call(
        functools.partial(_attn_pool_kernel, width=width, qb=qb),
        grid=(b, t // rows),
        in_specs=[tile(width), seq(k), seq(v), lay(bias), seq(u), lay(pw), lay(ps)],
        out_specs=[tile(width), tile(cw)],
        out_shape=[jax.ShapeDtypeStruct((b, t, width), BF16), jax.ShapeDtypeStruct((b, t, cw), BF16)],
        scratch_shapes=[pltpu.VMEM((t + PREV_CHUNKS * CHUNK, width), BF16),
                        pltpu.VMEM((t + PREV_CHUNKS * CHUNK, width), BF16),
                        pltpu.VMEM((t + CHUNK, cw), F32)],
        compiler_params=_params("parallel", "arbitrary"),
        name="attn_pool",
    )(q, k, v, bias, u, pw, ps)


def _out_cross_kernel(x_ref, ya_ref, yb_ref, yc_ref, woa_ref, wob_ref, woc_ref, g_ref, wq_ref, qg_ref,
                      k_ref, v_ref, wo_ref, o_ref, att_ref, *, heads):
    x = (x_ref[...] + _dot(ya_ref[...], woa_ref[...]) + _dot(yb_ref[...], wob_ref[...])
         + _dot(yc_ref[...], woc_ref[...]))
    h = _rms(x, g_ref[...]).astype(BF16)
    q = _dot(h, wq_ref[...])
    hd = q.shape[1] // heads
    for hi in range(heads):
        cs = slice(hi * hd, (hi + 1) * hd)
        qh = (_rms(q[:, cs], qg_ref[...]) * (hd ** -0.5)).astype(BF16)
        s = _dot_nt(qh, k_ref[:, cs])
        e = jnp.exp(s - jnp.max(s, axis=-1, keepdims=True))
        oh = _dot(e.astype(BF16), v_ref[:, cs]) / jnp.sum(e, axis=-1, keepdims=True)
        att_ref[:, cs] = oh.astype(BF16)
    o_ref[...] = x + _dot(att_ref[...], wo_ref[...])


def _out_cross(x, ya, yb, yc, w_out, g, wq, qg, k, v, wo, l, *, tm, heads):
    b, t, d = x.shape
    lay = lambda a: _layer_spec(a, l)
    tile = lambda a: pl.BlockSpec((None, tm, a.shape[-1]), lambda i, j: (i, j, 0))
    mem = lambda a: pl.BlockSpec((None, None) + a.shape[2:], lambda i, j: (l, i, 0, 0))
    wa, wb, wc = ya.shape[-1], yb.shape[-1], yc.shape[-1]
    assert wa == wb and (wa + wb) % wc == 0, "row blocks of w_out must be block-aligned"
    w_rows = lambda rows, blk: pl.BlockSpec((None, rows, d), lambda i, j: (l, blk, 0))
    return pl.pallas_call(
        functools.partial(_out_cross_kernel, heads=heads),
        grid=(b, t // tm),
        in_specs=[tile(x), tile(ya), tile(yb), tile(yc), w_rows(wa, 0), w_rows(wb, 1), w_rows(wc, (wa + wb) // wc),
                  lay(g), lay(wq), lay(qg), mem(k), mem(v), lay(wo)],
        out_specs=tile(x),
        out_shape=jax.ShapeDtypeStruct((b, t, d), F32),
        scratch_shapes=[pltpu.VMEM((tm, d), BF16)],
        compiler_params=_params("parallel", "parallel"),
        name="out_cross",
    )(x, ya, yb, yc, w_out, w_out, w_out, g, wq, qg, k, v, wo)


def _mem_kv_kernel(mem_ref, g_ref, w_ref, kg_ref, k_ref, v_ref, *, heads):
    h = _rms(mem_ref[...], g_ref[...]).astype(BF16)
    kv = _dot(h, w_ref[...])
    d = kv.shape[1] // 2
    hd = d // heads
    for hi in range(heads):
        cs = slice(hi * hd, (hi + 1) * hd)
        k_ref[:, cs] = _rms(kv[:, cs], kg_ref[...]).astype(BF16)
    v_ref[...] = kv[:, d:].astype(BF16)


def _mem_kv(mem, g, wkv, kg, *, heads):
    b, m, d = mem.shape
    depth = wkv.shape[0]
    per_layer = lambda a: pl.BlockSpec((None,) + a.shape[1:], lambda l, i: (l,) + (0,) * (a.ndim - 1))
    out = pl.BlockSpec((None, None, m, d), lambda l, i: (l, i, 0, 0))
    return pl.pallas_call(
        functools.partial(_mem_kv_kernel, heads=heads),
        grid=(depth, b),
        in_specs=[pl.BlockSpec((None, m, d), lambda l, i: (i, 0, 0)), per_layer(g), per_layer(wkv), per_layer(kg)],
        out_specs=[out, out],
        out_shape=[jax.ShapeDtypeStruct((depth, b, m, d), BF16)] * 2,
        compiler_params=_params("parallel", "parallel"),
        name="mem_kv",
    )(mem, g, wkv, kg)


def _segment_ones(width):
    head = np.arange(width) // HEAD
    return jnp.asarray((head[:, None] == head[None, :]).astype(np.float32)).astype(BF16)


def _split_weight(w):
    hi = w.astype(BF16)
    return hi, (w - hi.astype(F32)).astype(BF16)


def _rel_bias_band(rel_bias):
    heads, n_rel = rel_bias.shape
    ext_len = BAND + CHUNK - 1
    ext = jnp.concatenate([rel_bias, jnp.broadcast_to(rel_bias[:, -1:], (heads, ext_len - n_rel))], axis=1)
    rev = ext[:, ::-1]
    return jnp.stack([rev[:, CHUNK - 1 - i:CHUNK - 1 - i + BAND] for i in range(CHUNK)], axis=1)


def _rel_bias_window(rel_bias, qb):
    band = _rel_bias_band(rel_bias)
    heads = band.shape[0]
    per_chunk = [jnp.pad(band, ((0, 0), (0, 0), (qi * CHUNK, (qb - 1 - qi) * CHUNK)), constant_values=NEG_INF)
                 for qi in range(qb)]
    win = jnp.stack(per_chunk, axis=1)
    return win.reshape(heads // 2, 2 * qb * CHUNK, (PREV_CHUNKS + qb) * CHUNK)


def _lora_weight(w_up, a_up, g_up):
    depth, _, width = w_up.shape
    z = lambda r: jnp.zeros((depth, r, width), F32)
    return jnp.concatenate([
        jnp.concatenate([w_up, z(LORA_W), z(LORA_W)], axis=2),
        jnp.concatenate([z(LORA_A), a_up, z(LORA_A)], axis=2),
        jnp.concatenate([z(LORA_G), z(LORA_G), g_up], axis=2)], axis=1)


def _pool_weight(pool_w):
    depth, groups, cg, _ = pool_w.shape
    rows = []
    for gi in range(groups):
        blocks = [pool_w[:, gi] if gj == gi else jnp.zeros((depth, cg, cg), pool_w.dtype) for gj in range(groups)]
        rows.append(jnp.concatenate(blocks, axis=2))
    return jnp.concatenate(rows, axis=1)


def _stack_rows(a):
    return a.reshape(a.shape[0], 1, -1)


def _attn_pool_params(rel_bias, pool_w, pool_scale, qb):
    return {"bias": jax.vmap(functools.partial(_rel_bias_window, qb=qb))(rel_bias),
            "pw": _pool_weight(pool_w).astype(BF16), "ps": _stack_rows(pool_scale)}


def _rwkv_params(a_mu, a_w0, a_w_up, a_a0, a_a_up, a_g_up, a_k_k, a_k_a, a_r_k, a_gn_g, a_gn_b,
                 a_v0, a_v_down, a_v_up):
    wl_hi, wl_lo = _split_weight(_lora_weight(a_w_up, a_a_up, a_g_up))
    pad_lanes = LANES - LORA_V
    vd_hi, vd_lo = _split_weight(jnp.pad(a_v_down, ((0, 0), (0, 0), (0, pad_lanes))))
    vu_hi, vu_lo = _split_weight(jnp.pad(a_v_up, ((0, 0), (0, pad_lanes), (0, 0))))
    return {"mu": _stack_rows(a_mu), "wl_hi": wl_hi, "wl_lo": wl_lo, "w0": _stack_rows(a_w0),
            "a0": _stack_rows(a_a0), "k_k": _stack_rows(a_k_k), "k_a": _stack_rows(a_k_a),
            "r_k": _stack_rows(a_r_k), "gn_g": _stack_rows(a_gn_g), "gn_b": _stack_rows(a_gn_b),
            "v0": _stack_rows(a_v0),
            "vd_hi": vd_hi, "vd_lo": vd_lo, "vu_hi": vu_hi, "vu_lo": vu_lo}


def kernel(x, mem, norm_ffn1, ffn1_wi, ffn1_wo, norm_mix, w_in, w_out, a_mu, a_w0, a_w_up, a_a0, a_a_up, a_g_up,
           a_k_k, a_k_a, a_r_k, a_gn_g, a_gn_b, a_v0, a_v_down, a_v_up, b_q_gain, b_k_gain, b_rel_bias,
           c_pool_w, c_pool_scale, norm_cross, norm_mem, x_wq, x_wkv, x_wo, x_q_gain, x_k_gain,
           norm_ffn2, ffn2_wi, ffn2_wo):
    b, t, d = x.shape
    depth = w_in.shape[0]
    a_proj = a_mu.shape[-1]
    b_width = b_rel_bias.shape[1] * HEAD
    c_width = c_pool_scale.shape[-1]
    x_heads = d // x_q_gain.shape[-1]
    n_tok = b * t
    b16 = lambda a: a.astype(BF16)

    ffn1 = (_stack_rows(norm_ffn1), b16(ffn1_wi), b16(ffn1_wo))
    ffn2 = (_stack_rows(norm_ffn2), b16(ffn2_wi), b16(ffn2_wo))
    w_in16, w_out16, wq16, wo16 = b16(w_in), b16(w_out), b16(x_wq), b16(x_wo)
    q_gain = _stack_rows(jnp.tile(b_q_gain, (1, b_width // HEAD))) * (HEAD ** -0.5)
    k_gain = _stack_rows(jnp.tile(b_k_gain, (1, b_width // HEAD)))
    seg_b = _segment_ones(b_width)
    rwkv_prm = _rwkv_params(a_mu, a_w0, a_w_up, a_a0, a_a_up, a_g_up, a_k_k, a_k_a, a_r_k, a_gn_g, a_gn_b,
                            a_v0, a_v_down, a_v_up)
    ap_prm = _attn_pool_params(b_rel_bias, c_pool_w, c_pool_scale, ATTN_QB)
    mem_k, mem_v = _mem_kv(mem, _stack_rows(norm_mem), b16(x_wkv), _stack_rows(x_k_gain), heads=x_heads)

    ffn_tf = FFN_TF
    seq = lambda a: a.reshape(b, t, a.shape[-1])
    x = x.reshape(n_tok, d)
    v_first = None
    for l in range(depth):
        x = _ffn(x, *ffn1, l, tm=FFN_TM, tf=ffn_tf)
        pa, q, k, v, u = _inproj(x, _stack_rows(norm_mix), w_in16, seg_b, q_gain, k_gain, l,
                                 tm=ROW_TM, a_proj=a_proj, b_width=b_width, c_width=c_width)
        if l == 0:
            y_a, v_first = _rwkv(seq(pa), None, rwkv_prm, l, tc=RWKV_TC)
        else:
            y_a = _rwkv(seq(pa), v_first, rwkv_prm, l, tc=RWKV_TC)
        y_b, y_c = _attn_pool(seq(q), seq(k), seq(v), ap_prm["bias"], seq(u), ap_prm["pw"], ap_prm["ps"], l,
                              qb=ATTN_QB)
        x = _out_cross(seq(x), y_a, y_b, y_c, w_out16, _stack_rows(norm_cross), wq16, _stack_rows(x_q_gain),
                       mem_k, mem_v, wo16, l, tm=ROW_TM, heads=x_heads)
        x = _ffn(x.reshape(n_tok, d), *ffn2, l, tm=FFN_TM, tf=ffn_tf)
    return x.reshape(b, t, d)
```

```python
import functools

import numpy as np
import jax
import jax.numpy as jnp
from jax import lax
from jax.experimental import pallas as pl
from jax.experimental.pallas import tpu as pltpu

F32 = jnp.float32
BF16 = jnp.bfloat16

LANES = 128
HEAD = 64
PAIR = 2 * HEAD
CHUNK = 64
INV_BLOCK = 8
PREV_CHUNKS = 8
BAND = (PREV_CHUNKS + 1) * CHUNK
REL_MAX = 256
POOL_WINDOWS = (2, 4, 8, 16)
LORA_W, LORA_A, LORA_G, LORA_V = 32, 32, 64, 32
RMS_EPS = 1e-6
GN_EPS = 64e-5
NEG_INF = -1e30
VMEM_LIMIT = 56 * 1024 * 1024
FFN_TM = 1024
FFN_TF = 256
ROW_TM = 1024
RWKV_TC = 512
ATTN_QB = 4


def _dot(a, b, precision=None):
    return jnp.dot(a, b, preferred_element_type=F32, precision=precision)


def _dot_nt(a, b):
    return lax.dot_general(a, b, (((1,), (1,)), ((), ())), preferred_element_type=F32)


def _dot_tn(a, b):
    return lax.dot_general(a, b, (((0,), (0,)), ((), ())), preferred_element_type=F32)


def _split(x, terms):
    parts = []
    for _ in range(terms):
        hi = x.astype(BF16)
        parts.append(hi)
        x = x - hi.astype(F32)
    return parts


def _dot_x3(x, w_hi, w_lo):
    x_hi, x_lo = _split(x, 2)
    return _dot(x_hi, w_hi) + (_dot(x_lo, w_hi) + _dot(x_hi, w_lo))


def _rms(x, g):
    return x * lax.rsqrt(jnp.mean(x * x, axis=-1, keepdims=True) + RMS_EPS) * g


def _sigmoid(x):
    return 1.0 / (1.0 + jnp.exp(-x))


def _params(*sem):
    return pltpu.CompilerParams(dimension_semantics=sem, vmem_limit_bytes=VMEM_LIMIT)


def _layer_spec(a, l):
    zeros = (0,) * (a.ndim - 1)
    return pl.BlockSpec((None,) + a.shape[1:], lambda *_: (l,) + zeros)


def _ffn_kernel(x_ref, g_ref, wi_ref, wo_ref, o_ref, *, tf):
    x = x_ref[...]
    h = _rms(x, g_ref[...]).astype(BF16)
    dff = wo_ref.shape[0]
    acc = None
    for c0 in range(0, dff, tf):
        gate = _dot(h, wi_ref[:, c0:c0 + tf])
        up = _dot(h, wi_ref[:, dff + c0:dff + c0 + tf])
        act = (gate * _sigmoid(gate) * up).astype(BF16)
        part = _dot(act, wo_ref[c0:c0 + tf, :])
        acc = part if acc is None else acc + part
    o_ref[...] = x + 0.5 * acc


def _ffn(x, g, wi, wo, l, *, tm, tf):
    n, d = x.shape
    resident = lambda a: pl.BlockSpec((None,) + a.shape[1:], lambda i: (l, 0, 0), pipeline_mode=pl.Buffered(1))
    return pl.pallas_call(
        functools.partial(_ffn_kernel, tf=tf),
        grid=(n // tm,),
        in_specs=[pl.BlockSpec((tm, d), lambda i: (i, 0)), _layer_spec(g, l), resident(wi), resident(wo)],
        out_specs=pl.BlockSpec((tm, d), lambda i: (i, 0)),
        out_shape=jax.ShapeDtypeStruct((n, d), F32),
        compiler_params=_params("parallel"),
        name="ffn",
    )(x, g, wi, wo)


def _inproj_kernel(x_ref, g_ref, w_ref, qg_ref, kg_ref,
                   pa_ref, q_ref, k_ref, v_ref, u_ref, *, a_proj, b_width):
    h = _rms(x_ref[...], g_ref[...]).astype(BF16)
    p = _dot(h, w_ref[...])
    pa_ref[...] = p[:, :a_proj]
    q = p[:, a_proj:a_proj + b_width]
    k = p[:, a_proj + b_width:a_proj + 2 * b_width]
    qms = _head_sum(q * q) * (1.0 / HEAD)
    kms = _head_sum(k * k) * (1.0 / HEAD)
    q_ref[...] = (q * lax.rsqrt(qms + RMS_EPS) * qg_ref[...]).astype(BF16)
    k_ref[...] = (k * lax.rsqrt(kms + RMS_EPS) * kg_ref[...]).astype(BF16)
    v_ref[...] = p[:, a_proj + 2 * b_width:a_proj + 3 * b_width].astype(BF16)
    u_ref[...] = p[:, a_proj + 3 * b_width:]


def _inproj(x, g, w, qg, kg, l, *, tm, a_proj, b_width, c_width):
    n, d = x.shape
    row = lambda w_: pl.BlockSpec((tm, w_), lambda i: (i, 0))
    lay = lambda a: _layer_spec(a, l)
    return pl.pallas_call(
        functools.partial(_inproj_kernel, a_proj=a_proj, b_width=b_width),
        grid=(n // tm,),
        in_specs=[row(d), lay(g), lay(w), lay(qg), lay(kg)],
        out_specs=[row(a_proj), row(b_width), row(b_width), row(b_width), row(c_width)],
        out_shape=[jax.ShapeDtypeStruct((n, a_proj), F32),
                   jax.ShapeDtypeStruct((n, b_width), BF16),
                   jax.ShapeDtypeStruct((n, b_width), BF16),
                   jax.ShapeDtypeStruct((n, b_width), BF16),
                   jax.ShapeDtypeStruct((n, c_width), F32)],
        compiler_params=_params("parallel"),
        name="in_proj",
    )(x, g, w, qg, kg)


def _lane_lt_head(shape):
    return lax.broadcasted_iota(jnp.int32, shape, len(shape) - 1) < HEAD


def _head_sum(x):
    out = []
    for ls in range(0, x.shape[1], PAIR):
        xs = x[:, ls:ls + PAIR]
        first = _lane_lt_head(xs.shape)
        sum_a = jnp.sum(jnp.where(first, xs, 0.0), axis=-1, keepdims=True)
        sum_b = jnp.sum(jnp.where(first, 0.0, xs), axis=-1, keepdims=True)
        out.append(jnp.where(first, sum_a, sum_b))
    return jnp.concatenate(out, axis=1)


def _blockdiag(x2):
    first = _lane_lt_head(x2.shape)
    zero = jnp.zeros_like(x2)
    return jnp.concatenate([jnp.where(first, x2, zero), jnp.where(first, zero, x2)], axis=0)


def _rwkv_chunk_maps(insts):
    c = CHUNK
    b16 = lambda t: t.astype(BF16)
    t_idx = lax.broadcasted_iota(jnp.int32, (c, PAIR), 0)
    s_idx = lax.broadcasted_iota(jnp.int32, (c, PAIR), 1) % HEAD
    strict = t_idx > s_idx
    incl = t_idx >= s_idx
    zero = jnp.zeros((c, PAIR), F32)
    eye = jnp.where(t_idx == s_idx, 1.0, 0.0)
    row = lax.broadcasted_iota(jnp.int32, (PAIR, PAIR), 0)
    col = lax.broadcasted_iota(jnp.int32, (PAIR, PAIR), 1)
    same_head = (row < HEAD) == (col < HEAD)
    zero2 = jnp.zeros((PAIR, PAIR), F32)

    s_all = [_dot_nt(b16(jnp.concatenate([qa, qr], axis=0)),
                     b16(jnp.concatenate([_blockdiag(kb), _blockdiag(kk)], axis=0)))
             for qa, qr, kb, kk, _, _, _, _ in insts]
    a_ab = [jnp.where(strict, s[:c, :PAIR], zero) for s in s_all]
    a_ak = [jnp.where(strict, s[:c, PAIR:], zero) for s in s_all]
    a_rb = [jnp.where(incl, s[c:, :PAIR], zero) for s in s_all]
    a_rk = [jnp.where(incl, s[c:, PAIR:], zero) for s in s_all]
    av = [_dot(b16(jnp.concatenate([ak, rk], axis=0)), b16(_blockdiag(inst[6])))
          for ak, rk, inst in zip(a_ak, a_rk, insts)]

    same_block = lambda blk: (t_idx // blk) == (s_idx // blk)
    a0 = [jnp.where(same_block(INV_BLOCK), a, zero) for a in a_ab]
    a2 = [_dot(b16(a), b16(_blockdiag(a))) for a in a0]
    p1 = [eye + a for a in a0]
    st = [_dot(b16(jnp.concatenate([sq, p], axis=0)), b16(_blockdiag(sq))) for sq, p in zip(a2, p1)]
    p2 = [p + s[c:] for p, s in zip(p1, st)]
    tinv = [p + _dot(b16(p), b16(_blockdiag(s[:c]))) for p, s in zip(p2, st)]
    blk = INV_BLOCK
    while blk < c:
        off_diag = same_block(2 * blk) & jnp.logical_not(same_block(blk))
        x1 = [_dot(b16(jnp.where(off_diag, a, zero)), b16(_blockdiag(t))) for a, t in zip(a_ab, tinv)]
        tinv = [t + _dot(b16(t), b16(_blockdiag(x))) for t, x in zip(tinv, x1)]
        blk *= 2

    r1 = [_dot(b16(t), b16(jnp.concatenate([_blockdiag(inst[0]), _blockdiag(a[:c])], axis=1)))
          for t, a, inst in zip(tinv, av, insts)]
    r2 = [_dot(b16(rb), b16(jnp.concatenate([_blockdiag(r[:, :PAIR]), _blockdiag(r[:, PAIR:])], axis=1)))
          for rb, r in zip(a_rb, r1)]
    out = []
    for r1_i, r2_i, av_i, (qa, qr, kb, kk, kbe, kke, v, wlast_row) in zip(r1, r2, av, insts):
        qa_p, u0 = r1_i[:, :PAIR], r1_i[:, PAIR:]
        m = jnp.where(same_head, _dot_tn(b16(qa_p), b16(kbe)), zero2)
        m = m + jnp.where(row == col, jnp.broadcast_to(wlast_row, (PAIR, PAIR)), zero2)
        n = jnp.where(same_head,
                      _dot_tn(b16(jnp.concatenate([u0, v], axis=0)),
                              b16(jnp.concatenate([kbe, kke], axis=0))), zero2)
        out.append((qr + r2_i[:, :PAIR], r2_i[:, PAIR:] + av_i[c:], m, n))
    return out


def _rwkv_kernel(*refs, has_vres, tc, width):
    if has_vres:
        (pa_ref, mu_ref, wlh_ref, wll_ref, w0_ref, a0_ref, kk_ref, ka_ref, rk_ref, gng_ref, gnb_ref,
         vf_ref, v0_ref, vdh_ref, vdl_ref, vuh_ref, vul_ref, y_ref, prev_ref, s_ref, yraw_ref) = refs
    else:
        (pa_ref, mu_ref, wlh_ref, wll_ref, w0_ref, a0_ref, kk_ref, ka_ref, rk_ref, gng_ref, gnb_ref,
         y_ref, vf_ref, prev_ref, s_ref, yraw_ref) = refs
    n_pairs = width // PAIR
    n_chunks = tc // CHUNK

    @pl.when(pl.program_id(1) == 0)
    def _():
        prev_ref[...] = jnp.zeros_like(prev_ref)
        s_ref[...] = jnp.zeros_like(s_ref)

    p = pa_ref[...]
    row = lax.broadcasted_iota(jnp.int32, p.shape, 0)
    shifted = jnp.where(row == 0, jnp.broadcast_to(prev_ref[0:1, :], p.shape), pltpu.roll(p, 1, axis=0))
    prev_ref[0:1, :] = p[tc - 1:tc, :]
    p = p + mu_ref[...] * (shifted - p)

    r = p[:, :width]
    k = p[:, width:2 * width]
    v = p[:, 2 * width:3 * width]
    lo_in = p[:, 3 * width:]
    lane = lax.broadcasted_iota(jnp.int32, lo_in.shape, 1)
    lo_act = jnp.where(lane < LORA_W, jnp.tanh(lo_in),
                       jnp.where(lane < LORA_W + LORA_A, lo_in, _sigmoid(lo_in)))
    lo = _dot_x3(lo_act, wlh_ref[...], wll_ref[...])
    wz = w0_ref[...] + lo[:, :width]
    w = -(jnp.maximum(-wz, 0.0) + jnp.log(1.0 + jnp.exp(-jnp.abs(wz)))) - 0.5
    lw = -jnp.exp(w)
    a = _sigmoid(a0_ref[...] + lo[:, width:2 * width])
    g = lo[:, 2 * width:]

    if has_vres:
        v_lo = _dot_x3(v, vdh_ref[...], vdl_ref[...])
        gate = _sigmoid(v0_ref[...] + _dot_x3(v_lo, vuh_ref[...], vul_ref[...]))
        v = v + (vf_ref[...] - v) * gate
    else:
        vf_ref[...] = v

    kkx = k * kk_ref[...]
    kk = kkx / jnp.maximum(jnp.sqrt(_head_sum(kkx * kkx)), 1e-12)
    kmod = k * (1.0 + (a - 1.0) * ka_ref[...])

    grp = 2 * CHUNK
    ti = lax.broadcasted_iota(jnp.int32, (grp, grp), 0)
    tj = lax.broadcasted_iota(jnp.int32, (grp, grp), 1)
    tri = jnp.where(((ti // CHUNK) == (tj // CHUNK)) & (tj <= ti), 1.0, 0.0).astype(BF16)
    lw_hi, lw_lo = _split(lw, 2)
    cum = jnp.concatenate([_dot(tri, lw_hi[r0:r0 + grp]) + _dot(tri, lw_lo[r0:r0 + grp])
                           for r0 in range(0, tc, grp)], axis=0)
    tot = jnp.concatenate(
        [jnp.broadcast_to(cum[(ci + 1) * CHUNK - 1:(ci + 1) * CHUNK, :], (CHUNK, width)) for ci in range(n_chunks)],
        axis=0)
    w_inc = jnp.exp(cum)
    w_exc = jnp.exp(cum - lw)
    w_inv = jnp.exp(-cum)
    w_end = jnp.exp(tot - cum)
    w_tot = jnp.exp(tot)

    kka = kk * a
    qa_all = -kk * w_exc
    qr_all = r * w_inc
    kb_all = kka * w_inv
    kk_all = kmod * w_inv
    kbe_all = kka * w_end
    kke_all = kmod * w_end

    insts = []
    for ci in range(n_chunks):
        rs = slice(ci * CHUNK, (ci + 1) * CHUNK)
        for pi in range(n_pairs):
            ls = slice(pi * PAIR, (pi + 1) * PAIR)
            insts.append((qa_all[rs, ls], qr_all[rs, ls], kb_all[rs, ls], kk_all[rs, ls],
                          kbe_all[rs, ls], kke_all[rs, ls], v[rs, ls], w_tot[ci * CHUNK:ci * CHUNK + 1, ls]))
    maps = _rwkv_chunk_maps(insts)
    states = [s_ref[pi] for pi in range(n_pairs)]
    for ci in range(n_chunks):
        for pi in range(n_pairs):
            qr_p, y0, m, n = maps[ci * n_pairs + pi]
            s16 = states[pi].astype(BF16)
            yraw_ref[ci * CHUNK:(ci + 1) * CHUNK, pi * PAIR:(pi + 1) * PAIR] = _dot_nt(qr_p.astype(BF16), s16) + y0
            states[pi] = _dot(s16, m.astype(BF16)) + n
    for pi in range(n_pairs):
        s_ref[pi] = states[pi]

    y = yraw_ref[...]
    mean = _head_sum(y) * (1.0 / HEAD)
    yc = y - mean
    var = _head_sum(yc * yc) * (1.0 / HEAD)
    yn = yc * lax.rsqrt(var + GN_EPS) * gng_ref[...] + gnb_ref[...]
    bonus = _head_sum(r * kmod * rk_ref[...]) * v
    y_ref[...] = ((yn + bonus) * g).astype(y_ref.dtype)


def _rwkv(pa, vfirst, prm, l, *, tc):
    b, t, a_proj = pa.shape
    width = prm["w0"].shape[-1]
    has_vres = vfirst is not None
    tile = lambda w_: pl.BlockSpec((None, tc, w_), lambda i, j: (i, j, 0))
    names = ["mu", "wl_hi", "wl_lo", "w0", "a0", "k_k", "k_a", "r_k", "gn_g", "gn_b"]
    args = [pa] + [prm[nm] for nm in names]
    in_specs = [tile(a_proj)] + [_layer_spec(prm[nm], l) for nm in names]
    y_shape = jax.ShapeDtypeStruct((b, t, width), BF16)
    if has_vres:
        extra = [prm["v0"], prm["vd_hi"], prm["vd_lo"], prm["vu_hi"], prm["vu_lo"]]
        args += [vfirst] + extra
        in_specs += [tile(width)] + [_layer_spec(a, l - 1) for a in extra]
        out_specs, out_shape = tile(width), y_shape
    else:
        out_specs = [tile(width), tile(width)]
        out_shape = [y_shape, jax.ShapeDtypeStruct((b, t, width), F32)]
    return pl.pallas_call(
        functools.partial(_rwkv_kernel, has_vres=has_vres, tc=tc, width=width),
        grid=(b, t // tc),
        in_specs=in_specs,
        out_specs=out_specs,
        out_shape=out_shape,
        scratch_shapes=[pltpu.VMEM((8, a_proj), F32),
                        pltpu.VMEM((width // PAIR, PAIR, PAIR), F32),
                        pltpu.VMEM((tc, width), F32)],
        compiler_params=_params("parallel", "arbitrary"),
        name="rwkv",
    )(*args)


def _attn_pool_kernel(q_ref, k_ref, v_ref, bias_ref, u_ref, pw_ref, ps_ref, yb_ref, yc_ref,
                      kpad_ref, vpad_ref, upad_ref, *, width, qb):
    step = pl.program_id(1)
    rows = qb * CHUNK
    win_rows = (PREV_CHUNKS + qb) * CHUNK
    seq = k_ref.shape[0]

    @pl.when(step == 0)
    def _():
        kpad_ref[:PREV_CHUNKS * CHUNK, :] = jnp.zeros((PREV_CHUNKS * CHUNK, width), BF16)
        vpad_ref[:PREV_CHUNKS * CHUNK, :] = jnp.zeros((PREV_CHUNKS * CHUNK, width), BF16)
        kpad_ref[PREV_CHUNKS * CHUNK:, :] = k_ref[...]
        vpad_ref[PREV_CHUNKS * CHUNK:, :] = v_ref[...]
        upad_ref[:CHUNK, :] = jnp.zeros((CHUNK, upad_ref.shape[1]), F32)
        upad_ref[CHUNK:, :] = u_ref[...]

    start = pl.multiple_of(step * rows, rows)
    q = q_ref[...]
    slot = lax.broadcasted_iota(jnp.int32, (1, win_rows), 1)
    before_start = jnp.where(slot >= PREV_CHUNKS * CHUNK - start, 0.0, NEG_INF)
    first = _lane_lt_head((rows, PAIR))
    n_pairs = width // PAIR
    scores = []
    for pi in range(n_pairs):
        ls = slice(pi * PAIR, (pi + 1) * PAIR)
        q2 = q[:, ls]
        zq = jnp.zeros_like(q2)
        qs = jnp.concatenate([jnp.where(first, q2, zq), jnp.where(first, zq, q2)], axis=0)
        scores.append(_dot_nt(qs, kpad_ref[pl.ds(start, win_rows), ls]) + bias_ref[pi] + before_start)
    probs, sums = [], []
    for s in scores:
        e = jnp.exp(s - jnp.max(s, axis=-1, keepdims=True))
        sums.append(jnp.sum(e, axis=-1, keepdims=True))
        probs.append(e.astype(BF16))
    for pi in range(n_pairs):
        ls = slice(pi * PAIR, (pi + 1) * PAIR)
        o = _dot(probs[pi], vpad_ref[pl.ds(start, win_rows), ls]) / sums[pi]
        yb_ref[:, ls] = jnp.where(first, o[:rows], o[rows:]).astype(yb_ref.dtype)

    x = upad_ref[pl.ds(start, rows + CHUNK), :]
    cw = x.shape[1]
    acc, span, win_sums = x, 1, []
    for win in POOL_WINDOWS:
        while span < win:
            acc = acc + pltpu.roll(acc, span, axis=0)
            span *= 2
        win_sums.append(acc)
    lane = lax.broadcasted_iota(jnp.int32, (rows, cw), 1)
    t1 = (lax.broadcasted_iota(jnp.int32, (rows, cw), 0) + start + 1).astype(F32)
    pooled = jnp.zeros((rows, cw), F32)
    for gi, win in enumerate(POOL_WINDOWS):
        grp = (lane >= gi * HEAD) & (lane < (gi + 1) * HEAD)
        pooled = jnp.where(grp, win_sums[gi][CHUNK:] / jnp.minimum(t1, float(win)), pooled)
    pooled = pooled - x[CHUNK:]
    yc_ref[...] = (_dot(pooled.astype(BF16), pw_ref[...]) * ps_ref[...]).astype(yc_ref.dtype)


def _attn_pool(q, k, v, bias, u, pw, ps, l, *, qb):
    b, t, width = q.shape
    cw = u.shape[-1]
    rows = qb * CHUNK
    lay = lambda a: _layer_spec(a, l)
    seq = lambda a: pl.BlockSpec((None,) + a.shape[1:], lambda i, j: (i, 0, 0))
    tile = lambda w_: pl.BlockSpec((None, rows, w_), lambda i, j: (i, j, 0))
    return pl.pallas_call(
        functools.partial(_attn_pool_kernel, width=width, qb=qb),
        grid=(b, t // rows),
        in_specs=[tile(width), seq(k), seq(v), lay(bias), seq(u), lay(pw), lay(ps)],
        out_specs=[tile(width), tile(cw)],
        out_shape=[jax.ShapeDtypeStruct((b, t, width), BF16), jax.ShapeDtypeStruct((b, t, cw), BF16)],
        scratch_shapes=[pltpu.VMEM((t + PREV_CHUNKS * CHUNK, width), BF16),
                        pltpu.VMEM((t + PREV_CHUNKS * CHUNK, width), BF16),
                        pltpu.VMEM((t + CHUNK, cw), F32)],
        compiler_params=_params("parallel", "arbitrary"),
        name="attn_pool",
    )(q, k, v, bias, u, pw, ps)


def _out_cross_kernel(x_ref, ya_ref, yb_ref, yc_ref, woa_ref, wob_ref, woc_ref, g_ref, wq_ref, qg_ref,
                      k_ref, v_ref, wo_ref, o_ref, att_ref, *, heads):
    x = (x_ref[...] + _dot(ya_ref[...], woa_ref[...]) + _dot(yb_ref[...], wob_ref[...])
         + _dot(yc_ref[...], woc_ref[...]))
    h = _rms(x, g_ref[...]).astype(BF16)
    q = _dot(h, wq_ref[...])
    hd = q.shape[1] // heads
    cols = [slice(hi * hd, (hi + 1) * hd) for hi in range(heads)]
    qh = [(_rms(q[:, cs], qg_ref[...]) * (hd ** -0.5)).astype(BF16) for cs in cols]
    s = [_dot_nt(qi, k_ref[:, cs]) for qi, cs in zip(qh, cols)]
    e = [jnp.exp(si - jnp.max(si, axis=-1, keepdims=True)) for si in s]
    for ei, cs in zip(e, cols):
        oh = _dot(ei.astype(BF16), v_ref[:, cs]) / jnp.sum(ei, axis=-1, keepdims=True)
        att_ref[:, cs] = oh.astype(BF16)
    o_ref[...] = x + _dot(att_ref[...], wo_ref[...])


def _out_cross(x, ya, yb, yc, w_out, g, wq, qg, k, v, wo, l, *, tm, heads):
    b, t, d = x.shape
    lay = lambda a: _layer_spec(a, l)
    tile = lambda a: pl.BlockSpec((None, tm, a.shape[-1]), lambda i, j: (i, j, 0))
    mem = lambda a: pl.BlockSpec((None, None) + a.shape[2:], lambda i, j: (l, i, 0, 0))
    wa, wb, wc = ya.shape[-1], yb.shape[-1], yc.shape[-1]
    assert wa == wb and (wa + wb) % wc == 0, "row blocks of w_out must be block-aligned"
    w_rows = lambda rows, blk: pl.BlockSpec((None, rows, d), lambda i, j: (l, blk, 0))
    return pl.pallas_call(
        functools.partial(_out_cross_kernel, heads=heads),
        grid=(b, t // tm),
        in_specs=[tile(x), tile(ya), tile(yb), tile(yc), w_rows(wa, 0), w_rows(wb, 1), w_rows(wc, (wa + wb) // wc),
                  lay(g), lay(wq), lay(qg), mem(k), mem(v), lay(wo)],
        out_specs=tile(x),
        out_shape=jax.ShapeDtypeStruct((b, t, d), F32),
        scratch_shapes=[pltpu.VMEM((tm, d), BF16)],
        compiler_params=_params("parallel", "parallel"),
        name="out_cross",
    )(x, ya, yb, yc, w_out, w_out, w_out, g, wq, qg, k, v, wo)


def _mem_kv_kernel(mem_ref, g_ref, w_ref, kg_ref, k_ref, v_ref, *, heads):
    h = _rms(mem_ref[...], g_ref[...]).astype(BF16)
    kv = _dot(h, w_ref[...])
    d = kv.shape[1] // 2
    hd = d // heads
    for hi in range(heads):
        cs = slice(hi * hd, (hi + 1) * hd)
        k_ref[:, cs] = _rms(kv[:, cs], kg_ref[...]).astype(BF16)
    v_ref[...] = kv[:, d:].astype(BF16)


def _mem_kv(mem, g, wkv, kg, *, heads):
    b, m, d = mem.shape
    depth = wkv.shape[0]
    per_layer = lambda a: pl.BlockSpec((None,) + a.shape[1:], lambda l, i: (l,) + (0,) * (a.ndim - 1))
    out = pl.BlockSpec((None, None, m, d), lambda l, i: (l, i, 0, 0))
    return pl.pallas_call(
        functools.partial(_mem_kv_kernel, heads=heads),
        grid=(depth, b),
        in_specs=[pl.BlockSpec((None, m, d), lambda l, i: (i, 0, 0)), per_layer(g), per_layer(wkv), per_layer(kg)],
        out_specs=[out, out],
        out_shape=[jax.ShapeDtypeStruct((depth, b, m, d), BF16)] * 2,
        compiler_params=_params("parallel", "parallel"),
        name="mem_kv",
    )(mem, g, wkv, kg)


def _split_weight(w):
    hi = w.astype(BF16)
    return hi, (w - hi.astype(F32)).astype(BF16)


def _rel_bias_band(rel_bias):
    heads, n_rel = rel_bias.shape
    ext_len = BAND + CHUNK - 1
    ext = jnp.concatenate([rel_bias, jnp.broadcast_to(rel_bias[:, -1:], (heads, ext_len - n_rel))], axis=1)
    rev = ext[:, ::-1]
    return jnp.stack([rev[:, CHUNK - 1 - i:CHUNK - 1 - i + BAND] for i in range(CHUNK)], axis=1)


def _rel_bias_window(rel_bias, qb):
    band = _rel_bias_band(rel_bias)
    heads = band.shape[0]
    per_chunk = [jnp.pad(band, ((0, 0), (0, 0), (qi * CHUNK, (qb - 1 - qi) * CHUNK)), constant_values=NEG_INF)
                 for qi in range(qb)]
    win = jnp.stack(per_chunk, axis=1)
    return win.reshape(heads // 2, 2 * qb * CHUNK, (PREV_CHUNKS + qb) * CHUNK)


def _lora_weight(w_up, a_up, g_up):
    depth, _, width = w_up.shape
    z = lambda r: jnp.zeros((depth, r, width), F32)
    return jnp.concatenate([
        jnp.concatenate([w_up, z(LORA_W), z(LORA_W)], axis=2),
        jnp.concatenate([z(LORA_A), a_up, z(LORA_A)], axis=2),
        jnp.concatenate([z(LORA_G), z(LORA_G), g_up], axis=2)], axis=1)


def _pool_weight(pool_w):
    depth, groups, cg, _ = pool_w.shape
    rows = []
    for gi in range(groups):
        blocks = [pool_w[:, gi] if gj == gi else jnp.zeros((depth, cg, cg), pool_w.dtype) for gj in range(groups)]
        rows.append(jnp.concatenate(blocks, axis=2))
    return jnp.concatenate(rows, axis=1)


def _stack_rows(a):
    return a.reshape(a.shape[0], 1, -1)


def _attn_pool_params(rel_bias, pool_w, pool_scale, qb):
    return {"bias": jax.vmap(functools.partial(_rel_bias_window, qb=qb))(rel_bias),
            "pw": _pool_weight(pool_w).astype(BF16), "ps": _stack_rows(pool_scale)}


def _rwkv_params(a_mu, a_w0, a_w_up, a_a0, a_a_up, a_g_up, a_k_k, a_k_a, a_r_k, a_gn_g, a_gn_b,
                 a_v0, a_v_down, a_v_up):
    wl_hi, wl_lo = _split_weight(_lora_weight(a_w_up, a_a_up, a_g_up))
    pad_lanes = LANES - LORA_V
    vd_hi, vd_lo = _split_weight(jnp.pad(a_v_down, ((0, 0), (0, 0), (0, pad_lanes))))
    vu_hi, vu_lo = _split_weight(jnp.pad(a_v_up, ((0, 0), (0, pad_lanes), (0, 0))))
    return {"mu": _stack_rows(a_mu), "wl_hi": wl_hi, "wl_lo": wl_lo, "w0": _stack_rows(a_w0),
            "a0": _stack_rows(a_a0), "k_k": _stack_rows(a_k_k), "k_a": _stack_rows(a_k_a),
            "r_k": _stack_rows(a_r_k), "gn_g": _stack_rows(a_gn_g), "gn_b": _stack_rows(a_gn_b),
            "v0": _stack_rows(a_v0),
            "vd_hi": vd_hi, "vd_lo": vd_lo, "vu_hi": vu_hi, "vu_lo": vu_lo}


def kernel(x, mem, norm_ffn1, ffn1_wi, ffn1_wo, norm_mix, w_in, w_out, a_mu, a_w0, a_w_up, a_a0, a_a_up, a_g_up,
           a_k_k, a_k_a, a_r_k, a_gn_g, a_gn_b, a_v0, a_v_down, a_v_up, b_q_gain, b_k_gain, b_rel_bias,
           c_pool_w, c_pool_scale, norm_cross, norm_mem, x_wq, x_wkv, x_wo, x_q_gain, x_k_gain,
           norm_ffn2, ffn2_wi, ffn2_wo):
    b, t, d = x.shape
    depth = w_in.shape[0]
    a_proj = a_mu.shape[-1]
    b_width = b_rel_bias.shape[1] * HEAD
    c_width = c_pool_scale.shape[-1]
    x_heads = d // x_q_gain.shape[-1]
    n_tok = b * t
    b16 = lambda a: a.astype(BF16)

    ffn1 = (_stack_rows(norm_ffn1), b16(ffn1_wi), b16(ffn1_wo))
    ffn2 = (_stack_rows(norm_ffn2), b16(ffn2_wi), b16(ffn2_wo))
    w_in16, w_out16, wq16, wo16 = b16(w_in), b16(w_out), b16(x_wq), b16(x_wo)
    q_gain = _stack_rows(jnp.tile(b_q_gain, (1, b_width // HEAD))) * (HEAD ** -0.5)
    k_gain = _stack_rows(jnp.tile(b_k_gain, (1, b_width // HEAD)))
    rwkv_prm = _rwkv_params(a_mu, a_w0, a_w_up, a_a0, a_a_up, a_g_up, a_k_k, a_k_a, a_r_k, a_gn_g, a_gn_b,
                            a_v0, a_v_down, a_v_up)
    ap_prm = _attn_pool_params(b_rel_bias, c_pool_w, c_pool_scale, ATTN_QB)
    mem_k, mem_v = _mem_kv(mem, _stack_rows(norm_mem), b16(x_wkv), _stack_rows(x_k_gain), heads=x_heads)

    ffn_tf = FFN_TF
    seq = lambda a: a.reshape(b, t, a.shape[-1])
    x = x.reshape(n_tok, d)
    v_first = None
    for l in range(depth):
        x = _ffn(x, *ffn1, l, tm=FFN_TM, tf=ffn_tf)
        pa, q, k, v, u = _inproj(x, _stack_rows(norm_mix), w_in16, q_gain, k_gain, l,
                                 tm=ROW_TM, a_proj=a_proj, b_width=b_width, c_width=c_width)
        if l == 0:
            y_a, v_first = _rwkv(seq(pa), None, rwkv_prm, l, tc=RWKV_TC)
        else:
            y_a = _rwkv(seq(pa), v_first, rwkv_prm, l, tc=RWKV_TC)
        y_b, y_c = _attn_pool(seq(q), seq(k), seq(v), ap_prm["bias"], seq(u), ap_prm["pw"], ap_prm["ps"], l,
                              qb=ATTN_QB)
        x = _out_cross(seq(x), y_a, y_b, y_c, w_out16, _stack_rows(norm_cross), wq16, _stack_rows(x_q_gain),
                       mem_k, mem_v, wo16, l, tm=ROW_TM, heads=x_heads)
        x = _ffn(x.reshape(n_tok, d), *ffn2, l, tm=FFN_TM, tf=ffn_tf)
    return x.reshape(b, t, d)
```

```python
import functools

import numpy as np
import jax
import jax.numpy as jnp
from jax import lax
from jax.experimental import pallas as pl
from jax.experimental.pallas import tpu as pltpu

F32 = jnp.float32
BF16 = jnp.bfloat16

LANES = 128
HEAD = 64
PAIR = 2 * HEAD
CHUNK = 64
INV_BLOCK = 8
PREV_CHUNKS = 8
BAND = (PREV_CHUNKS + 1) * CHUNK
REL_MAX = 256
POOL_WINDOWS = (2, 4, 8, 16)
LORA_W, LORA_A, LORA_G, LORA_V = 32, 32, 64, 32
RMS_EPS = 1e-6
GN_EPS = 64e-5
NEG_INF = -1e30
LOG2_E = 1.4426950408889634
VMEM_LIMIT = 56 * 1024 * 1024
FFN_TM = 512
FFN_TF = 256
ROW_TM = 1024
RWKV_TC = 512
ATTN_QB = 4


def _dot(a, b, precision=None):
    return jnp.dot(a, b, preferred_element_type=F32, precision=precision)


def _dot_nt(a, b):
    return lax.dot_general(a, b, (((1,), (1,)), ((), ())), preferred_element_type=F32)


def _dot_tn(a, b):
    return lax.dot_general(a, b, (((0,), (0,)), ((), ())), preferred_element_type=F32)


def _split(x, terms):
    parts = []
    for _ in range(terms):
        hi = x.astype(BF16)
        parts.append(hi)
        x = x - hi.astype(F32)
    return parts


def _dot_x3(x, w_hi, w_lo):
    x_hi, x_lo = _split(x, 2)
    return _dot(x_hi, w_hi) + (_dot(x_lo, w_hi) + _dot(x_hi, w_lo))


def _rms(x, g):
    return x * lax.rsqrt(jnp.mean(x * x, axis=-1, keepdims=True) + RMS_EPS) * g


def _sigmoid(x):
    return 1.0 / (1.0 + jnp.exp(-x))


def _params(*sem):
    return pltpu.CompilerParams(dimension_semantics=sem, vmem_limit_bytes=VMEM_LIMIT)


def _layer_spec(a, l):
    zeros = (0,) * (a.ndim - 1)
    return pl.BlockSpec((None,) + a.shape[1:], lambda *_: (l,) + zeros, pipeline_mode=pl.Buffered(1))


def _ffn_kernel(x_ref, g_ref, wi_ref, wo_ref, o_ref, *, tf):
    x = x_ref[...]
    h = _rms(x, g_ref[...]).astype(BF16)
    dff = wo_ref.shape[0]
    acc = None
    for c0 in range(0, dff, tf):
        gate = _dot(h, wi_ref[:, c0:c0 + tf].astype(BF16))
        up = _dot(h, wi_ref[:, dff + c0:dff + c0 + tf].astype(BF16))
        act = (gate * _sigmoid(gate) * up).astype(BF16)
        part = _dot(act, wo_ref[c0:c0 + tf, :].astype(BF16))
        acc = part if acc is None else acc + part
    o_ref[...] = x + 0.5 * acc


def _ffn(x, g, wi, wo, l, *, tm, tf):
    n, d = x.shape
    return pl.pallas_call(
        functools.partial(_ffn_kernel, tf=tf),
        grid=(n // tm,),
        in_specs=[pl.BlockSpec((tm, d), lambda i: (i, 0)), _layer_spec(g, l), _layer_spec(wi, l), _layer_spec(wo, l)],
        out_specs=pl.BlockSpec((tm, d), lambda i: (i, 0)),
        out_shape=jax.ShapeDtypeStruct((n, d), F32),
        compiler_params=_params("parallel"),
        name="ffn",
    )(x, g, wi, wo)


def _inproj_kernel(x_ref, g_ref, w_ref, qg_ref, kg_ref,
                   pa_ref, q_ref, k_ref, v_ref, u_ref, *, a_proj, b_width):
    h = _rms(x_ref[...], g_ref[...]).astype(BF16)
    p = _dot(h, w_ref[...].astype(BF16))
    pa_ref[...] = p[:, :a_proj]
    q = p[:, a_proj:a_proj + b_width]
    k = p[:, a_proj + b_width:a_proj + 2 * b_width]
    qms = _head_sum(q * q) * (1.0 / HEAD)
    kms = _head_sum(k * k) * (1.0 / HEAD)
    q_ref[...] = (q * lax.rsqrt(qms + RMS_EPS) * qg_ref[...]).astype(BF16)
    k_ref[...] = (k * lax.rsqrt(kms + RMS_EPS) * kg_ref[...]).astype(BF16)
    v_ref[...] = p[:, a_proj + 2 * b_width:a_proj + 3 * b_width].astype(BF16)
    u_ref[...] = p[:, a_proj + 3 * b_width:]


def _inproj(x, g, w, qg, kg, l, *, tm, a_proj, b_width, c_width):
    n, d = x.shape
    row = lambda w_: pl.BlockSpec((tm, w_), lambda i: (i, 0))
    lay = lambda a: _layer_spec(a, l)
    return pl.pallas_call(
        functools.partial(_inproj_kernel, a_proj=a_proj, b_width=b_width),
        grid=(n // tm,),
        in_specs=[row(d), lay(g), lay(w), lay(qg), lay(kg)],
        out_specs=[row(a_proj), row(b_width), row(b_width), row(b_width), row(c_width)],
        out_shape=[jax.ShapeDtypeStruct((n, a_proj), F32),
                   jax.ShapeDtypeStruct((n, b_width), BF16),
                   jax.ShapeDtypeStruct((n, b_width), BF16),
                   jax.ShapeDtypeStruct((n, b_width), BF16),
                   jax.ShapeDtypeStruct((n, c_width), F32)],
        compiler_params=_params("parallel"),
        name="in_proj",
    )(x, g, w, qg, kg)


def _lane_lt_head(shape):
    return lax.broadcasted_iota(jnp.int32, shape, len(shape) - 1) < HEAD


def _head_sum(x):
    out = []
    for ls in range(0, x.shape[1], PAIR):
        xs = x[:, ls:ls + PAIR]
        first = _lane_lt_head(xs.shape)
        sum_a = jnp.sum(jnp.where(first, xs, 0.0), axis=-1, keepdims=True)
        sum_b = jnp.sum(jnp.where(first, 0.0, xs), axis=-1, keepdims=True)
        out.append(jnp.where(first, sum_a, sum_b))
    return jnp.concatenate(out, axis=1)


def _blockdiag(x2):
    first = _lane_lt_head(x2.shape)
    zero = jnp.zeros_like(x2)
    return jnp.concatenate([jnp.where(first, x2, zero), jnp.where(first, zero, x2)], axis=0)


def _rwkv_chunk_maps(insts):
    c = CHUNK
    b16 = lambda t: t.astype(BF16)
    t_idx = lax.broadcasted_iota(jnp.int32, (c, PAIR), 0)
    s_idx = lax.broadcasted_iota(jnp.int32, (c, PAIR), 1) % HEAD
    strict = t_idx > s_idx
    incl = t_idx >= s_idx
    zero = jnp.zeros((c, PAIR), F32)
    eye = jnp.where(t_idx == s_idx, 1.0, 0.0)
    row = lax.broadcasted_iota(jnp.int32, (PAIR, PAIR), 0)
    col = lax.broadcasted_iota(jnp.int32, (PAIR, PAIR), 1)
    same_head = (row < HEAD) == (col < HEAD)
    zero2 = jnp.zeros((PAIR, PAIR), F32)

    s_all = [_dot_nt(b16(jnp.concatenate([qa, qr], axis=0)),
                     b16(jnp.concatenate([_blockdiag(kb), _blockdiag(kk)], axis=0)))
             for qa, qr, kb, kk, _, _, _, _ in insts]
    a_ab = [jnp.where(strict, s[:c, :PAIR], zero) for s in s_all]
    a_ak = [jnp.where(strict, s[:c, PAIR:], zero) for s in s_all]
    a_rb = [jnp.where(incl, s[c:, :PAIR], zero) for s in s_all]
    a_rk = [jnp.where(incl, s[c:, PAIR:], zero) for s in s_all]
    av = [_dot(b16(jnp.concatenate([ak, rk], axis=0)), b16(_blockdiag(inst[6])))
          for ak, rk, inst in zip(a_ak, a_rk, insts)]

    same_block = lambda blk: (t_idx // blk) == (s_idx // blk)
    a0 = [jnp.where(same_block(INV_BLOCK), a, zero) for a in a_ab]
    a2 = [_dot(b16(a), b16(_blockdiag(a))) for a in a0]
    p1 = [eye + a for a in a0]
    st = [_dot(b16(jnp.concatenate([sq, p], axis=0)), b16(_blockdiag(sq))) for sq, p in zip(a2, p1)]
    p2 = [p + s[c:] for p, s in zip(p1, st)]
    tinv = [p + _dot(b16(p), b16(_blockdiag(s[:c]))) for p, s in zip(p2, st)]
    blk = INV_BLOCK
    while blk < c:
        off_diag = same_block(2 * blk) & jnp.logical_not(same_block(blk))
        x1 = [_dot(b16(jnp.where(off_diag, a, zero)), b16(_blockdiag(t))) for a, t in zip(a_ab, tinv)]
        tinv = [t + _dot(b16(t), b16(_blockdiag(x))) for t, x in zip(tinv, x1)]
        blk *= 2

    r1 = [_dot(b16(t), b16(jnp.concatenate([_blockdiag(inst[0]), _blockdiag(a[:c])], axis=1)))
          for t, a, inst in zip(tinv, av, insts)]
    r2 = [_dot(b16(rb), b16(jnp.concatenate([_blockdiag(r[:, :PAIR]), _blockdiag(r[:, PAIR:])], axis=1)))
          for rb, r in zip(a_rb, r1)]
    out = []
    for r1_i, r2_i, av_i, (qa, qr, kb, kk, kbe, kke, v, wlast_row) in zip(r1, r2, av, insts):
        qa_p, u0 = r1_i[:, :PAIR], r1_i[:, PAIR:]
        m = jnp.where(same_head, _dot_tn(b16(qa_p), b16(kbe)), zero2)
        m = m + jnp.where(row == col, jnp.broadcast_to(wlast_row, (PAIR, PAIR)), zero2)
        n = jnp.where(same_head,
                      _dot_tn(b16(jnp.concatenate([u0, v], axis=0)),
                              b16(jnp.concatenate([kbe, kke], axis=0))), zero2)
        out.append((qr + r2_i[:, :PAIR], r2_i[:, PAIR:] + av_i[c:], m, n))
    return out


def _rwkv_kernel(*refs, has_vres, tc, width):
    if has_vres:
        (pa_ref, mu_ref, wlh_ref, wll_ref, w0_ref, a0_ref, kk_ref, ka_ref, rk_ref, gng_ref, gnb_ref,
         vf_ref, v0_ref, vdh_ref, vdl_ref, vuh_ref, vul_ref, y_ref, prev_ref, s_ref, yraw_ref) = refs
    else:
        (pa_ref, mu_ref, wlh_ref, wll_ref, w0_ref, a0_ref, kk_ref, ka_ref, rk_ref, gng_ref, gnb_ref,
         y_ref, vf_ref, prev_ref, s_ref, yraw_ref) = refs
    n_pairs = width // PAIR
    n_chunks = tc // CHUNK

    @pl.when(pl.program_id(1) == 0)
    def _():
        prev_ref[...] = jnp.zeros_like(prev_ref)
        s_ref[...] = jnp.zeros_like(s_ref)

    p = pa_ref[...]
    row = lax.broadcasted_iota(jnp.int32, p.shape, 0)
    shifted = jnp.where(row == 0, jnp.broadcast_to(prev_ref[0:1, :], p.shape), pltpu.roll(p, 1, axis=0))
    prev_ref[0:1, :] = p[tc - 1:tc, :]
    p = p + mu_ref[...] * (shifted - p)

    r = p[:, :width]
    k = p[:, width:2 * width]
    v = p[:, 2 * width:3 * width]
    lo_in = p[:, 3 * width:]
    lane = lax.broadcasted_iota(jnp.int32, lo_in.shape, 1)
    lo_act = jnp.where(lane < LORA_W, jnp.tanh(lo_in),
                       jnp.where(lane < LORA_W + LORA_A, lo_in, _sigmoid(lo_in)))
    lo = _dot_x3(lo_act, wlh_ref[...], wll_ref[...])
    wz = w0_ref[...] + lo[:, :width]
    w = -(jnp.maximum(-wz, 0.0) + jnp.log(1.0 + jnp.exp(-jnp.abs(wz)))) - 0.5
    lw = -jnp.exp(w)
    a = _sigmoid(a0_ref[...] + lo[:, width:2 * width])
    g = lo[:, 2 * width:]

    if has_vres:
        v_lo = _dot_x3(v, vdh_ref[...], vdl_ref[...])
        gate = _sigmoid(v0_ref[...] + _dot_x3(v_lo, vuh_ref[...], vul_ref[...]))
        v = v + (vf_ref[...] - v) * gate
    else:
        vf_ref[...] = v

    kkx = k * kk_ref[...]
    kk = kkx / jnp.maximum(jnp.sqrt(_head_sum(kkx * kkx)), 1e-12)
    kmod = k * (1.0 + (a - 1.0) * ka_ref[...])

    grp = 2 * CHUNK
    ti = lax.broadcasted_iota(jnp.int32, (grp, grp), 0)
    tj = lax.broadcasted_iota(jnp.int32, (grp, grp), 1)
    tri = jnp.where(((ti // CHUNK) == (tj // CHUNK)) & (tj <= ti), 1.0, 0.0).astype(BF16)
    lw_hi, lw_lo = _split(lw, 2)
    cum = jnp.concatenate([_dot(tri, lw_hi[r0:r0 + grp]) + _dot(tri, lw_lo[r0:r0 + grp])
                           for r0 in range(0, tc, grp)], axis=0)
    tot = jnp.concatenate(
        [jnp.broadcast_to(cum[(ci + 1) * CHUNK - 1:(ci + 1) * CHUNK, :], (CHUNK, width)) for ci in range(n_chunks)],
        axis=0)
    w_inc = jnp.exp(cum)
    w_exc = jnp.exp(cum - lw)
    w_inv = jnp.exp(-cum)
    w_end = jnp.exp(tot - cum)
    w_tot = jnp.exp(tot)

    kka = kk * a
    qa_all = -kk * w_exc
    qr_all = r * w_inc
    kb_all = kka * w_inv
    kk_all = kmod * w_inv
    kbe_all = kka * w_end
    kke_all = kmod * w_end

    insts = []
    for ci in range(n_chunks):
        rs = slice(ci * CHUNK, (ci + 1) * CHUNK)
        for pi in range(n_pairs):
            ls = slice(pi * PAIR, (pi + 1) * PAIR)
            insts.append((qa_all[rs, ls], qr_all[rs, ls], kb_all[rs, ls], kk_all[rs, ls],
                          kbe_all[rs, ls], kke_all[rs, ls], v[rs, ls], w_tot[ci * CHUNK:ci * CHUNK + 1, ls]))
    maps = _rwkv_chunk_maps(insts)
    states = [s_ref[pi] for pi in range(n_pairs)]
    for ci in range(n_chunks):
        for pi in range(n_pairs):
            qr_p, y0, m, n = maps[ci * n_pairs + pi]
            s16 = states[pi].astype(BF16)
            yraw_ref[ci * CHUNK:(ci + 1) * CHUNK, pi * PAIR:(pi + 1) * PAIR] = _dot_nt(qr_p.astype(BF16), s16) + y0
            states[pi] = _dot(s16, m.astype(BF16)) + n
    for pi in range(n_pairs):
        s_ref[pi] = states[pi]

    y = yraw_ref[...]
    mean = _head_sum(y) * (1.0 / HEAD)
    yc = y - mean
    var = _head_sum(yc * yc) * (1.0 / HEAD)
    yn = yc * lax.rsqrt(var + GN_EPS) * gng_ref[...] + gnb_ref[...]
    bonus = _head_sum(r * kmod * rk_ref[...]) * v
    y_ref[...] = ((yn + bonus) * g).astype(y_ref.dtype)


def _rwkv(pa, vfirst, prm, l, *, tc):
    b, t, a_proj = pa.shape
    width = prm["w0"].shape[-1]
    has_vres = vfirst is not None
    tile = lambda w_: pl.BlockSpec((None, tc, w_), lambda i, j: (i, j, 0))
    names = ["mu", "wl_hi", "wl_lo", "w0", "a0", "k_k", "k_a", "r_k", "gn_g", "gn_b"]
    args = [pa] + [prm[nm] for nm in names]
    in_specs = [tile(a_proj)] + [_layer_spec(prm[nm], l) for nm in names]
    y_shape = jax.ShapeDtypeStruct((b, t, width), BF16)
    if has_vres:
        extra = [prm["v0"], prm["vd_hi"], prm["vd_lo"], prm["vu_hi"], prm["vu_lo"]]
        args += [vfirst] + extra
        in_specs += [tile(width)] + [_layer_spec(a, l - 1) for a in extra]
        out_specs, out_shape = tile(width), y_shape
    else:
        out_specs = [tile(width), tile(width)]
        out_shape = [y_shape, jax.ShapeDtypeStruct((b, t, width), F32)]
    return pl.pallas_call(
        functools.partial(_rwkv_kernel, has_vres=has_vres, tc=tc, width=width),
        grid=(b, t // tc),
        in_specs=in_specs,
        out_specs=out_specs,
        out_shape=out_shape,
        scratch_shapes=[pltpu.VMEM((8, a_proj), F32),
                        pltpu.VMEM((width // PAIR, PAIR, PAIR), F32),
                        pltpu.VMEM((tc, width), F32)],
        compiler_params=_params("parallel", "arbitrary"),
        name="rwkv",
    )(*args)


def _attn_pool_kernel(q_ref, k_ref, v_ref, bias_ref, u_ref, pw_ref, ps_ref, yb_ref, yc_ref,
                      kpad_ref, vpad_ref, upad_ref, *, width, qb):
    step = pl.program_id(1)
    rows = qb * CHUNK
    win_rows = (PREV_CHUNKS + qb) * CHUNK

    @pl.when(step == 0)
    def _():
        kpad_ref[:PREV_CHUNKS * CHUNK, :] = jnp.zeros((PREV_CHUNKS * CHUNK, width), BF16)
        vpad_ref[:PREV_CHUNKS * CHUNK, :] = jnp.zeros((PREV_CHUNKS * CHUNK, width), BF16)
        kpad_ref[PREV_CHUNKS * CHUNK:, :] = k_ref[...]
        vpad_ref[PREV_CHUNKS * CHUNK:, :] = v_ref[...]
        upad_ref[:CHUNK, :] = jnp.zeros((CHUNK, upad_ref.shape[1]), F32)
        upad_ref[CHUNK:, :] = u_ref[...]

    start = pl.multiple_of(step * rows, rows)
    q = q_ref[...]
    slot = lax.broadcasted_iota(jnp.int32, (1, win_rows), 1)
    before_start = jnp.where(slot >= PREV_CHUNKS * CHUNK - start, 0.0, NEG_INF)
    first = _lane_lt_head((rows, PAIR))
    n_pairs = width // PAIR
    scores = []
    for pi in range(n_pairs):
        ls = slice(pi * PAIR, (pi + 1) * PAIR)
        q2 = q[:, ls]
        zq = jnp.zeros_like(q2)
        qs = jnp.concatenate([jnp.where(first, q2, zq), jnp.where(first, zq, q2)], axis=0)
        scores.append(_dot_nt(qs, kpad_ref[pl.ds(start, win_rows), ls]) + bias_ref[pi] + before_start)
    probs, sums = [], []
    for s in scores:
        e = jnp.exp2(s - jnp.max(s, axis=-1, keepdims=True))
        sums.append(jnp.sum(e, axis=-1, keepdims=True))
        probs.append(e.astype(BF16))
    for pi in range(n_pairs):
        ls = slice(pi * PAIR, (pi + 1) * PAIR)
        o = _dot(probs[pi], vpad_ref[pl.ds(start, win_rows), ls]) / sums[pi]
        yb_ref[:, ls] = jnp.where(first, o[:rows], o[rows:]).astype(yb_ref.dtype)

    x = upad_ref[pl.ds(start, rows + CHUNK), :]
    cw = x.shape[1]
    acc, span, win_sums = x, 1, []
    for win in POOL_WINDOWS:
        while span < win:
            acc = acc + pltpu.roll(acc, span, axis=0)
            span *= 2
        win_sums.append(acc)
    lane = lax.broadcasted_iota(jnp.int32, (rows, cw), 1)
    t1 = (lax.broadcasted_iota(jnp.int32, (rows, cw), 0) + start + 1).astype(F32)
    pooled = jnp.zeros((rows, cw), F32)
    for gi, win in enumerate(POOL_WINDOWS):
        grp = (lane >= gi * HEAD) & (lane < (gi + 1) * HEAD)
        pooled = jnp.where(grp, win_sums[gi][CHUNK:] / jnp.minimum(t1, float(win)), pooled)
    pooled = pooled - x[CHUNK:]
    yc_ref[...] = (_dot(pooled.astype(BF16), pw_ref[...]) * ps_ref[...]).astype(yc_ref.dtype)


def _attn_pool(q, k, v, bias, u, pw, ps, l, *, qb):
    b, t, width = q.shape
    cw = u.shape[-1]
    rows = qb * CHUNK
    lay = lambda a: _layer_spec(a, l)
    seq = lambda a: pl.BlockSpec((None,) + a.shape[1:], lambda i, j: (i, 0, 0))
    tile = lambda w_: pl.BlockSpec((None, rows, w_), lambda i, j: (i, j, 0))
    return pl.pallas_call(
        functools.partial(_attn_pool_kernel, width=width, qb=qb),
        grid=(b, t // rows),
        in_specs=[tile(width), seq(k), seq(v), lay(bias), seq(u), lay(pw), lay(ps)],
        out_specs=[tile(width), tile(cw)],
        out_shape=[jax.ShapeDtypeStruct((b, t, width), BF16), jax.ShapeDtypeStruct((b, t, cw), BF16)],
        scratch_shapes=[pltpu.VMEM((t + PREV_CHUNKS * CHUNK, width), BF16),
                        pltpu.VMEM((t + PREV_CHUNKS * CHUNK, width), BF16),
                        pltpu.VMEM((t + CHUNK, cw), F32)],
        compiler_params=_params("parallel", "arbitrary"),
        name="attn_pool",
    )(q, k, v, bias, u, pw, ps)


def _out_cross_kernel(x_ref, ya_ref, yb_ref, yc_ref, woa_ref, wob_ref, woc_ref, g_ref, wq_ref, qg_ref,
                      k_ref, v_ref, wo_ref, o_ref, att_ref, *, heads):
    w16 = lambda ref: ref[...].astype(BF16)
    x = (x_ref[...] + _dot(ya_ref[...], w16(woa_ref)) + _dot(yb_ref[...], w16(wob_ref))
         + _dot(yc_ref[...], w16(woc_ref)))
    h = _rms(x, g_ref[...]).astype(BF16)
    q = _dot(h, w16(wq_ref))
    hd = q.shape[1] // heads
    cols = [slice(hi * hd, (hi + 1) * hd) for hi in range(heads)]
    qh = [(_rms(q[:, cs], qg_ref[...]) * (hd ** -0.5 * LOG2_E)).astype(BF16) for cs in cols]
    s = [_dot_nt(qi, k_ref[:, cs]) for qi, cs in zip(qh, cols)]
    e = [jnp.exp2(si - jnp.max(si, axis=-1, keepdims=True)) for si in s]
    for ei, cs in zip(e, cols):
        oh = _dot(ei.astype(BF16), v_ref[:, cs]) / jnp.sum(ei, axis=-1, keepdims=True)
        att_ref[:, cs] = oh.astype(BF16)
    o_ref[...] = x + _dot(att_ref[...], w16(wo_ref))


def _out_cross(x, ya, yb, yc, w_out, g, wq, qg, k, v, wo, l, *, tm, heads):
    b, t, d = x.shape
    lay = lambda a: _layer_spec(a, l)
    tile = lambda a: pl.BlockSpec((None, tm, a.shape[-1]), lambda i, j: (i, j, 0))
    mem = lambda a: pl.BlockSpec((None, None) + a.shape[2:], lambda i, j: (l, i, 0, 0))
    wa, wb, wc = ya.shape[-1], yb.shape[-1], yc.shape[-1]
    assert wa == wb and (wa + wb) % wc == 0, "row blocks of w_out must be block-aligned"
    w_rows = lambda rows, blk: pl.BlockSpec((None, rows, d), lambda i, j: (l, blk, 0))
    return pl.pallas_call(
        functools.partial(_out_cross_kernel, heads=heads),
        grid=(b, t // tm),
        in_specs=[tile(x), tile(ya), tile(yb), tile(yc), w_rows(wa, 0), w_rows(wb, 1), w_rows(wc, (wa + wb) // wc),
                  lay(g), lay(wq), lay(qg), mem(k), mem(v), lay(wo)],
        out_specs=tile(x),
        out_shape=jax.ShapeDtypeStruct((b, t, d), F32),
        scratch_shapes=[pltpu.VMEM((tm, d), BF16)],
        compiler_params=_params("parallel", "parallel"),
        name="out_cross",
    )(x, ya, yb, yc, w_out, w_out, w_out, g, wq, qg, k, v, wo)


def _mem_kv_kernel(mem_ref, g_ref, w_ref, kg_ref, k_ref, v_ref, *, heads):
    h = _rms(mem_ref[...], g_ref[...]).astype(BF16)
    kv = _dot(h, w_ref[...].astype(BF16))
    d = kv.shape[1] // 2
    hd = d // heads
    for hi in range(heads):
        cs = slice(hi * hd, (hi + 1) * hd)
        k_ref[:, cs] = _rms(kv[:, cs], kg_ref[...]).astype(BF16)
    v_ref[...] = kv[:, d:].astype(BF16)


def _mem_kv(mem, g, wkv, kg, *, heads):
    b, m, d = mem.shape
    depth = wkv.shape[0]
    per_layer = lambda a: pl.BlockSpec((None,) + a.shape[1:], lambda l, i: (l,) + (0,) * (a.ndim - 1))
    out = pl.BlockSpec((None, None, m, d), lambda l, i: (l, i, 0, 0))
    return pl.pallas_call(
        functools.partial(_mem_kv_kernel, heads=heads),
        grid=(depth, b),
        in_specs=[pl.BlockSpec((None, m, d), lambda l, i: (i, 0, 0)), per_layer(g), per_layer(wkv), per_layer(kg)],
        out_specs=[out, out],
        out_shape=[jax.ShapeDtypeStruct((depth, b, m, d), BF16)] * 2,
        compiler_params=_params("parallel", "parallel"),
        name="mem_kv",
    )(mem, g, wkv, kg)


def _split_weight(w):
    hi = w.astype(BF16)
    return hi, (w - hi.astype(F32)).astype(BF16)


def _rel_bias_band(rel_bias):
    heads, n_rel = rel_bias.shape
    ext_len = BAND + CHUNK - 1
    ext = jnp.concatenate([rel_bias, jnp.broadcast_to(rel_bias[:, -1:], (heads, ext_len - n_rel))], axis=1)
    rev = ext[:, ::-1]
    return jnp.stack([rev[:, CHUNK - 1 - i:CHUNK - 1 - i + BAND] for i in range(CHUNK)], axis=1)


def _rel_bias_window(rel_bias, qb):
    band = _rel_bias_band(rel_bias)
    heads = band.shape[0]
    per_chunk = [jnp.pad(band, ((0, 0), (0, 0), (qi * CHUNK, (qb - 1 - qi) * CHUNK)), constant_values=NEG_INF)
                 for qi in range(qb)]
    win = jnp.stack(per_chunk, axis=1)
    return win.reshape(heads // 2, 2 * qb * CHUNK, (PREV_CHUNKS + qb) * CHUNK)


def _lora_weight(w_up, a_up, g_up):
    depth, _, width = w_up.shape
    z = lambda r: jnp.zeros((depth, r, width), F32)
    return jnp.concatenate([
        jnp.concatenate([w_up, z(LORA_W), z(LORA_W)], axis=2),
        jnp.concatenate([z(LORA_A), a_up, z(LORA_A)], axis=2),
        jnp.concatenate([z(LORA_G), z(LORA_G), g_up], axis=2)], axis=1)


def _pool_weight(pool_w):
    depth, groups, cg, _ = pool_w.shape
    rows = []
    for gi in range(groups):
        blocks = [pool_w[:, gi] if gj == gi else jnp.zeros((depth, cg, cg), pool_w.dtype) for gj in range(groups)]
        rows.append(jnp.concatenate(blocks, axis=2))
    return jnp.concatenate(rows, axis=1)


def _stack_rows(a):
    return a.reshape(a.shape[0], 1, -1)


def _attn_pool_params(rel_bias, pool_w, pool_scale, qb):
    return {"bias": jax.vmap(functools.partial(_rel_bias_window, qb=qb))(rel_bias * LOG2_E),
            "pw": _pool_weight(pool_w).astype(BF16), "ps": _stack_rows(pool_scale)}


def _rwkv_params(a_mu, a_w0, a_w_up, a_a0, a_a_up, a_g_up, a_k_k, a_k_a, a_r_k, a_gn_g, a_gn_b,
                 a_v0, a_v_down, a_v_up):
    wl_hi, wl_lo = _split_weight(_lora_weight(a_w_up, a_a_up, a_g_up))
    pad_lanes = LANES - LORA_V
    vd_hi, vd_lo = _split_weight(jnp.pad(a_v_down, ((0, 0), (0, 0), (0, pad_lanes))))
    vu_hi, vu_lo = _split_weight(jnp.pad(a_v_up, ((0, 0), (0, pad_lanes), (0, 0))))
    return {"mu": _stack_rows(a_mu), "wl_hi": wl_hi, "wl_lo": wl_lo, "w0": _stack_rows(a_w0),
            "a0": _stack_rows(a_a0), "k_k": _stack_rows(a_k_k), "k_a": _stack_rows(a_k_a),
            "r_k": _stack_rows(a_r_k), "gn_g": _stack_rows(a_gn_g), "gn_b": _stack_rows(a_gn_b),
            "v0": _stack_rows(a_v0),
            "vd_hi": vd_hi, "vd_lo": vd_lo, "vu_hi": vu_hi, "vu_lo": vu_lo}


def kernel(x, mem, norm_ffn1, ffn1_wi, ffn1_wo, norm_mix, w_in, w_out, a_mu, a_w0, a_w_up, a_a0, a_a_up, a_g_up,
           a_k_k, a_k_a, a_r_k, a_gn_g, a_gn_b, a_v0, a_v_down, a_v_up, b_q_gain, b_k_gain, b_rel_bias,
           c_pool_w, c_pool_scale, norm_cross, norm_mem, x_wq, x_wkv, x_wo, x_q_gain, x_k_gain,
           norm_ffn2, ffn2_wi, ffn2_wo):
    b, t, d = x.shape
    depth = w_in.shape[0]
    a_proj = a_mu.shape[-1]
    b_width = b_rel_bias.shape[1] * HEAD
    c_width = c_pool_scale.shape[-1]
    x_heads = d // x_q_gain.shape[-1]
    n_tok = b * t

    ffn1 = (_stack_rows(norm_ffn1), ffn1_wi, ffn1_wo)
    ffn2 = (_stack_rows(norm_ffn2), ffn2_wi, ffn2_wo)
    q_gain = _stack_rows(jnp.tile(b_q_gain, (1, b_width // HEAD))) * (HEAD ** -0.5 * LOG2_E)
    k_gain = _stack_rows(jnp.tile(b_k_gain, (1, b_width // HEAD)))
    rwkv_prm = _rwkv_params(a_mu, a_w0, a_w_up, a_a0, a_a_up, a_g_up, a_k_k, a_k_a, a_r_k, a_gn_g, a_gn_b,
                            a_v0, a_v_down, a_v_up)
    ap_prm = _attn_pool_params(b_rel_bias, c_pool_w, c_pool_scale, ATTN_QB)
    mem_k, mem_v = _mem_kv(mem, _stack_rows(norm_mem), x_wkv, _stack_rows(x_k_gain), heads=x_heads)

    seq = lambda a: a.reshape(b, t, a.shape[-1])
    x = x.reshape(n_tok, d)
    v_first = None
    for l in range(depth):
        x = _ffn(x, *ffn1, l, tm=FFN_TM, tf=FFN_TF)
        pa, q, k, v, u = _inproj(x, _stack_rows(norm_mix), w_in, q_gain, k_gain, l,
                                 tm=ROW_TM, a_proj=a_proj, b_width=b_width, c_width=c_width)
        if l == 0:
            y_a, v_first = _rwkv(seq(pa), None, rwkv_prm, l, tc=RWKV_TC)
        else:
            y_a = _rwkv(seq(pa), v_first, rwkv_prm, l, tc=RWKV_TC)
        y_b, y_c = _attn_pool(seq(q), seq(k), seq(v), ap_prm["bias"], seq(u), ap_prm["pw"], ap_prm["ps"], l,
                              qb=ATTN_QB)
        x = _out_cross(seq(x), y_a, y_b, y_c, w_out, _stack_rows(norm_cross), x_wq, _stack_rows(x_q_gain),
                       mem_k, mem_v, x_wo, l, tm=ROW_TM, heads=x_heads)
        x = _ffn(x.reshape(n_tok, d), *ffn2, l, tm=FFN_TM, tf=FFN_TF)
    return x.reshape(b, t, d)
```

```python
import functools

import numpy as np
import jax
import jax.numpy as jnp
from jax import lax
from jax.experimental import pallas as pl
from jax.experimental.pallas import tpu as pltpu

F32 = jnp.float32
BF16 = jnp.bfloat16

LANES = 128
HEAD = 64
PAIR = 2 * HEAD
CHUNK = 64
INV_BLOCK = 8
PREV_CHUNKS = 8
BAND = (PREV_CHUNKS + 1) * CHUNK
REL_MAX = 256
POOL_WINDOWS = (2, 4, 8, 16)
LORA_W, LORA_A, LORA_G, LORA_V = 32, 32, 64, 32
RMS_EPS = 1e-6
GN_EPS = 64e-5
NEG_INF = -1e30
LOG2_E = 1.4426950408889634
VMEM_LIMIT = 56 * 1024 * 1024
FFN_TM = 1024
FFN_TF = 256
ROW_TM = 1024
RWKV_TC = 512
ATTN_QB = 4


def _dot(a, b, precision=None):
    return jnp.dot(a, b, preferred_element_type=F32, precision=precision)


def _dot_nt(a, b):
    return lax.dot_general(a, b, (((1,), (1,)), ((), ())), preferred_element_type=F32)


def _dot_tn(a, b):
    return lax.dot_general(a, b, (((0,), (0,)), ((), ())), preferred_element_type=F32)


def _split(x, terms):
    parts = []
    for _ in range(terms):
        hi = x.astype(BF16)
        parts.append(hi)
        x = x - hi.astype(F32)
    return parts


def _dot_x3(x, w_hi, w_lo):
    x_hi, x_lo = _split(x, 2)
    return _dot(x_hi, w_hi) + (_dot(x_lo, w_hi) + _dot(x_hi, w_lo))


def _rms(x, g):
    return x * lax.rsqrt(jnp.mean(x * x, axis=-1, keepdims=True) + RMS_EPS) * g


def _sigmoid(x):
    return 1.0 / (1.0 + jnp.exp(-x))


def _params(*sem):
    return pltpu.CompilerParams(dimension_semantics=sem, vmem_limit_bytes=VMEM_LIMIT)


def _layer_spec(a, l):
    zeros = (0,) * (a.ndim - 1)
    return pl.BlockSpec((None,) + a.shape[1:], lambda *_: (l,) + zeros, pipeline_mode=pl.Buffered(1))


def _ffn_kernel(x_ref, g_ref, wi_ref, wo_ref, o_ref, *, tf):
    x = x_ref[...]
    h = _rms(x, g_ref[...]).astype(BF16)
    dff = wo_ref.shape[0]
    acc = None
    for c0 in range(0, dff, tf):
        gate = _dot(h, wi_ref[:, c0:c0 + tf].astype(BF16))
        up = _dot(h, wi_ref[:, dff + c0:dff + c0 + tf].astype(BF16))
        act = (gate * _sigmoid(gate) * up).astype(BF16)
        part = _dot(act, wo_ref[c0:c0 + tf, :].astype(BF16))
        acc = part if acc is None else acc + part
    o_ref[...] = x + 0.5 * acc


def _ffn(x, g, wi, wo, l, *, tm, tf):
    n, d = x.shape
    return pl.pallas_call(
        functools.partial(_ffn_kernel, tf=tf),
        grid=(n // tm,),
        in_specs=[pl.BlockSpec((tm, d), lambda i: (i, 0)), _layer_spec(g, l), _layer_spec(wi, l), _layer_spec(wo, l)],
        out_specs=pl.BlockSpec((tm, d), lambda i: (i, 0)),
        out_shape=jax.ShapeDtypeStruct((n, d), F32),
        compiler_params=_params("parallel"),
        name="ffn",
    )(x, g, wi, wo)


def _inproj_kernel(x_ref, g_ref, w_ref, qg_ref, kg_ref,
                   pa_ref, q_ref, k_ref, v_ref, u_ref, *, a_proj, b_width):
    h = _rms(x_ref[...], g_ref[...]).astype(BF16)
    p = _dot(h, w_ref[...].astype(BF16))
    pa_ref[...] = p[:, :a_proj]
    q = p[:, a_proj:a_proj + b_width]
    k = p[:, a_proj + b_width:a_proj + 2 * b_width]
    qms = _head_sum(q * q) * (1.0 / HEAD)
    kms = _head_sum(k * k) * (1.0 / HEAD)
    q_ref[...] = (q * lax.rsqrt(qms + RMS_EPS) * qg_ref[...]).astype(BF16)
    k_ref[...] = (k * lax.rsqrt(kms + RMS_EPS) * kg_ref[...]).astype(BF16)
    v_ref[...] = p[:, a_proj + 2 * b_width:a_proj + 3 * b_width].astype(BF16)
    u_ref[...] = p[:, a_proj + 3 * b_width:]


def _inproj(x, g, w, qg, kg, l, *, tm, a_proj, b_width, c_width):
    n, d = x.shape
    row = lambda w_: pl.BlockSpec((tm, w_), lambda i: (i, 0))
    lay = lambda a: _layer_spec(a, l)
    return pl.pallas_call(
        functools.partial(_inproj_kernel, a_proj=a_proj, b_width=b_width),
        grid=(n // tm,),
        in_specs=[row(d), lay(g), lay(w), lay(qg), lay(kg)],
        out_specs=[row(a_proj), row(b_width), row(b_width), row(b_width), row(c_width)],
        out_shape=[jax.ShapeDtypeStruct((n, a_proj), F32),
                   jax.ShapeDtypeStruct((n, b_width), BF16),
                   jax.ShapeDtypeStruct((n, b_width), BF16),
                   jax.ShapeDtypeStruct((n, b_width), BF16),
                   jax.ShapeDtypeStruct((n, c_width), F32)],
        compiler_params=_params("parallel"),
        name="in_proj",
    )(x, g, w, qg, kg)


def _lane_lt_head(shape):
    return lax.broadcasted_iota(jnp.int32, shape, len(shape) - 1) < HEAD


def _head_sum(x):
    out = []
    for ls in range(0, x.shape[1], PAIR):
        xs = x[:, ls:ls + PAIR]
        first = _lane_lt_head(xs.shape)
        sum_a = jnp.sum(jnp.where(first, xs, 0.0), axis=-1, keepdims=True)
        sum_b = jnp.sum(jnp.where(first, 0.0, xs), axis=-1, keepdims=True)
        out.append(jnp.where(first, sum_a, sum_b))
    return jnp.concatenate(out, axis=1)


def _blockdiag(x2):
    first = _lane_lt_head(x2.shape)
    zero = jnp.zeros_like(x2)
    return jnp.concatenate([jnp.where(first, x2, zero), jnp.where(first, zero, x2)], axis=0)


def _rwkv_chunk_maps(insts):
    c = CHUNK
    b16 = lambda t: t.astype(BF16)
    t_idx = lax.broadcasted_iota(jnp.int32, (c, PAIR), 0)
    s_idx = lax.broadcasted_iota(jnp.int32, (c, PAIR), 1) % HEAD
    strict = t_idx > s_idx
    incl = t_idx >= s_idx
    zero = jnp.zeros((c, PAIR), F32)
    eye = jnp.where(t_idx == s_idx, 1.0, 0.0)
    row = lax.broadcasted_iota(jnp.int32, (PAIR, PAIR), 0)
    col = lax.broadcasted_iota(jnp.int32, (PAIR, PAIR), 1)
    same_head = (row < HEAD) == (col < HEAD)
    zero2 = jnp.zeros((PAIR, PAIR), F32)

    s_all = [_dot_nt(b16(jnp.concatenate([qa, qr], axis=0)),
                     b16(jnp.concatenate([_blockdiag(kb), _blockdiag(kk)], axis=0)))
             for qa, qr, kb, kk, _, _, _, _ in insts]
    a_ab = [jnp.where(strict, s[:c, :PAIR], zero) for s in s_all]
    a_ak = [jnp.where(strict, s[:c, PAIR:], zero) for s in s_all]
    a_rb = [jnp.where(incl, s[c:, :PAIR], zero) for s in s_all]
    a_rk = [jnp.where(incl, s[c:, PAIR:], zero) for s in s_all]
    av = [_dot(b16(jnp.concatenate([ak, rk], axis=0)), b16(_blockdiag(inst[6])))
          for ak, rk, inst in zip(a_ak, a_rk, insts)]

    same_block = lambda blk: (t_idx // blk) == (s_idx // blk)
    a0 = [jnp.where(same_block(INV_BLOCK), a, zero) for a in a_ab]
    a2 = [_dot(b16(a), b16(_blockdiag(a))) for a in a0]
    p1 = [eye + a for a in a0]
    st = [_dot(b16(jnp.concatenate([sq, p], axis=0)), b16(_blockdiag(sq))) for sq, p in zip(a2, p1)]
    p2 = [p + s[c:] for p, s in zip(p1, st)]
    tinv = [p + _dot(b16(p), b16(_blockdiag(s[:c]))) for p, s in zip(p2, st)]
    blk = INV_BLOCK
    while blk < c:
        off_diag = same_block(2 * blk) & jnp.logical_not(same_block(blk))
        x1 = [_dot(b16(jnp.where(off_diag, a, zero)), b16(_blockdiag(t))) for a, t in zip(a_ab, tinv)]
        tinv = [t + _dot(b16(t), b16(_blockdiag(x))) for t, x in zip(tinv, x1)]
        blk *= 2

    r1 = [_dot(b16(t), b16(jnp.concatenate([_blockdiag(inst[0]), _blockdiag(a[:c])], axis=1)))
          for t, a, inst in zip(tinv, av, insts)]
    r2 = [_dot(b16(rb), b16(jnp.concatenate([_blockdiag(r[:, :PAIR]), _blockdiag(r[:, PAIR:])], axis=1)))
          for rb, r in zip(a_rb, r1)]
    out = []
    for r1_i, r2_i, av_i, (qa, qr, kb, kk, kbe, kke, v, wlast_row) in zip(r1, r2, av, insts):
        qa_p, u0 = r1_i[:, :PAIR], r1_i[:, PAIR:]
        m = jnp.where(same_head, _dot_tn(b16(qa_p), b16(kbe)), zero2)
        m = m + jnp.where(row == col, jnp.broadcast_to(wlast_row, (PAIR, PAIR)), zero2)
        n = jnp.where(same_head,
                      _dot_tn(b16(jnp.concatenate([u0, v], axis=0)),
                              b16(jnp.concatenate([kbe, kke], axis=0))), zero2)
        out.append((qr + r2_i[:, :PAIR], r2_i[:, PAIR:] + av_i[c:], m, n))
    return out


def _rwkv_kernel(*refs, has_vres, tc, width):
    if has_vres:
        (pa_ref, mu_ref, wlh_ref, wll_ref, w0_ref, a0_ref, kk_ref, ka_ref, rk_ref, gng_ref, gnb_ref,
         vf_ref, v0_ref, vdh_ref, vdl_ref, vuh_ref, vul_ref, y_ref, prev_ref, s_ref, yraw_ref) = refs
    else:
        (pa_ref, mu_ref, wlh_ref, wll_ref, w0_ref, a0_ref, kk_ref, ka_ref, rk_ref, gng_ref, gnb_ref,
         y_ref, vf_ref, prev_ref, s_ref, yraw_ref) = refs
    n_pairs = width // PAIR
    n_chunks = tc // CHUNK

    @pl.when(pl.program_id(1) == 0)
    def _():
        prev_ref[...] = jnp.zeros_like(prev_ref)
        s_ref[...] = jnp.zeros_like(s_ref)

    p = pa_ref[...]
    row = lax.broadcasted_iota(jnp.int32, p.shape, 0)
    shifted = jnp.where(row == 0, jnp.broadcast_to(prev_ref[0:1, :], p.shape), pltpu.roll(p, 1, axis=0))
    prev_ref[0:1, :] = p[tc - 1:tc, :]
    p = p + mu_ref[...] * (shifted - p)

    r = p[:, :width]
    k = p[:, width:2 * width]
    v = p[:, 2 * width:3 * width]
    lo_in = p[:, 3 * width:]
    lane = lax.broadcasted_iota(jnp.int32, lo_in.shape, 1)
    lo_act = jnp.where(lane < LORA_W, jnp.tanh(lo_in),
                       jnp.where(lane < LORA_W + LORA_A, lo_in, _sigmoid(lo_in)))
    lo = _dot_x3(lo_act, wlh_ref[...], wll_ref[...])
    wz = w0_ref[...] + lo[:, :width]
    w = -(jnp.maximum(-wz, 0.0) + jnp.log(1.0 + jnp.exp(-jnp.abs(wz)))) - 0.5
    lw = -jnp.exp(w)
    a = _sigmoid(a0_ref[...] + lo[:, width:2 * width])
    g = lo[:, 2 * width:]

    if has_vres:
        v_lo = _dot_x3(v, vdh_ref[...], vdl_ref[...])
        gate = _sigmoid(v0_ref[...] + _dot_x3(v_lo, vuh_ref[...], vul_ref[...]))
        v = v + (vf_ref[...] - v) * gate
    else:
        vf_ref[...] = v

    kkx = k * kk_ref[...]
    kk = kkx / jnp.maximum(jnp.sqrt(_head_sum(kkx * kkx)), 1e-12)
    kmod = k * (1.0 + (a - 1.0) * ka_ref[...])

    grp = 2 * CHUNK
    ti = lax.broadcasted_iota(jnp.int32, (grp, grp), 0)
    tj = lax.broadcasted_iota(jnp.int32, (grp, grp), 1)
    tri = jnp.where(((ti // CHUNK) == (tj // CHUNK)) & (tj <= ti), 1.0, 0.0).astype(BF16)
    lw_hi, lw_lo = _split(lw, 2)
    cum = jnp.concatenate([_dot(tri, lw_hi[r0:r0 + grp]) + _dot(tri, lw_lo[r0:r0 + grp])
                           for r0 in range(0, tc, grp)], axis=0)
    tot = jnp.concatenate(
        [jnp.broadcast_to(cum[(ci + 1) * CHUNK - 1:(ci + 1) * CHUNK, :], (CHUNK, width)) for ci in range(n_chunks)],
        axis=0)
    w_inc = jnp.exp(cum)
    w_exc = jnp.exp(cum - lw)
    w_inv = jnp.exp(-cum)
    w_end = jnp.exp(tot - cum)
    w_tot = jnp.exp(tot)

    kka = kk * a
    qa_all = -kk * w_exc
    qr_all = r * w_inc
    kb_all = kka * w_inv
    kk_all = kmod * w_inv
    kbe_all = kka * w_end
    kke_all = kmod * w_end

    insts = []
    for ci in range(n_chunks):
        rs = slice(ci * CHUNK, (ci + 1) * CHUNK)
        for pi in range(n_pairs):
            ls = slice(pi * PAIR, (pi + 1) * PAIR)
            insts.append((qa_all[rs, ls], qr_all[rs, ls], kb_all[rs, ls], kk_all[rs, ls],
                          kbe_all[rs, ls], kke_all[rs, ls], v[rs, ls], w_tot[ci * CHUNK:ci * CHUNK + 1, ls]))
    maps = _rwkv_chunk_maps(insts)
    states = [s_ref[pi] for pi in range(n_pairs)]
    for ci in range(n_chunks):
        for pi in range(n_pairs):
            qr_p, y0, m, n = maps[ci * n_pairs + pi]
            s16 = states[pi].astype(BF16)
            yraw_ref[ci * CHUNK:(ci + 1) * CHUNK, pi * PAIR:(pi + 1) * PAIR] = _dot_nt(qr_p.astype(BF16), s16) + y0
            states[pi] = _dot(s16, m.astype(BF16)) + n
    for pi in range(n_pairs):
        s_ref[pi] = states[pi]

    y = yraw_ref[...]
    mean = _head_sum(y) * (1.0 / HEAD)
    yc = y - mean
    var = _head_sum(yc * yc) * (1.0 / HEAD)
    yn = yc * lax.rsqrt(var + GN_EPS) * gng_ref[...] + gnb_ref[...]
    bonus = _head_sum(r * kmod * rk_ref[...]) * v
    y_ref[...] = ((yn + bonus) * g).astype(y_ref.dtype)


def _rwkv(pa, vfirst, prm, l, *, tc):
    b, t, a_proj = pa.shape
    width = prm["w0"].shape[-1]
    has_vres = vfirst is not None
    tile = lambda w_: pl.BlockSpec((None, tc, w_), lambda i, j: (i, j, 0))
    names = ["mu", "wl_hi", "wl_lo", "w0", "a0", "k_k", "k_a", "r_k", "gn_g", "gn_b"]
    args = [pa] + [prm[nm] for nm in names]
    in_specs = [tile(a_proj)] + [_layer_spec(prm[nm], l) for nm in names]
    y_shape = jax.ShapeDtypeStruct((b, t, width), BF16)
    if has_vres:
        extra = [prm["v0"], prm["vd_hi"], prm["vd_lo"], prm["vu_hi"], prm["vu_lo"]]
        args += [vfirst] + extra
        in_specs += [tile(width)] + [_layer_spec(a, l - 1) for a in extra]
        out_specs, out_shape = tile(width), y_shape
    else:
        out_specs = [tile(width), tile(width)]
        out_shape = [y_shape, jax.ShapeDtypeStruct((b, t, width), F32)]
    return pl.pallas_call(
        functools.partial(_rwkv_kernel, has_vres=has_vres, tc=tc, width=width),
        grid=(b, t // tc),
        in_specs=in_specs,
        out_specs=out_specs,
        out_shape=out_shape,
        scratch_shapes=[pltpu.VMEM((8, a_proj), F32),
                        pltpu.VMEM((width // PAIR, PAIR, PAIR), F32),
                        pltpu.VMEM((tc, width), F32)],
        compiler_params=_params("parallel", "arbitrary"),
        name="rwkv",
    )(*args)


def _attn_pool_kernel(q_ref, k_ref, v_ref, bias_ref, u_ref, pw_ref, ps_ref, yb_ref, yc_ref,
                      kpad_ref, vpad_ref, upad_ref, *, width, qb):
    step = pl.program_id(1)
    rows = qb * CHUNK
    win_rows = (PREV_CHUNKS + qb) * CHUNK

    @pl.when(step == 0)
    def _():
        kpad_ref[:PREV_CHUNKS * CHUNK, :] = jnp.zeros((PREV_CHUNKS * CHUNK, width), BF16)
        vpad_ref[:PREV_CHUNKS * CHUNK, :] = jnp.zeros((PREV_CHUNKS * CHUNK, width), BF16)
        kpad_ref[PREV_CHUNKS * CHUNK:, :] = k_ref[...]
        vpad_ref[PREV_CHUNKS * CHUNK:, :] = v_ref[...]
        upad_ref[:CHUNK, :] = jnp.zeros((CHUNK, upad_ref.shape[1]), F32)
        upad_ref[CHUNK:, :] = u_ref[...]

    start = pl.multiple_of(step * rows, rows)
    q = q_ref[...]
    slot = lax.broadcasted_iota(jnp.int32, (1, win_rows), 1)
    before_start = jnp.where(slot >= PREV_CHUNKS * CHUNK - start, 0.0, NEG_INF)
    first = _lane_lt_head((rows, PAIR))
    n_pairs = width // PAIR
    scores = []
    for pi in range(n_pairs):
        ls = slice(pi * PAIR, (pi + 1) * PAIR)
        q2 = q[:, ls]
        zq = jnp.zeros_like(q2)
        qs = jnp.concatenate([jnp.where(first, q2, zq), jnp.where(first, zq, q2)], axis=0)
        scores.append(_dot_nt(qs, kpad_ref[pl.ds(start, win_rows), ls]) + bias_ref[pi] + before_start)
    probs, sums = [], []
    for s in scores:
        e = jnp.exp2(s - jnp.max(s, axis=-1, keepdims=True))
        sums.append(jnp.sum(e, axis=-1, keepdims=True))
        probs.append(e.astype(BF16))
    for pi in range(n_pairs):
        ls = slice(pi * PAIR, (pi + 1) * PAIR)
        o = _dot(probs[pi], vpad_ref[pl.ds(start, win_rows), ls]) / sums[pi]
        yb_ref[:, ls] = jnp.where(first, o[:rows], o[rows:]).astype(yb_ref.dtype)

    x = upad_ref[pl.ds(start, rows + CHUNK), :]
    cw = x.shape[1]
    acc, span, win_sums = x, 1, []
    for win in POOL_WINDOWS:
        while span < win:
            acc = acc + pltpu.roll(acc, span, axis=0)
            span *= 2
        win_sums.append(acc)
    lane = lax.broadcasted_iota(jnp.int32, (rows, cw), 1)
    t1 = (lax.broadcasted_iota(jnp.int32, (rows, cw), 0) + start + 1).astype(F32)
    pooled = jnp.zeros((rows, cw), F32)
    for gi, win in enumerate(POOL_WINDOWS):
        grp = (lane >= gi * HEAD) & (lane < (gi + 1) * HEAD)
        pooled = jnp.where(grp, win_sums[gi][CHUNK:] / jnp.minimum(t1, float(win)), pooled)
    pooled = pooled - x[CHUNK:]
    yc_ref[...] = (_dot(pooled.astype(BF16), pw_ref[...]) * ps_ref[...]).astype(yc_ref.dtype)


def _attn_pool(q, k, v, bias, u, pw, ps, l, *, qb):
    b, t, width = q.shape
    cw = u.shape[-1]
    rows = qb * CHUNK
    lay = lambda a: _layer_spec(a, l)
    seq = lambda a: pl.BlockSpec((None,) + a.shape[1:], lambda i, j: (i, 0, 0))
    tile = lambda w_: pl.BlockSpec((None, rows, w_), lambda i, j: (i, j, 0))
    return pl.pallas_call(
        functools.partial(_attn_pool_kernel, width=width, qb=qb),
        grid=(b, t // rows),
        in_specs=[tile(width), seq(k), seq(v), lay(bias), seq(u), lay(pw), lay(ps)],
        out_specs=[tile(width), tile(cw)],
        out_shape=[jax.ShapeDtypeStruct((b, t, width), BF16), jax.ShapeDtypeStruct((b, t, cw), BF16)],
        scratch_shapes=[pltpu.VMEM((t + PREV_CHUNKS * CHUNK, width), BF16),
                        pltpu.VMEM((t + PREV_CHUNKS * CHUNK, width), BF16),
                        pltpu.VMEM((t + CHUNK, cw), F32)],
        compiler_params=_params("parallel", "arbitrary"),
        name="attn_pool",
    )(q, k, v, bias, u, pw, ps)


def _out_cross_kernel(x_ref, ya_ref, yb_ref, yc_ref, woa_ref, wob_ref, woc_ref, g_ref, wq_ref, qg_ref,
                      k_ref, v_ref, wo_ref, o_ref, att_ref, *, heads):
    w16 = lambda ref: ref[...].astype(BF16)
    x = (x_ref[...] + _dot(ya_ref[...], w16(woa_ref)) + _dot(yb_ref[...], w16(wob_ref))
         + _dot(yc_ref[...], w16(woc_ref)))
    h = _rms(x, g_ref[...]).astype(BF16)
    q = _dot(h, w16(wq_ref))
    hd = q.shape[1] // heads
    cols = [slice(hi * hd, (hi + 1) * hd) for hi in range(heads)]
    qh = [(_rms(q[:, cs], qg_ref[...]) * (hd ** -0.5 * LOG2_E)).astype(BF16) for cs in cols]
    s = [_dot_nt(qi, k_ref[:, cs]) for qi, cs in zip(qh, cols)]
    e = [jnp.exp2(si - jnp.max(si, axis=-1, keepdims=True)) for si in s]
    for ei, cs in zip(e, cols):
        oh = _dot(ei.astype(BF16), v_ref[:, cs]) / jnp.sum(ei, axis=-1, keepdims=True)
        att_ref[:, cs] = oh.astype(BF16)
    o_ref[...] = x + _dot(att_ref[...], w16(wo_ref))


def _out_cross(x, ya, yb, yc, w_out, g, wq, qg, k, v, wo, l, *, tm, heads):
    b, t, d = x.shape
    lay = lambda a: _layer_spec(a, l)
    tile = lambda a: pl.BlockSpec((None, tm, a.shape[-1]), lambda i, j: (i, j, 0))
    mem = lambda a: pl.BlockSpec((None, None) + a.shape[2:], lambda i, j: (l, i, 0, 0))
    wa, wb, wc = ya.shape[-1], yb.shape[-1], yc.shape[-1]
    assert wa == wb and (wa + wb) % wc == 0, "row blocks of w_out must be block-aligned"
    w_rows = lambda rows, blk: pl.BlockSpec((None, rows, d), lambda i, j: (l, blk, 0))
    return pl.pallas_call(
        functools.partial(_out_cross_kernel, heads=heads),
        grid=(b, t // tm),
        in_specs=[tile(x), tile(ya), tile(yb), tile(yc), w_rows(wa, 0), w_rows(wb, 1), w_rows(wc, (wa + wb) // wc),
                  lay(g), lay(wq), lay(qg), mem(k), mem(v), lay(wo)],
        out_specs=tile(x),
        out_shape=jax.ShapeDtypeStruct((b, t, d), F32),
        scratch_shapes=[pltpu.VMEM((tm, d), BF16)],
        compiler_params=_params("parallel", "parallel"),
        name="out_cross",
    )(x, ya, yb, yc, w_out, w_out, w_out, g, wq, qg, k, v, wo)


def _mem_kv_kernel(mem_ref, g_ref, w_ref, kg_ref, k_ref, v_ref, *, heads):
    h = _rms(mem_ref[...], g_ref[...]).astype(BF16)
    kv = _dot(h, w_ref[...].astype(BF16))
    d = kv.shape[1] // 2
    hd = d // heads
    for hi in range(heads):
        cs = slice(hi * hd, (hi + 1) * hd)
        k_ref[:, cs] = _rms(kv[:, cs], kg_ref[...]).astype(BF16)
    v_ref[...] = kv[:, d:].astype(BF16)


def _mem_kv(mem, g, wkv, kg, *, heads):
    b, m, d = mem.shape
    depth = wkv.shape[0]
    per_layer = lambda a: pl.BlockSpec((None,) + a.shape[1:], lambda l, i: (l,) + (0,) * (a.ndim - 1))
    out = pl.BlockSpec((None, None, m, d), lambda l, i: (l, i, 0, 0))
    return pl.pallas_call(
        functools.partial(_mem_kv_kernel, heads=heads),
        grid=(depth, b),
        in_specs=[pl.BlockSpec((None, m, d), lambda l, i: (i, 0, 0)), per_layer(g), per_layer(wkv), per_layer(kg)],
        out_specs=[out, out],
        out_shape=[jax.ShapeDtypeStruct((depth, b, m, d), BF16)] * 2,
        compiler_params=_params("parallel", "parallel"),
        name="mem_kv",
    )(mem, g, wkv, kg)


def _split_weight(w):
    hi = w.astype(BF16)
    return hi, (w - hi.astype(F32)).astype(BF16)


def _rel_bias_band(rel_bias):
    heads, n_rel = rel_bias.shape
    period = BAND + CHUNK
    ext = jnp.concatenate([rel_bias, jnp.broadcast_to(rel_bias[:, -1:], (heads, BAND - n_rel))], axis=1)
    v = jnp.concatenate([ext[:, ::-1], jnp.broadcast_to(rel_bias[:, -1:], (heads, CHUNK))], axis=1)
    flat = jnp.tile(v, (1, CHUNK))[:, :CHUNK * (period - 1)]
    return flat.reshape(heads, CHUNK, period - 1)[:, :, :BAND]


def _rel_bias_window(rel_bias, qb):
    band = _rel_bias_band(rel_bias)
    heads = band.shape[0]
    per_chunk = [jnp.pad(band, ((0, 0), (0, 0), (qi * CHUNK, (qb - 1 - qi) * CHUNK)), constant_values=NEG_INF)
                 for qi in range(qb)]
    win = jnp.stack(per_chunk, axis=1)
    return win.reshape(heads // 2, 2 * qb * CHUNK, (PREV_CHUNKS + qb) * CHUNK)


def _lora_weight(w_up, a_up, g_up):
    depth, _, width = w_up.shape
    z = lambda r: jnp.zeros((depth, r, width), F32)
    return jnp.concatenate([
        jnp.concatenate([w_up, z(LORA_W), z(LORA_W)], axis=2),
        jnp.concatenate([z(LORA_A), a_up, z(LORA_A)], axis=2),
        jnp.concatenate([z(LORA_G), z(LORA_G), g_up], axis=2)], axis=1)


def _pool_weight(pool_w):
    depth, groups, cg, _ = pool_w.shape
    rows = []
    for gi in range(groups):
        blocks = [pool_w[:, gi] if gj == gi else jnp.zeros((depth, cg, cg), pool_w.dtype) for gj in range(groups)]
        rows.append(jnp.concatenate(blocks, axis=2))
    return jnp.concatenate(rows, axis=1)


def _stack_rows(a):
    return a.reshape(a.shape[0], 1, -1)


def _attn_pool_params(rel_bias, pool_w, pool_scale, qb):
    return {"bias": jax.vmap(functools.partial(_rel_bias_window, qb=qb))(rel_bias * LOG2_E),
            "pw": _pool_weight(pool_w).astype(BF16), "ps": _stack_rows(pool_scale)}


def _rwkv_params(a_mu, a_w0, a_w_up, a_a0, a_a_up, a_g_up, a_k_k, a_k_a, a_r_k, a_gn_g, a_gn_b,
                 a_v0, a_v_down, a_v_up):
    wl_hi, wl_lo = _split_weight(_lora_weight(a_w_up, a_a_up, a_g_up))
    pad_lanes = LANES - LORA_V
    vd_hi, vd_lo = _split_weight(jnp.pad(a_v_down, ((0, 0), (0, 0), (0, pad_lanes))))
    vu_hi, vu_lo = _split_weight(jnp.pad(a_v_up, ((0, 0), (0, pad_lanes), (0, 0))))
    return {"mu": _stack_rows(a_mu), "wl_hi": wl_hi, "wl_lo": wl_lo, "w0": _stack_rows(a_w0),
            "a0": _stack_rows(a_a0), "k_k": _stack_rows(a_k_k), "k_a": _stack_rows(a_k_a),
            "r_k": _stack_rows(a_r_k), "gn_g": _stack_rows(a_gn_g), "gn_b": _stack_rows(a_gn_b),
            "v0": _stack_rows(a_v0),
            "vd_hi": vd_hi, "vd_lo": vd_lo, "vu_hi": vu_hi, "vu_lo": vu_lo}


def kernel(x, mem, norm_ffn1, ffn1_wi, ffn1_wo, norm_mix, w_in, w_out, a_mu, a_w0, a_w_up, a_a0, a_a_up, a_g_up,
           a_k_k, a_k_a, a_r_k, a_gn_g, a_gn_b, a_v0, a_v_down, a_v_up, b_q_gain, b_k_gain, b_rel_bias,
           c_pool_w, c_pool_scale, norm_cross, norm_mem, x_wq, x_wkv, x_wo, x_q_gain, x_k_gain,
           norm_ffn2, ffn2_wi, ffn2_wo):
    b, t, d = x.shape
    depth = w_in.shape[0]
    a_proj = a_mu.shape[-1]
    b_width = b_rel_bias.shape[1] * HEAD
    c_width = c_pool_scale.shape[-1]
    x_heads = d // x_q_gain.shape[-1]
    n_tok = b * t

    ffn1 = (_stack_rows(norm_ffn1), ffn1_wi, ffn1_wo)
    ffn2 = (_stack_rows(norm_ffn2), ffn2_wi, ffn2_wo)
    q_gain = _stack_rows(jnp.tile(b_q_gain, (1, b_width // HEAD))) * (HEAD ** -0.5 * LOG2_E)
    k_gain = _stack_rows(jnp.tile(b_k_gain, (1, b_width // HEAD)))
    rwkv_prm = _rwkv_params(a_mu, a_w0, a_w_up, a_a0, a_a_up, a_g_up, a_k_k, a_k_a, a_r_k, a_gn_g, a_gn_b,
                            a_v0, a_v_down, a_v_up)
    ap_prm = _attn_pool_params(b_rel_bias, c_pool_w, c_pool_scale, ATTN_QB)
    mem_k, mem_v = _mem_kv(mem, _stack_rows(norm_mem), x_wkv, _stack_rows(x_k_gain), heads=x_heads)

    seq = lambda a: a.reshape(b, t, a.shape[-1])
    x = x.reshape(n_tok, d)
    v_first = None
    for l in range(depth):
        x = _ffn(x, *ffn1, l, tm=FFN_TM, tf=FFN_TF)
        pa, q, k, v, u = _inproj(x, _stack_rows(norm_mix), w_in, q_gain, k_gain, l,
                                 tm=ROW_TM, a_proj=a_proj, b_width=b_width, c_width=c_width)
        if l == 0:
            y_a, v_first = _rwkv(seq(pa), None, rwkv_prm, l, tc=RWKV_TC)
        else:
            y_a = _rwkv(seq(pa), v_first, rwkv_prm, l, tc=RWKV_TC)
        y_b, y_c = _attn_pool(seq(q), seq(k), seq(v), ap_prm["bias"], seq(u), ap_prm["pw"], ap_prm["ps"], l,
                              qb=ATTN_QB)
        x = _out_cross(seq(x), y_a, y_b, y_c, w_out, _stack_rows(norm_cross), x_wq, _stack_rows(x_q_gain),
                       mem_k, mem_v, x_wo, l, tm=ROW_TM, heads=x_heads)
        x = _ffn(x.reshape(n_tok, d), *ffn2, l, tm=FFN_TM, tf=FFN_TF)
    return x.reshape(b, t, d)
```

```python
import functools

import numpy as np
import jax
import jax.numpy as jnp
from jax import lax
from jax.experimental import pallas as pl
from jax.experimental.pallas import tpu as pltpu

F32 = jnp.float32
BF16 = jnp.bfloat16

LANES = 128
HEAD = 64
PAIR = 2 * HEAD
CHUNK = 64
INV_BLOCK = 8
PREV_CHUNKS = 8
BAND = (PREV_CHUNKS + 1) * CHUNK
REL_MAX = 256
POOL_WINDOWS = (2, 4, 8, 16)
LORA_W, LORA_A, LORA_G, LORA_V = 32, 32, 64, 32
RMS_EPS = 1e-6
GN_EPS = 64e-5
NEG_INF = -1e30
LOG2_E = 1.4426950408889634
VMEM_LIMIT = 56 * 1024 * 1024
FFN_TM = 1024
FFN_TF = 256
ROW_TM = 1024
RWKV_TC = 512
ATTN_QB = 4


def _dot(a, b, precision=None):
    return jnp.dot(a, b, preferred_element_type=F32, precision=precision)


def _dot_nt(a, b):
    return lax.dot_general(a, b, (((1,), (1,)), ((), ())), preferred_element_type=F32)


def _dot_tn(a, b):
    return lax.dot_general(a, b, (((0,), (0,)), ((), ())), preferred_element_type=F32)


def _split(x, terms):
    parts = []
    for _ in range(terms):
        hi = x.astype(BF16)
        parts.append(hi)
        x = x - hi.astype(F32)
    return parts


def _dot_x3(x, w_hi, w_lo):
    x_hi, x_lo = _split(x, 2)
    return _dot(x_hi, w_hi) + (_dot(x_lo, w_hi) + _dot(x_hi, w_lo))


def _rms(x, g):
    return x * lax.rsqrt(jnp.mean(x * x, axis=-1, keepdims=True) + RMS_EPS) * g


def _sigmoid(x):
    return 1.0 / (1.0 + jnp.exp(-x))


def _params(*sem):
    return pltpu.CompilerParams(dimension_semantics=sem, vmem_limit_bytes=VMEM_LIMIT)


def _layer_spec(a, l):
    zeros = (0,) * (a.ndim - 1)
    return pl.BlockSpec((None,) + a.shape[1:], lambda *_: (l,) + zeros, pipeline_mode=pl.Buffered(1))


def _ffn_kernel(x_ref, g_ref, wi_ref, wo_ref, o_ref, *, tf):
    x = x_ref[...]
    h = _rms(x, g_ref[...]).astype(BF16)
    dff = wo_ref.shape[0]
    acc = None
    for c0 in range(0, dff, tf):
        gate = _dot(h, wi_ref[:, c0:c0 + tf].astype(BF16))
        up = _dot(h, wi_ref[:, dff + c0:dff + c0 + tf].astype(BF16))
        act = (gate * _sigmoid(gate) * up).astype(BF16)
        part = _dot(act, wo_ref[c0:c0 + tf, :].astype(BF16))
        acc = part if acc is None else acc + part
    o_ref[...] = x + 0.5 * acc


def _ffn(x, g, wi, wo, l, *, tm, tf):
    n, d = x.shape
    return pl.pallas_call(
        functools.partial(_ffn_kernel, tf=tf),
        grid=(n // tm,),
        in_specs=[pl.BlockSpec((tm, d), lambda i: (i, 0)), _layer_spec(g, l), _layer_spec(wi, l), _layer_spec(wo, l)],
        out_specs=pl.BlockSpec((tm, d), lambda i: (i, 0)),
        out_shape=jax.ShapeDtypeStruct((n, d), F32),
        compiler_params=_params("parallel"),
        name="ffn",
    )(x, g, wi, wo)


def _inproj_kernel(x_ref, g_ref, w_ref, qg_ref, kg_ref,
                   pa_ref, q_ref, k_ref, v_ref, u_ref, *, a_proj, b_width):
    h = _rms(x_ref[...], g_ref[...]).astype(BF16)
    p = _dot(h, w_ref[...].astype(BF16))
    pa_ref[...] = p[:, :a_proj]
    q = p[:, a_proj:a_proj + b_width]
    k = p[:, a_proj + b_width:a_proj + 2 * b_width]
    qms = _head_sum(q * q) * (1.0 / HEAD)
    kms = _head_sum(k * k) * (1.0 / HEAD)
    q_ref[...] = (q * lax.rsqrt(qms + RMS_EPS) * qg_ref[...]).astype(BF16)
    k_ref[...] = (k * lax.rsqrt(kms + RMS_EPS) * kg_ref[...]).astype(BF16)
    v_ref[...] = p[:, a_proj + 2 * b_width:a_proj + 3 * b_width].astype(BF16)
    u_ref[...] = p[:, a_proj + 3 * b_width:]


def _inproj(x, g, w, qg, kg, l, *, tm, a_proj, b_width, c_width):
    n, d = x.shape
    row = lambda w_: pl.BlockSpec((tm, w_), lambda i: (i, 0))
    lay = lambda a: _layer_spec(a, l)
    return pl.pallas_call(
        functools.partial(_inproj_kernel, a_proj=a_proj, b_width=b_width),
        grid=(n // tm,),
        in_specs=[row(d), lay(g), lay(w), lay(qg), lay(kg)],
        out_specs=[row(a_proj), row(b_width), row(b_width), row(b_width), row(c_width)],
        out_shape=[jax.ShapeDtypeStruct((n, a_proj), F32),
                   jax.ShapeDtypeStruct((n, b_width), BF16),
                   jax.ShapeDtypeStruct((n, b_width), BF16),
                   jax.ShapeDtypeStruct((n, b_width), BF16),
                   jax.ShapeDtypeStruct((n, c_width), F32)],
        compiler_params=_params("parallel"),
        name="in_proj",
    )(x, g, w, qg, kg)


def _lane_lt_head(shape):
    return lax.broadcasted_iota(jnp.int32, shape, len(shape) - 1) < HEAD


def _head_sum(x):
    out = []
    for ls in range(0, x.shape[1], PAIR):
        xs = x[:, ls:ls + PAIR]
        first = _lane_lt_head(xs.shape)
        sum_a = jnp.sum(jnp.where(first, xs, 0.0), axis=-1, keepdims=True)
        sum_b = jnp.sum(jnp.where(first, 0.0, xs), axis=-1, keepdims=True)
        out.append(jnp.where(first, sum_a, sum_b))
    return jnp.concatenate(out, axis=1)


def _blockdiag(x2):
    first = _lane_lt_head(x2.shape)
    zero = jnp.zeros_like(x2)
    return jnp.concatenate([jnp.where(first, x2, zero), jnp.where(first, zero, x2)], axis=0)


def _rwkv_chunk_maps(insts):
    c = CHUNK
    b16 = lambda t: t.astype(BF16)
    t_idx = lax.broadcasted_iota(jnp.int32, (c, PAIR), 0)
    s_idx = lax.broadcasted_iota(jnp.int32, (c, PAIR), 1) % HEAD
    strict = t_idx > s_idx
    incl = t_idx >= s_idx
    zero = jnp.zeros((c, PAIR), F32)
    eye = jnp.where(t_idx == s_idx, 1.0, 0.0)
    row = lax.broadcasted_iota(jnp.int32, (PAIR, PAIR), 0)
    col = lax.broadcasted_iota(jnp.int32, (PAIR, PAIR), 1)
    same_head = (row < HEAD) == (col < HEAD)
    zero2 = jnp.zeros((PAIR, PAIR), F32)

    s_all = [_dot_nt(b16(jnp.concatenate([qa, qr], axis=0)),
                     b16(jnp.concatenate([_blockdiag(kb), _blockdiag(kk)], axis=0)))
             for qa, qr, kb, kk, _, _, _, _ in insts]
    a_ab = [jnp.where(strict, s[:c, :PAIR], zero) for s in s_all]
    a_ak = [jnp.where(strict, s[:c, PAIR:], zero) for s in s_all]
    a_rb = [jnp.where(incl, s[c:, :PAIR], zero) for s in s_all]
    a_rk = [jnp.where(incl, s[c:, PAIR:], zero) for s in s_all]
    av = [_dot(b16(jnp.concatenate([ak, rk], axis=0)), b16(_blockdiag(inst[6])))
          for ak, rk, inst in zip(a_ak, a_rk, insts)]

    same_block = lambda blk: (t_idx // blk) == (s_idx // blk)
    a0 = [jnp.where(same_block(INV_BLOCK), a, zero) for a in a_ab]
    a2 = [_dot(b16(a), b16(_blockdiag(a))) for a in a0]
    p1 = [eye + a for a in a0]
    st = [_dot(b16(jnp.concatenate([sq, p], axis=0)), b16(_blockdiag(sq))) for sq, p in zip(a2, p1)]
    p2 = [p + s[c:] for p, s in zip(p1, st)]
    tinv = [p + _dot(b16(p), b16(_blockdiag(s[:c]))) for p, s in zip(p2, st)]
    blk = INV_BLOCK
    while blk < c:
        off_diag = same_block(2 * blk) & jnp.logical_not(same_block(blk))
        x1 = [_dot(b16(jnp.where(off_diag, a, zero)), b16(_blockdiag(t))) for a, t in zip(a_ab, tinv)]
        tinv = [t + _dot(b16(t), b16(_blockdiag(x))) for t, x in zip(tinv, x1)]
        blk *= 2

    r1 = [_dot(b16(t), b16(jnp.concatenate([_blockdiag(inst[0]), _blockdiag(a[:c])], axis=1)))
          for t, a, inst in zip(tinv, av, insts)]
    r2 = [_dot(b16(rb), b16(jnp.concatenate([_blockdiag(r[:, :PAIR]), _blockdiag(r[:, PAIR:])], axis=1)))
          for rb, r in zip(a_rb, r1)]
    out = []
    for r1_i, r2_i, av_i, (qa, qr, kb, kk, kbe, kke, v, wlast_row) in zip(r1, r2, av, insts):
        qa_p, u0 = r1_i[:, :PAIR], r1_i[:, PAIR:]
        m = jnp.where(same_head, _dot_tn(b16(qa_p), b16(kbe)), zero2)
        m = m + jnp.where(row == col, jnp.broadcast_to(wlast_row, (PAIR, PAIR)), zero2)
        n = jnp.where(same_head,
                      _dot_tn(b16(jnp.concatenate([u0, v], axis=0)),
                              b16(jnp.concatenate([kbe, kke], axis=0))), zero2)
        out.append((qr + r2_i[:, :PAIR], r2_i[:, PAIR:] + av_i[c:], m, n))
    return out


def _rwkv_kernel(*refs, has_vres, tc, width):
    if has_vres:
        (pa_ref, mu_ref, wlh_ref, wll_ref, w0_ref, a0_ref, kk_ref, ka_ref, rk_ref, gng_ref, gnb_ref,
         vf_ref, v0_ref, vdh_ref, vdl_ref, vuh_ref, vul_ref, y_ref, prev_ref, s_ref, yraw_ref) = refs
    else:
        (pa_ref, mu_ref, wlh_ref, wll_ref, w0_ref, a0_ref, kk_ref, ka_ref, rk_ref, gng_ref, gnb_ref,
         y_ref, vf_ref, prev_ref, s_ref, yraw_ref) = refs
    n_pairs = width // PAIR
    n_chunks = tc // CHUNK

    @pl.when(pl.program_id(1) == 0)
    def _():
        prev_ref[...] = jnp.zeros_like(prev_ref)
        s_ref[...] = jnp.zeros_like(s_ref)

    p = pa_ref[...]
    row = lax.broadcasted_iota(jnp.int32, p.shape, 0)
    shifted = jnp.where(row == 0, jnp.broadcast_to(prev_ref[0:1, :], p.shape), pltpu.roll(p, 1, axis=0))
    prev_ref[0:1, :] = p[tc - 1:tc, :]
    p = p + mu_ref[...] * (shifted - p)

    r = p[:, :width]
    k = p[:, width:2 * width]
    v = p[:, 2 * width:3 * width]
    lo_in = p[:, 3 * width:]
    lane = lax.broadcasted_iota(jnp.int32, lo_in.shape, 1)
    lo_act = jnp.where(lane < LORA_W, jnp.tanh(lo_in),
                       jnp.where(lane < LORA_W + LORA_A, lo_in, _sigmoid(lo_in)))
    lo = _dot_x3(lo_act, wlh_ref[...], wll_ref[...])
    wz = w0_ref[...] + lo[:, :width]
    w = -(jnp.maximum(-wz, 0.0) + jnp.log(1.0 + jnp.exp(-jnp.abs(wz)))) - 0.5
    lw = -jnp.exp(w)
    a = _sigmoid(a0_ref[...] + lo[:, width:2 * width])
    g = lo[:, 2 * width:]

    if has_vres:
        v_lo = _dot_x3(v, vdh_ref[...], vdl_ref[...])
        gate = _sigmoid(v0_ref[...] + _dot_x3(v_lo, vuh_ref[...], vul_ref[...]))
        v = v + (vf_ref[...] - v) * gate
    else:
        vf_ref[...] = v

    kkx = k * kk_ref[...]
    kk = kkx / jnp.maximum(jnp.sqrt(_head_sum(kkx * kkx)), 1e-12)
    kmod = k * (1.0 + (a - 1.0) * ka_ref[...])

    grp = 2 * CHUNK
    ti = lax.broadcasted_iota(jnp.int32, (grp, grp), 0)
    tj = lax.broadcasted_iota(jnp.int32, (grp, grp), 1)
    tri = jnp.where(((ti // CHUNK) == (tj // CHUNK)) & (tj <= ti), 1.0, 0.0).astype(BF16)
    lw_hi, lw_lo = _split(lw, 2)
    cum = jnp.concatenate([_dot(tri, lw_hi[r0:r0 + grp]) + _dot(tri, lw_lo[r0:r0 + grp])
                           for r0 in range(0, tc, grp)], axis=0)
    tot = jnp.concatenate(
        [jnp.broadcast_to(cum[(ci + 1) * CHUNK - 1:(ci + 1) * CHUNK, :], (CHUNK, width)) for ci in range(n_chunks)],
        axis=0)
    w_inc = jnp.exp(cum)
    w_exc = jnp.exp(cum - lw)
    w_inv = jnp.exp(-cum)
    w_end = jnp.exp(tot - cum)
    w_tot = jnp.exp(tot)

    kka = kk * a
    qa_all = -kk * w_exc
    qr_all = r * w_inc
    kb_all = kka * w_inv
    kk_all = kmod * w_inv
    kbe_all = kka * w_end
    kke_all = kmod * w_end

    insts = []
    for ci in range(n_chunks):
        rs = slice(ci * CHUNK, (ci + 1) * CHUNK)
        for pi in range(n_pairs):
            ls = slice(pi * PAIR, (pi + 1) * PAIR)
            insts.append((qa_all[rs, ls], qr_all[rs, ls], kb_all[rs, ls], kk_all[rs, ls],
                          kbe_all[rs, ls], kke_all[rs, ls], v[rs, ls], w_tot[ci * CHUNK:ci * CHUNK + 1, ls]))
    maps = _rwkv_chunk_maps(insts)
    states = [s_ref[pi] for pi in range(n_pairs)]
    for ci in range(n_chunks):
        for pi in range(n_pairs):
            qr_p, y0, m, n = maps[ci * n_pairs + pi]
            s16 = states[pi].astype(BF16)
            yraw_ref[ci * CHUNK:(ci + 1) * CHUNK, pi * PAIR:(pi + 1) * PAIR] = _dot_nt(qr_p.astype(BF16), s16) + y0
            states[pi] = _dot(s16, m.astype(BF16)) + n
    for pi in range(n_pairs):
        s_ref[pi] = states[pi]

    y = yraw_ref[...]
    mean = _head_sum(y) * (1.0 / HEAD)
    yc = y - mean
    var = _head_sum(yc * yc) * (1.0 / HEAD)
    yn = yc * lax.rsqrt(var + GN_EPS) * gng_ref[...] + gnb_ref[...]
    bonus = _head_sum(r * kmod * rk_ref[...]) * v
    y_ref[...] = ((yn + bonus) * g).astype(y_ref.dtype)


def _rwkv(pa, vfirst, prm, l, *, tc):
    b, t, a_proj = pa.shape
    width = prm["w0"].shape[-1]
    has_vres = vfirst is not None
    tile = lambda w_: pl.BlockSpec((None, tc, w_), lambda i, j: (i, j, 0))
    names = ["mu", "wl_hi", "wl_lo", "w0", "a0", "k_k", "k_a", "r_k", "gn_g", "gn_b"]
    args = [pa] + [prm[nm] for nm in names]
    in_specs = [tile(a_proj)] + [_layer_spec(prm[nm], l) for nm in names]
    y_shape = jax.ShapeDtypeStruct((b, t, width), BF16)
    if has_vres:
        extra = [prm["v0"], prm["vd_hi"], prm["vd_lo"], prm["vu_hi"], prm["vu_lo"]]
        args += [vfirst] + extra
        in_specs += [tile(width)] + [_layer_spec(a, l - 1) for a in extra]
        out_specs, out_shape = tile(width), y_shape
    else:
        out_specs = [tile(width), tile(width)]
        out_shape = [y_shape, jax.ShapeDtypeStruct((b, t, width), F32)]
    return pl.pallas_call(
        functools.partial(_rwkv_kernel, has_vres=has_vres, tc=tc, width=width),
        grid=(b, t // tc),
        in_specs=in_specs,
        out_specs=out_specs,
        out_shape=out_shape,
        scratch_shapes=[pltpu.VMEM((8, a_proj), F32),
                        pltpu.VMEM((width // PAIR, PAIR, PAIR), F32),
                        pltpu.VMEM((tc, width), F32)],
        compiler_params=_params("parallel", "arbitrary"),
        name="rwkv",
    )(*args)


def _attn_pool_kernel(q_ref, k_ref, v_ref, bias_ref, u_ref, pw_ref, ps_ref, yb_ref, yc_ref,
                      kpad_ref, vpad_ref, upad_ref, *, width, qb):
    step = pl.program_id(1)
    rows = qb * CHUNK
    win_rows = (PREV_CHUNKS + qb) * CHUNK

    @pl.when(step == 0)
    def _():
        kpad_ref[:PREV_CHUNKS * CHUNK, :] = jnp.zeros((PREV_CHUNKS * CHUNK, width), BF16)
        vpad_ref[:PREV_CHUNKS * CHUNK, :] = jnp.zeros((PREV_CHUNKS * CHUNK, width), BF16)
        kpad_ref[PREV_CHUNKS * CHUNK:, :] = k_ref[...]
        vpad_ref[PREV_CHUNKS * CHUNK:, :] = v_ref[...]
        upad_ref[:CHUNK, :] = jnp.zeros((CHUNK, upad_ref.shape[1]), F32)
        upad_ref[CHUNK:, :] = u_ref[...]

    start = pl.multiple_of(step * rows, rows)
    q = q_ref[...]
    slot = lax.broadcasted_iota(jnp.int32, (1, win_rows), 1)
    before_start = jnp.where(slot >= PREV_CHUNKS * CHUNK - start, 0.0, NEG_INF)
    first = _lane_lt_head((rows, PAIR))
    n_pairs = width // PAIR
    scores = []
    for pi in range(n_pairs):
        ls = slice(pi * PAIR, (pi + 1) * PAIR)
        q2 = q[:, ls]
        zq = jnp.zeros_like(q2)
        qs = jnp.concatenate([jnp.where(first, q2, zq), jnp.where(first, zq, q2)], axis=0)
        scores.append(_dot_nt(qs, kpad_ref[pl.ds(start, win_rows), ls]) + bias_ref[pi] + before_start)
    probs, sums = [], []
    for s in scores:
        e = jnp.exp2(s - jnp.max(s, axis=-1, keepdims=True))
        sums.append(jnp.sum(e, axis=-1, keepdims=True))
        probs.append(e.astype(BF16))
    for pi in range(n_pairs):
        ls = slice(pi * PAIR, (pi + 1) * PAIR)
        o = _dot(probs[pi], vpad_ref[pl.ds(start, win_rows), ls]) / sums[pi]
        yb_ref[:, ls] = jnp.where(first, o[:rows], o[rows:]).astype(yb_ref.dtype)

    x = upad_ref[pl.ds(start, rows + CHUNK), :]
    cw = x.shape[1]
    acc, span, win_sums = x, 1, []
    for win in POOL_WINDOWS:
        while span < win:
            acc = acc + pltpu.roll(acc, span, axis=0)
            span *= 2
        win_sums.append(acc)
    lane = lax.broadcasted_iota(jnp.int32, (rows, cw), 1)
    t1 = (lax.broadcasted_iota(jnp.int32, (rows, cw), 0) + start + 1).astype(F32)
    pooled = jnp.zeros((rows, cw), F32)
    for gi, win in enumerate(POOL_WINDOWS):
        grp = (lane >= gi * HEAD) & (lane < (gi + 1) * HEAD)
        pooled = jnp.where(grp, win_sums[gi][CHUNK:] / jnp.minimum(t1, float(win)), pooled)
    pooled = pooled - x[CHUNK:]
    yc_ref[...] = (_dot(pooled.astype(BF16), pw_ref[...]) * ps_ref[...]).astype(yc_ref.dtype)


def _attn_pool(q, k, v, bias, u, pw, ps, l, *, qb):
    b, t, width = q.shape
    cw = u.shape[-1]
    rows = qb * CHUNK
    lay = lambda a: _layer_spec(a, l)
    seq = lambda a: pl.BlockSpec((None,) + a.shape[1:], lambda i, j: (i, 0, 0))
    tile = lambda w_: pl.BlockSpec((None, rows, w_), lambda i, j: (i, j, 0))
    return pl.pallas_call(
        functools.partial(_attn_pool_kernel, width=width, qb=qb),
        grid=(b, t // rows),
        in_specs=[tile(width), seq(k), seq(v), lay(bias), seq(u), lay(pw), lay(ps)],
        out_specs=[tile(width), tile(cw)],
        out_shape=[jax.ShapeDtypeStruct((b, t, width), BF16), jax.ShapeDtypeStruct((b, t, cw), BF16)],
        scratch_shapes=[pltpu.VMEM((t + PREV_CHUNKS * CHUNK, width), BF16),
                        pltpu.VMEM((t + PREV_CHUNKS * CHUNK, width), BF16),
                        pltpu.VMEM((t + CHUNK, cw), F32)],
        compiler_params=_params("parallel", "arbitrary"),
        name="attn_pool",
    )(q, k, v, bias, u, pw, ps)


def _out_cross_kernel(x_ref, ya_ref, yb_ref, yc_ref, wout_ref, g_ref, wq_ref, qg_ref,
                      k_ref, v_ref, wo_ref, o_ref, att_ref, *, heads):
    w16 = lambda ref: ref[...].astype(BF16)
    mix = jnp.concatenate([ya_ref[...], yb_ref[...], yc_ref[...]], axis=1)
    x = x_ref[...] + _dot(mix, w16(wout_ref))
    h = _rms(x, g_ref[...]).astype(BF16)
    q = _dot(h, w16(wq_ref))
    hd = q.shape[1] // heads
    cols = [slice(hi * hd, (hi + 1) * hd) for hi in range(heads)]
    qh = [(_rms(q[:, cs], qg_ref[...]) * (hd ** -0.5 * LOG2_E)).astype(BF16) for cs in cols]
    s = [_dot_nt(qi, k_ref[:, cs]) for qi, cs in zip(qh, cols)]
    e = [jnp.exp2(si - jnp.max(si, axis=-1, keepdims=True)) for si in s]
    for ei, cs in zip(e, cols):
        oh = _dot(ei.astype(BF16), v_ref[:, cs]) / jnp.sum(ei, axis=-1, keepdims=True)
        att_ref[:, cs] = oh.astype(BF16)
    o_ref[...] = x + _dot(att_ref[...], w16(wo_ref))


def _out_cross(x, ya, yb, yc, w_out, g, wq, qg, k, v, wo, l, *, tm, heads):
    b, t, d = x.shape
    lay = lambda a: _layer_spec(a, l)
    tile = lambda a: pl.BlockSpec((None, tm, a.shape[-1]), lambda i, j: (i, j, 0))
    mem = lambda a: pl.BlockSpec((None, None) + a.shape[2:], lambda i, j: (l, i, 0, 0))
    return pl.pallas_call(
        functools.partial(_out_cross_kernel, heads=heads),
        grid=(b, t // tm),
        in_specs=[tile(x), tile(ya), tile(yb), tile(yc), lay(w_out),
                  lay(g), lay(wq), lay(qg), mem(k), mem(v), lay(wo)],
        out_specs=tile(x),
        out_shape=jax.ShapeDtypeStruct((b, t, d), F32),
        scratch_shapes=[pltpu.VMEM((tm, d), BF16)],
        compiler_params=_params("parallel", "parallel"),
        name="out_cross",
    )(x, ya, yb, yc, w_out, g, wq, qg, k, v, wo)


def _mem_kv_kernel(mem_ref, g_ref, w_ref, kg_ref, k_ref, v_ref, *, heads):
    h = _rms(mem_ref[...], g_ref[...]).astype(BF16)
    kv = _dot(h, w_ref[...].astype(BF16))
    d = kv.shape[1] // 2
    hd = d // heads
    for hi in range(heads):
        cs = slice(hi * hd, (hi + 1) * hd)
        k_ref[:, cs] = _rms(kv[:, cs], kg_ref[...]).astype(BF16)
    v_ref[...] = kv[:, d:].astype(BF16)


def _mem_kv(mem, g, wkv, kg, *, heads):
    b, m, d = mem.shape
    depth = wkv.shape[0]
    per_layer = lambda a: pl.BlockSpec((None,) + a.shape[1:], lambda l, i: (l,) + (0,) * (a.ndim - 1))
    out = pl.BlockSpec((None, None, m, d), lambda l, i: (l, i, 0, 0))
    return pl.pallas_call(
        functools.partial(_mem_kv_kernel, heads=heads),
        grid=(depth, b),
        in_specs=[pl.BlockSpec((None, m, d), lambda l, i: (i, 0, 0)), per_layer(g), per_layer(wkv), per_layer(kg)],
        out_specs=[out, out],
        out_shape=[jax.ShapeDtypeStruct((depth, b, m, d), BF16)] * 2,
        compiler_params=_params("parallel", "parallel"),
        name="mem_kv",
    )(mem, g, wkv, kg)


def _split_weight(w):
    hi = w.astype(BF16)
    return hi, (w - hi.astype(F32)).astype(BF16)


def _rel_bias_band(rel_bias):
    heads, n_rel = rel_bias.shape
    period = BAND + CHUNK
    ext = jnp.concatenate([rel_bias, jnp.broadcast_to(rel_bias[:, -1:], (heads, BAND - n_rel))], axis=1)
    v = jnp.concatenate([ext[:, ::-1], jnp.broadcast_to(rel_bias[:, -1:], (heads, CHUNK))], axis=1)
    flat = jnp.tile(v, (1, CHUNK))[:, :CHUNK * (period - 1)]
    return flat.reshape(heads, CHUNK, period - 1)[:, :, :BAND]


def _rel_bias_window(rel_bias, qb):
    band = _rel_bias_band(rel_bias)
    heads = band.shape[0]
    per_chunk = [jnp.pad(band, ((0, 0), (0, 0), (qi * CHUNK, (qb - 1 - qi) * CHUNK)), constant_values=NEG_INF)
                 for qi in range(qb)]
    win = jnp.stack(per_chunk, axis=1)
    return win.reshape(heads // 2, 2 * qb * CHUNK, (PREV_CHUNKS + qb) * CHUNK)


def _lora_weight(w_up, a_up, g_up):
    depth, _, width = w_up.shape
    z = lambda r: jnp.zeros((depth, r, width), F32)
    return jnp.concatenate([
        jnp.concatenate([w_up, z(LORA_W), z(LORA_W)], axis=2),
        jnp.concatenate([z(LORA_A), a_up, z(LORA_A)], axis=2),
        jnp.concatenate([z(LORA_G), z(LORA_G), g_up], axis=2)], axis=1)


def _pool_weight(pool_w):
    depth, groups, cg, _ = pool_w.shape
    rows = []
    for gi in range(groups):
        blocks = [pool_w[:, gi] if gj == gi else jnp.zeros((depth, cg, cg), pool_w.dtype) for gj in range(groups)]
        rows.append(jnp.concatenate(blocks, axis=2))
    return jnp.concatenate(rows, axis=1)


def _stack_rows(a):
    return a.reshape(a.shape[0], 1, -1)


def _attn_pool_params(rel_bias, pool_w, pool_scale, qb):
    return {"bias": jax.vmap(functools.partial(_rel_bias_window, qb=qb))(rel_bias * LOG2_E),
            "pw": _pool_weight(pool_w).astype(BF16), "ps": _stack_rows(pool_scale)}


def _rwkv_params(a_mu, a_w0, a_w_up, a_a0, a_a_up, a_g_up, a_k_k, a_k_a, a_r_k, a_gn_g, a_gn_b,
                 a_v0, a_v_down, a_v_up):
    wl_hi, wl_lo = _split_weight(_lora_weight(a_w_up, a_a_up, a_g_up))
    pad_lanes = LANES - LORA_V
    vd_hi, vd_lo = _split_weight(jnp.pad(a_v_down, ((0, 0), (0, 0), (0, pad_lanes))))
    vu_hi, vu_lo = _split_weight(jnp.pad(a_v_up, ((0, 0), (0, pad_lanes), (0, 0))))
    return {"mu": _stack_rows(a_mu), "wl_hi": wl_hi, "wl_lo": wl_lo, "w0": _stack_rows(a_w0),
            "a0": _stack_rows(a_a0), "k_k": _stack_rows(a_k_k), "k_a": _stack_rows(a_k_a),
            "r_k": _stack_rows(a_r_k), "gn_g": _stack_rows(a_gn_g), "gn_b": _stack_rows(a_gn_b),
            "v0": _stack_rows(a_v0),
            "vd_hi": vd_hi, "vd_lo": vd_lo, "vu_hi": vu_hi, "vu_lo": vu_lo}


def kernel(x, mem, norm_ffn1, ffn1_wi, ffn1_wo, norm_mix, w_in, w_out, a_mu, a_w0, a_w_up, a_a0, a_a_up, a_g_up,
           a_k_k, a_k_a, a_r_k, a_gn_g, a_gn_b, a_v0, a_v_down, a_v_up, b_q_gain, b_k_gain, b_rel_bias,
           c_pool_w, c_pool_scale, norm_cross, norm_mem, x_wq, x_wkv, x_wo, x_q_gain, x_k_gain,
           norm_ffn2, ffn2_wi, ffn2_wo):
    b, t, d = x.shape
    depth = w_in.shape[0]
    a_proj = a_mu.shape[-1]
    b_width = b_rel_bias.shape[1] * HEAD
    c_width = c_pool_scale.shape[-1]
    x_heads = d // x_q_gain.shape[-1]
    n_tok = b * t

    ffn1 = (_stack_rows(norm_ffn1), ffn1_wi, ffn1_wo)
    ffn2 = (_stack_rows(norm_ffn2), ffn2_wi, ffn2_wo)
    q_gain = _stack_rows(jnp.tile(b_q_gain, (1, b_width // HEAD))) * (HEAD ** -0.5 * LOG2_E)
    k_gain = _stack_rows(jnp.tile(b_k_gain, (1, b_width // HEAD)))
    rwkv_prm = _rwkv_params(a_mu, a_w0, a_w_up, a_a0, a_a_up, a_g_up, a_k_k, a_k_a, a_r_k, a_gn_g, a_gn_b,
                            a_v0, a_v_down, a_v_up)
    ap_prm = _attn_pool_params(b_rel_bias, c_pool_w, c_pool_scale, ATTN_QB)
    mem_k, mem_v = _mem_kv(mem, _stack_rows(norm_mem), x_wkv, _stack_rows(x_k_gain), heads=x_heads)

    seq = lambda a: a.reshape(b, t, a.shape[-1])
    x = x.reshape(n_tok, d)
    v_first = None
    for l in range(depth):
        x = _ffn(x, *ffn1, l, tm=FFN_TM, tf=FFN_TF)
        pa, q, k, v, u = _inproj(x, _stack_rows(norm_mix), w_in, q_gain, k_gain, l,
                                 tm=ROW_TM, a_proj=a_proj, b_width=b_width, c_width=c_width)
        if l == 0:
            y_a, v_first = _rwkv(seq(pa), None, rwkv_prm, l, tc=RWKV_TC)
        else:
            y_a = _rwkv(seq(pa), v_first, rwkv_prm, l, tc=RWKV_TC)
        y_b, y_c = _attn_pool(seq(q), seq(k), seq(v), ap_prm["bias"], seq(u), ap_prm["pw"], ap_prm["ps"], l,
                              qb=ATTN_QB)
        x = _out_cross(seq(x), y_a, y_b, y_c, w_out, _stack_rows(norm_cross), x_wq, _stack_rows(x_q_gain),
                       mem_k, mem_v, x_wo, l, tm=ROW_TM, heads=x_heads)
        x = _ffn(x.reshape(n_tok, d), *ffn2, l, tm=FFN_TM, tf=FFN_TF)
    return x.reshape(b, t, d)
```

```python
import functools

import numpy as np
import jax
import jax.numpy as jnp
from jax import lax
from jax.experimental import pallas as pl
from jax.experimental.pallas import tpu as pltpu

F32 = jnp.float32
BF16 = jnp.bfloat16

LANES = 128
HEAD = 64
PAIR = 2 * HEAD
CHUNK = 64
INV_BLOCK = 8
PREV_CHUNKS = 8
BAND = (PREV_CHUNKS + 1) * CHUNK
REL_MAX = 256
POOL_WINDOWS = (2, 4, 8, 16)
LORA_W, LORA_A, LORA_G, LORA_V = 32, 32, 64, 32
RMS_EPS = 1e-6
GN_EPS = 64e-5
NEG_INF = -1e30
LOG2_E = 1.4426950408889634
VMEM_LIMIT = 56 * 1024 * 1024
FFN_TM = 1024
FFN_TF = 256
ROW_TM = 1024
RWKV_TC = 1024
ATTN_QB = 4
MEM_KV_ROWS = 4


def _dot(a, b, precision=None):
    return jnp.dot(a, b, preferred_element_type=F32, precision=precision)


def _dot_nt(a, b):
    return lax.dot_general(a, b, (((1,), (1,)), ((), ())), preferred_element_type=F32)


def _dot_tn(a, b):
    return lax.dot_general(a, b, (((0,), (0,)), ((), ())), preferred_element_type=F32)


def _split(x, terms):
    parts = []
    for _ in range(terms):
        hi = x.astype(BF16)
        parts.append(hi)
        x = x - hi.astype(F32)
    return parts


def _dot_x3(x, w_hi, w_lo):
    x_hi, x_lo = _split(x, 2)
    return _dot(x_hi, w_hi) + (_dot(x_lo, w_hi) + _dot(x_hi, w_lo))


def _rms(x, g):
    return x * lax.rsqrt(jnp.mean(x * x, axis=-1, keepdims=True) + RMS_EPS) * g


def _sigmoid(x):
    return 1.0 / (1.0 + jnp.exp(-x))


def _params(*sem):
    return pltpu.CompilerParams(dimension_semantics=sem, vmem_limit_bytes=VMEM_LIMIT)


def _layer_spec(a, l):
    zeros = (0,) * (a.ndim - 1)
    return pl.BlockSpec((None,) + a.shape[1:], lambda *_: (l,) + zeros, pipeline_mode=pl.Buffered(1))


def _ffn_kernel(x_ref, g_ref, wi_ref, wo_ref, o_ref, *, tf):
    x = x_ref[...]
    h = _rms(x, g_ref[...]).astype(BF16)
    dff = wo_ref.shape[0]
    acc = None
    for c0 in range(0, dff, tf):
        gate = _dot(h, wi_ref[:, c0:c0 + tf].astype(BF16))
        up = _dot(h, wi_ref[:, dff + c0:dff + c0 + tf].astype(BF16))
        act = (gate * _sigmoid(gate) * up).astype(BF16)
        part = _dot(act, wo_ref[c0:c0 + tf, :].astype(BF16))
        acc = part if acc is None else acc + part
    o_ref[...] = x + 0.5 * acc


def _ffn(x, g, wi, wo, l, *, tm, tf):
    n, d = x.shape
    return pl.pallas_call(
        functools.partial(_ffn_kernel, tf=tf),
        grid=(n // tm,),
        in_specs=[pl.BlockSpec((tm, d), lambda i: (i, 0)), _layer_spec(g, l), _layer_spec(wi, l), _layer_spec(wo, l)],
        out_specs=pl.BlockSpec((tm, d), lambda i: (i, 0)),
        out_shape=jax.ShapeDtypeStruct((n, d), F32),
        compiler_params=_params("parallel"),
        name="ffn",
    )(x, g, wi, wo)


def _inproj_kernel(x_ref, g_ref, w_ref, qg_ref, kg_ref,
                   pa_ref, q_ref, k_ref, v_ref, u_ref, *, a_proj, b_width):
    h = _rms(x_ref[...], g_ref[...]).astype(BF16)
    p = _dot(h, w_ref[...].astype(BF16))
    pa_ref[...] = p[:, :a_proj]
    q = p[:, a_proj:a_proj + b_width]
    k = p[:, a_proj + b_width:a_proj + 2 * b_width]
    qms = _head_sum(q * q) * (1.0 / HEAD)
    kms = _head_sum(k * k) * (1.0 / HEAD)
    q_ref[...] = (q * lax.rsqrt(qms + RMS_EPS) * qg_ref[...]).astype(BF16)
    k_ref[...] = (k * lax.rsqrt(kms + RMS_EPS) * kg_ref[...]).astype(BF16)
    v_ref[...] = p[:, a_proj + 2 * b_width:a_proj + 3 * b_width].astype(BF16)
    u_ref[...] = p[:, a_proj + 3 * b_width:]


def _inproj(x, g, w, qg, kg, l, *, tm, a_proj, b_width, c_width):
    n, d = x.shape
    row = lambda w_: pl.BlockSpec((tm, w_), lambda i: (i, 0))
    lay = lambda a: _layer_spec(a, l)
    return pl.pallas_call(
        functools.partial(_inproj_kernel, a_proj=a_proj, b_width=b_width),
        grid=(n // tm,),
        in_specs=[row(d), lay(g), lay(w), lay(qg), lay(kg)],
        out_specs=[row(a_proj), row(b_width), row(b_width), row(b_width), row(c_width)],
        out_shape=[jax.ShapeDtypeStruct((n, a_proj), F32),
                   jax.ShapeDtypeStruct((n, b_width), BF16),
                   jax.ShapeDtypeStruct((n, b_width), BF16),
                   jax.ShapeDtypeStruct((n, b_width), BF16),
                   jax.ShapeDtypeStruct((n, c_width), F32)],
        compiler_params=_params("parallel"),
        name="in_proj",
    )(x, g, w, qg, kg)


def _lane_lt_head(shape):
    return lax.broadcasted_iota(jnp.int32, shape, len(shape) - 1) < HEAD


def _head_sum(x):
    out = []
    for ls in range(0, x.shape[1], PAIR):
        xs = x[:, ls:ls + PAIR]
        first = _lane_lt_head(xs.shape)
        sum_a = jnp.sum(jnp.where(first, xs, 0.0), axis=-1, keepdims=True)
        sum_b = jnp.sum(jnp.where(first, 0.0, xs), axis=-1, keepdims=True)
        out.append(jnp.where(first, sum_a, sum_b))
    return jnp.concatenate(out, axis=1)


def _blockdiag(x2):
    first = _lane_lt_head(x2.shape)
    zero = jnp.zeros_like(x2)
    return jnp.concatenate([jnp.where(first, x2, zero), jnp.where(first, zero, x2)], axis=0)


def _rwkv_chunk_maps(insts):
    c = CHUNK
    b16 = lambda t: t.astype(BF16)
    t_idx = lax.broadcasted_iota(jnp.int32, (c, PAIR), 0)
    s_idx = lax.broadcasted_iota(jnp.int32, (c, PAIR), 1) % HEAD
    strict = t_idx > s_idx
    incl = t_idx >= s_idx
    zero = jnp.zeros((c, PAIR), F32)
    eye = jnp.where(t_idx == s_idx, 1.0, 0.0)
    row = lax.broadcasted_iota(jnp.int32, (PAIR, PAIR), 0)
    col = lax.broadcasted_iota(jnp.int32, (PAIR, PAIR), 1)
    same_head = (row < HEAD) == (col < HEAD)
    zero2 = jnp.zeros((PAIR, PAIR), F32)

    s_all = [_dot_nt(b16(jnp.concatenate([qa, qr], axis=0)),
                     b16(jnp.concatenate([_blockdiag(kb), _blockdiag(kk)], axis=0)))
             for qa, qr, kb, kk, _, _, _, _ in insts]
    a_ab = [jnp.where(strict, s[:c, :PAIR], zero) for s in s_all]
    a_ak = [jnp.where(strict, s[:c, PAIR:], zero) for s in s_all]
    a_rb = [jnp.where(incl, s[c:, :PAIR], zero) for s in s_all]
    a_rk = [jnp.where(incl, s[c:, PAIR:], zero) for s in s_all]
    av = [_dot(b16(jnp.concatenate([ak, rk], axis=0)), b16(_blockdiag(inst[6])))
          for ak, rk, inst in zip(a_ak, a_rk, insts)]

    same_block = lambda blk: (t_idx // blk) == (s_idx // blk)
    a0 = [jnp.where(same_block(INV_BLOCK), a, zero) for a in a_ab]
    a2 = [_dot(b16(a), b16(_blockdiag(a))) for a in a0]
    p1 = [eye + a for a in a0]
    st = [_dot(b16(jnp.concatenate([sq, p], axis=0)), b16(_blockdiag(sq))) for sq, p in zip(a2, p1)]
    p2 = [p + s[c:] for p, s in zip(p1, st)]
    tinv = [p + _dot(b16(p), b16(_blockdiag(s[:c]))) for p, s in zip(p2, st)]
    blk = INV_BLOCK
    while blk < c:
        off_diag = same_block(2 * blk) & jnp.logical_not(same_block(blk))
        x1 = [_dot(b16(jnp.where(off_diag, a, zero)), b16(_blockdiag(t))) for a, t in zip(a_ab, tinv)]
        tinv = [t + _dot(b16(t), b16(_blockdiag(x))) for t, x in zip(tinv, x1)]
        blk *= 2

    r1 = [_dot(b16(t), b16(jnp.concatenate([_blockdiag(inst[0]), _blockdiag(a[:c])], axis=1)))
          for t, a, inst in zip(tinv, av, insts)]
    r2 = [_dot(b16(rb), b16(jnp.concatenate([_blockdiag(r[:, :PAIR]), _blockdiag(r[:, PAIR:])], axis=1)))
          for rb, r in zip(a_rb, r1)]
    out = []
    for r1_i, r2_i, av_i, (qa, qr, kb, kk, kbe, kke, v, wlast_row) in zip(r1, r2, av, insts):
        qa_p, u0 = r1_i[:, :PAIR], r1_i[:, PAIR:]
        m = jnp.where(same_head, _dot_tn(b16(qa_p), b16(kbe)), zero2)
        m = m + jnp.where(row == col, jnp.broadcast_to(wlast_row, (PAIR, PAIR)), zero2)
        n = jnp.where(same_head,
                      _dot_tn(b16(jnp.concatenate([u0, v], axis=0)),
                              b16(jnp.concatenate([kbe, kke], axis=0))), zero2)
        out.append((qr + r2_i[:, :PAIR], r2_i[:, PAIR:] + av_i[c:], m, n))
    return out


def _rwkv_kernel(*refs, has_vres, tc, width):
    if has_vres:
        (pa_ref, mu_ref, wlh_ref, wll_ref, w0_ref, a0_ref, kk_ref, ka_ref, rk_ref, gng_ref, gnb_ref,
         vf_ref, v0_ref, vdh_ref, vdl_ref, vuh_ref, vul_ref, y_ref, prev_ref, s_ref, yraw_ref) = refs
    else:
        (pa_ref, mu_ref, wlh_ref, wll_ref, w0_ref, a0_ref, kk_ref, ka_ref, rk_ref, gng_ref, gnb_ref,
         y_ref, vf_ref, prev_ref, s_ref, yraw_ref) = refs
    n_pairs = width // PAIR
    n_chunks = tc // CHUNK

    @pl.when(pl.program_id(1) == 0)
    def _():
        prev_ref[...] = jnp.zeros_like(prev_ref)
        s_ref[...] = jnp.zeros_like(s_ref)

    p = pa_ref[...]
    row = lax.broadcasted_iota(jnp.int32, p.shape, 0)
    shifted = jnp.where(row == 0, jnp.broadcast_to(prev_ref[0:1, :], p.shape), pltpu.roll(p, 1, axis=0))
    prev_ref[0:1, :] = p[tc - 1:tc, :]
    p = p + mu_ref[...] * (shifted - p)

    r = p[:, :width]
    k = p[:, width:2 * width]
    v = p[:, 2 * width:3 * width]
    lo_in = p[:, 3 * width:]
    lane = lax.broadcasted_iota(jnp.int32, lo_in.shape, 1)
    lo_act = jnp.where(lane < LORA_W, jnp.tanh(lo_in),
                       jnp.where(lane < LORA_W + LORA_A, lo_in, _sigmoid(lo_in)))
    lo = _dot_x3(lo_act, wlh_ref[...], wll_ref[...])
    wz = w0_ref[...] + lo[:, :width]
    w = -(jnp.maximum(-wz, 0.0) + jnp.log(1.0 + jnp.exp(-jnp.abs(wz)))) - 0.5
    lw = -jnp.exp(w)
    a = _sigmoid(a0_ref[...] + lo[:, width:2 * width])
    g = lo[:, 2 * width:]

    if has_vres:
        v_lo = _dot_x3(v, vdh_ref[...], vdl_ref[...])
        gate = _sigmoid(v0_ref[...] + _dot_x3(v_lo, vuh_ref[...], vul_ref[...]))
        v = v + (vf_ref[...] - v) * gate
    else:
        vf_ref[...] = v

    kkx = k * kk_ref[...]
    kk = kkx / jnp.maximum(jnp.sqrt(_head_sum(kkx * kkx)), 1e-12)
    kmod = k * (1.0 + (a - 1.0) * ka_ref[...])

    grp = 2 * CHUNK
    ti = lax.broadcasted_iota(jnp.int32, (grp, grp), 0)
    tj = lax.broadcasted_iota(jnp.int32, (grp, grp), 1)
    tri = jnp.where(((ti // CHUNK) == (tj // CHUNK)) & (tj <= ti), 1.0, 0.0).astype(BF16)
    lw_hi, lw_lo = _split(lw, 2)
    cum = jnp.concatenate([_dot(tri, lw_hi[r0:r0 + grp]) + _dot(tri, lw_lo[r0:r0 + grp])
                           for r0 in range(0, tc, grp)], axis=0)
    tot = jnp.concatenate(
        [jnp.broadcast_to(cum[(ci + 1) * CHUNK - 1:(ci + 1) * CHUNK, :], (CHUNK, width)) for ci in range(n_chunks)],
        axis=0)
    w_inc = jnp.exp(cum)
    w_exc = jnp.exp(cum - lw)
    w_inv = jnp.exp(-cum)
    w_end = jnp.exp(tot - cum)
    w_tot = jnp.exp(tot)

    kka = kk * a
    qa_all = -kk * w_exc
    qr_all = r * w_inc
    kb_all = kka * w_inv
    kk_all = kmod * w_inv
    kbe_all = kka * w_end
    kke_all = kmod * w_end

    insts = []
    for ci in range(n_chunks):
        rs = slice(ci * CHUNK, (ci + 1) * CHUNK)
        for pi in range(n_pairs):
            ls = slice(pi * PAIR, (pi + 1) * PAIR)
            insts.append((qa_all[rs, ls], qr_all[rs, ls], kb_all[rs, ls], kk_all[rs, ls],
                          kbe_all[rs, ls], kke_all[rs, ls], v[rs, ls], w_tot[ci * CHUNK:ci * CHUNK + 1, ls]))
    maps = _rwkv_chunk_maps(insts)
    states = [s_ref[pi] for pi in range(n_pairs)]
    for ci in range(n_chunks):
        for pi in range(n_pairs):
            qr_p, y0, m, n = maps[ci * n_pairs + pi]
            s16 = states[pi].astype(BF16)
            yraw_ref[ci * CHUNK:(ci + 1) * CHUNK, pi * PAIR:(pi + 1) * PAIR] = _dot_nt(qr_p.astype(BF16), s16) + y0
            states[pi] = _dot(s16, m.astype(BF16)) + n
    for pi in range(n_pairs):
        s_ref[pi] = states[pi]

    y = yraw_ref[...]
    mean = _head_sum(y) * (1.0 / HEAD)
    yc = y - mean
    var = _head_sum(yc * yc) * (1.0 / HEAD)
    yn = yc * lax.rsqrt(var + GN_EPS) * gng_ref[...] + gnb_ref[...]
    bonus = _head_sum(r * kmod * rk_ref[...]) * v
    y_ref[...] = ((yn + bonus) * g).astype(y_ref.dtype)


def _rwkv(pa, vfirst, prm, l, *, tc):
    b, t, a_proj = pa.shape
    width = prm["w0"].shape[-1]
    has_vres = vfirst is not None
    tile = lambda w_: pl.BlockSpec((None, tc, w_), lambda i, j: (i, j, 0))
    names = ["mu", "wl_hi", "wl_lo", "w0", "a0", "k_k", "k_a", "r_k", "gn_g", "gn_b"]
    args = [pa] + [prm[nm] for nm in names]
    in_specs = [tile(a_proj)] + [_layer_spec(prm[nm], l) for nm in names]
    y_shape = jax.ShapeDtypeStruct((b, t, width), BF16)
    if has_vres:
        extra = [prm["v0"], prm["vd_hi"], prm["vd_lo"], prm["vu_hi"], prm["vu_lo"]]
        args += [vfirst] + extra
        in_specs += [tile(width)] + [_layer_spec(a, l - 1) for a in extra]
        out_specs, out_shape = tile(width), y_shape
    else:
        out_specs = [tile(width), tile(width)]
        out_shape = [y_shape, jax.ShapeDtypeStruct((b, t, width), F32)]
    return pl.pallas_call(
        functools.partial(_rwkv_kernel, has_vres=has_vres, tc=tc, width=width),
        grid=(b, t // tc),
        in_specs=in_specs,
        out_specs=out_specs,
        out_shape=out_shape,
        scratch_shapes=[pltpu.VMEM((8, a_proj), F32),
                        pltpu.VMEM((width // PAIR, PAIR, PAIR), F32),
                        pltpu.VMEM((tc, width), F32)],
        compiler_params=_params("parallel", "arbitrary"),
        name="rwkv",
    )(*args)


def _attn_pool_kernel(q_ref, k_ref, v_ref, bias_ref, u_ref, pw_ref, ps_ref, yb_ref, yc_ref,
                      kpad_ref, vpad_ref, upad_ref, *, width, qb):
    step = pl.program_id(1)
    rows = qb * CHUNK
    win_rows = (PREV_CHUNKS + qb) * CHUNK

    @pl.when(step == 0)
    def _():
        kpad_ref[:PREV_CHUNKS * CHUNK, :] = jnp.zeros((PREV_CHUNKS * CHUNK, width), BF16)
        vpad_ref[:PREV_CHUNKS * CHUNK, :] = jnp.zeros((PREV_CHUNKS * CHUNK, width), BF16)
        kpad_ref[PREV_CHUNKS * CHUNK:, :] = k_ref[...]
        vpad_ref[PREV_CHUNKS * CHUNK:, :] = v_ref[...]
        upad_ref[:CHUNK, :] = jnp.zeros((CHUNK, upad_ref.shape[1]), F32)
        upad_ref[CHUNK:, :] = u_ref[...]

    start = pl.multiple_of(step * rows, rows)
    q = q_ref[...]
    slot = lax.broadcasted_iota(jnp.int32, (1, win_rows), 1)
    before_start = jnp.where(slot >= PREV_CHUNKS * CHUNK - start, 0.0, NEG_INF)
    first = _lane_lt_head((rows, PAIR))
    n_pairs = width // PAIR
    scores = []
    for pi in range(n_pairs):
        ls = slice(pi * PAIR, (pi + 1) * PAIR)
        q2 = q[:, ls]
        zq = jnp.zeros_like(q2)
        qs = jnp.concatenate([jnp.where(first, q2, zq), jnp.where(first, zq, q2)], axis=0)
        scores.append(_dot_nt(qs, kpad_ref[pl.ds(start, win_rows), ls]) + bias_ref[pi] + before_start)
    probs, sums = [], []
    for s in scores:
        e = jnp.exp2(s - jnp.max(s, axis=-1, keepdims=True))
        sums.append(jnp.sum(e, axis=-1, keepdims=True))
        probs.append(e.astype(BF16))
    for pi in range(n_pairs):
        ls = slice(pi * PAIR, (pi + 1) * PAIR)
        o = _dot(probs[pi], vpad_ref[pl.ds(start, win_rows), ls]) / sums[pi]
        yb_ref[:, ls] = jnp.where(first, o[:rows], o[rows:]).astype(yb_ref.dtype)

    x = upad_ref[pl.ds(start, rows + CHUNK), :]
    cw = x.shape[1]
    acc, span, win_sums = x, 1, []
    for win in POOL_WINDOWS:
        while span < win:
            acc = acc + pltpu.roll(acc, span, axis=0)
            span *= 2
        win_sums.append(acc)
    lane = lax.broadcasted_iota(jnp.int32, (rows, cw), 1)
    t1 = (lax.broadcasted_iota(jnp.int32, (rows, cw), 0) + start + 1).astype(F32)
    pooled = jnp.zeros((rows, cw), F32)
    for gi, win in enumerate(POOL_WINDOWS):
        grp = (lane >= gi * HEAD) & (lane < (gi + 1) * HEAD)
        pooled = jnp.where(grp, win_sums[gi][CHUNK:] / jnp.minimum(t1, float(win)), pooled)
    pooled = pooled - x[CHUNK:]
    yc_ref[...] = (_dot(pooled.astype(BF16), pw_ref[...]) * ps_ref[...]).astype(yc_ref.dtype)


def _attn_pool(q, k, v, bias, u, pw, ps, l, *, qb):
    b, t, width = q.shape
    cw = u.shape[-1]
    rows = qb * CHUNK
    lay = lambda a: _layer_spec(a, l)
    seq = lambda a: pl.BlockSpec((None,) + a.shape[1:], lambda i, j: (i, 0, 0))
    tile = lambda w_: pl.BlockSpec((None, rows, w_), lambda i, j: (i, j, 0))
    return pl.pallas_call(
        functools.partial(_attn_pool_kernel, width=width, qb=qb),
        grid=(b, t // rows),
        in_specs=[tile(width), seq(k), seq(v), lay(bias), seq(u), lay(pw), lay(ps)],
        out_specs=[tile(width), tile(cw)],
        out_shape=[jax.ShapeDtypeStruct((b, t, width), BF16), jax.ShapeDtypeStruct((b, t, cw), BF16)],
        scratch_shapes=[pltpu.VMEM((t + PREV_CHUNKS * CHUNK, width), BF16),
                        pltpu.VMEM((t + PREV_CHUNKS * CHUNK, width), BF16),
                        pltpu.VMEM((t + CHUNK, cw), F32)],
        compiler_params=_params("parallel", "arbitrary"),
        name="attn_pool",
    )(q, k, v, bias, u, pw, ps)


def _out_cross_kernel(x_ref, ya_ref, yb_ref, yc_ref, wout_ref, g_ref, wq_ref, qg_ref,
                      k_ref, v_ref, wo_ref, o_ref, att_ref, *, heads):
    w16 = lambda ref: ref[...].astype(BF16)
    mix = jnp.concatenate([ya_ref[...], yb_ref[...], yc_ref[...]], axis=1)
    x = x_ref[...] + _dot(mix, w16(wout_ref))
    h = _rms(x, g_ref[...]).astype(BF16)
    q = _dot(h, w16(wq_ref))
    hd = q.shape[1] // heads
    cols = [slice(hi * hd, (hi + 1) * hd) for hi in range(heads)]
    qh = [(_rms(q[:, cs], qg_ref[...]) * (hd ** -0.5 * LOG2_E)).astype(BF16) for cs in cols]
    s = [_dot_nt(qi, k_ref[:, cs]) for qi, cs in zip(qh, cols)]
    e = [jnp.exp2(si - jnp.max(si, axis=-1, keepdims=True)) for si in s]
    for ei, cs in zip(e, cols):
        oh = _dot(ei.astype(BF16), v_ref[:, cs]) / jnp.sum(ei, axis=-1, keepdims=True)
        att_ref[:, cs] = oh.astype(BF16)
    o_ref[...] = x + _dot(att_ref[...], w16(wo_ref))


def _out_cross(x, ya, yb, yc, w_out, g, wq, qg, k, v, wo, l, *, tm, heads):
    b, t, d = x.shape
    lay = lambda a: _layer_spec(a, l)
    tile = lambda a: pl.BlockSpec((None, tm, a.shape[-1]), lambda i, j: (i, j, 0))
    mem = lambda a: pl.BlockSpec((None, None) + a.shape[2:], lambda i, j: (l, i, 0, 0))
    return pl.pallas_call(
        functools.partial(_out_cross_kernel, heads=heads),
        grid=(b, t // tm),
        in_specs=[tile(x), tile(ya), tile(yb), tile(yc), lay(w_out),
                  lay(g), lay(wq), lay(qg), mem(k), mem(v), lay(wo)],
        out_specs=tile(x),
        out_shape=jax.ShapeDtypeStruct((b, t, d), F32),
        scratch_shapes=[pltpu.VMEM((tm, d), BF16)],
        compiler_params=_params("parallel", "parallel"),
        name="out_cross",
    )(x, ya, yb, yc, w_out, g, wq, qg, k, v, wo)


def _mem_kv_kernel(mem_ref, g_ref, w_ref, kg_ref, k_ref, v_ref, *, heads):
    nb, m, d = mem_ref.shape
    h = _rms(mem_ref[...].reshape(nb * m, d), g_ref[...]).astype(BF16)
    kv = _dot(h, w_ref[...].astype(BF16))
    hd = d // heads
    for hi in range(heads):
        cs = slice(hi * hd, (hi + 1) * hd)
        k_ref[:, :, cs] = _rms(kv[:, cs], kg_ref[...]).astype(BF16).reshape(nb, m, hd)
    v_ref[...] = kv[:, d:].astype(BF16).reshape(nb, m, d)


def _mem_kv(mem, g, wkv, kg, *, heads):
    b, m, d = mem.shape
    depth = wkv.shape[0]
    nb = max(n for n in range(1, MEM_KV_ROWS + 1) if b % n == 0)
    per_layer = lambda a: pl.BlockSpec((None,) + a.shape[1:], lambda l, i: (l,) + (0,) * (a.ndim - 1))
    out = pl.BlockSpec((None, nb, m, d), lambda l, i: (l, i, 0, 0))
    return pl.pallas_call(
        functools.partial(_mem_kv_kernel, heads=heads),
        grid=(depth, b // nb),
        in_specs=[pl.BlockSpec((nb, m, d), lambda l, i: (i, 0, 0)), per_layer(g), per_layer(wkv), per_layer(kg)],
        out_specs=[out, out],
        out_shape=[jax.ShapeDtypeStruct((depth, b, m, d), BF16)] * 2,
        compiler_params=_params("parallel", "parallel"),
        name="mem_kv",
    )(mem, g, wkv, kg)


def _split_weight(w):
    hi = w.astype(BF16)
    return hi, (w - hi.astype(F32)).astype(BF16)


def _rel_bias_band(rel_bias):
    heads, n_rel = rel_bias.shape
    period = BAND + CHUNK
    ext = jnp.concatenate([rel_bias, jnp.broadcast_to(rel_bias[:, -1:], (heads, BAND - n_rel))], axis=1)
    v = jnp.concatenate([ext[:, ::-1], jnp.broadcast_to(rel_bias[:, -1:], (heads, CHUNK))], axis=1)
    flat = jnp.tile(v, (1, CHUNK))[:, :CHUNK * (period - 1)]
    return flat.reshape(heads, CHUNK, period - 1)[:, :, :BAND]


def _rel_bias_window(rel_bias, qb):
    band = _rel_bias_band(rel_bias)
    heads = band.shape[0]
    per_chunk = [jnp.pad(band, ((0, 0), (0, 0), (qi * CHUNK, (qb - 1 - qi) * CHUNK)), constant_values=NEG_INF)
                 for qi in range(qb)]
    win = jnp.stack(per_chunk, axis=1)
    return win.reshape(heads // 2, 2 * qb * CHUNK, (PREV_CHUNKS + qb) * CHUNK)


def _lora_weight(w_up, a_up, g_up):
    depth, _, width = w_up.shape
    z = lambda r: jnp.zeros((depth, r, width), F32)
    return jnp.concatenate([
        jnp.concatenate([w_up, z(LORA_W), z(LORA_W)], axis=2),
        jnp.concatenate([z(LORA_A), a_up, z(LORA_A)], axis=2),
        jnp.concatenate([z(LORA_G), z(LORA_G), g_up], axis=2)], axis=1)


def _pool_weight(pool_w):
    depth, groups, cg, _ = pool_w.shape
    rows = []
    for gi in range(groups):
        blocks = [pool_w[:, gi] if gj == gi else jnp.zeros((depth, cg, cg), pool_w.dtype) for gj in range(groups)]
        rows.append(jnp.concatenate(blocks, axis=2))
    return jnp.concatenate(rows, axis=1)


def _stack_rows(a):
    return a.reshape(a.shape[0], 1, -1)


def _attn_pool_params(rel_bias, pool_w, pool_scale, qb):
    return {"bias": jax.vmap(functools.partial(_rel_bias_window, qb=qb))(rel_bias * LOG2_E),
            "pw": _pool_weight(pool_w).astype(BF16), "ps": _stack_rows(pool_scale)}


def _rwkv_params(a_mu, a_w0, a_w_up, a_a0, a_a_up, a_g_up, a_k_k, a_k_a, a_r_k, a_gn_g, a_gn_b,
                 a_v0, a_v_down, a_v_up):
    wl_hi, wl_lo = _split_weight(_lora_weight(a_w_up, a_a_up, a_g_up))
    pad_lanes = LANES - LORA_V
    vd_hi, vd_lo = _split_weight(jnp.pad(a_v_down, ((0, 0), (0, 0), (0, pad_lanes))))
    vu_hi, vu_lo = _split_weight(jnp.pad(a_v_up, ((0, 0), (0, pad_lanes), (0, 0))))
    return {"mu": _stack_rows(a_mu), "wl_hi": wl_hi, "wl_lo": wl_lo, "w0": _stack_rows(a_w0),
            "a0": _stack_rows(a_a0), "k_k": _stack_rows(a_k_k), "k_a": _stack_rows(a_k_a),
            "r_k": _stack_rows(a_r_k), "gn_g": _stack_rows(a_gn_g), "gn_b": _stack_rows(a_gn_b),
            "v0": _stack_rows(a_v0),
            "vd_hi": vd_hi, "vd_lo": vd_lo, "vu_hi": vu_hi, "vu_lo": vu_lo}


def kernel(x, mem, norm_ffn1, ffn1_wi, ffn1_wo, norm_mix, w_in, w_out, a_mu, a_w0, a_w_up, a_a0, a_a_up, a_g_up,
           a_k_k, a_k_a, a_r_k, a_gn_g, a_gn_b, a_v0, a_v_down, a_v_up, b_q_gain, b_k_gain, b_rel_bias,
           c_pool_w, c_pool_scale, norm_cross, norm_mem, x_wq, x_wkv, x_wo, x_q_gain, x_k_gain,
           norm_ffn2, ffn2_wi, ffn2_wo):
    b, t, d = x.shape
    depth = w_in.shape[0]
    a_proj = a_mu.shape[-1]
    b_width = b_rel_bias.shape[1] * HEAD
    c_width = c_pool_scale.shape[-1]
    x_heads = d // x_q_gain.shape[-1]
    n_tok = b * t

    ffn1 = (_stack_rows(norm_ffn1), ffn1_wi, ffn1_wo)
    ffn2 = (_stack_rows(norm_ffn2), ffn2_wi, ffn2_wo)
    q_gain = _stack_rows(jnp.tile(b_q_gain, (1, b_width // HEAD))) * (HEAD ** -0.5 * LOG2_E)
    k_gain = _stack_rows(jnp.tile(b_k_gain, (1, b_width // HEAD)))
    rwkv_prm = _rwkv_params(a_mu, a_w0, a_w_up, a_a0, a_a_up, a_g_up, a_k_k, a_k_a, a_r_k, a_gn_g, a_gn_b,
                            a_v0, a_v_down, a_v_up)
    ap_prm = _attn_pool_params(b_rel_bias, c_pool_w, c_pool_scale, ATTN_QB)
    mem_k, mem_v = _mem_kv(mem, _stack_rows(norm_mem), x_wkv, _stack_rows(x_k_gain), heads=x_heads)

    seq = lambda a: a.reshape(b, t, a.shape[-1])
    x = x.reshape(n_tok, d)
    v_first = None
    for l in range(depth):
        x = _ffn(x, *ffn1, l, tm=FFN_TM, tf=FFN_TF)
        pa, q, k, v, u = _inproj(x, _stack_rows(norm_mix), w_in, q_gain, k_gain, l,
                                 tm=ROW_TM, a_proj=a_proj, b_width=b_width, c_width=c_width)
        if l == 0:
            y_a, v_first = _rwkv(seq(pa), None, rwkv_prm, l, tc=RWKV_TC)
        else:
            y_a = _rwkv(seq(pa), v_first, rwkv_prm, l, tc=RWKV_TC)
        y_b, y_c = _attn_pool(seq(q), seq(k), seq(v), ap_prm["bias"], seq(u), ap_prm["pw"], ap_prm["ps"], l,
                              qb=ATTN_QB)
        x = _out_cross(seq(x), y_a, y_b, y_c, w_out, _stack_rows(norm_cross), x_wq, _stack_rows(x_q_gain),
                       mem_k, mem_v, x_wo, l, tm=ROW_TM, heads=x_heads)
        x = _ffn(x.reshape(n_tok, d), *ffn2, l, tm=FFN_TM, tf=FFN_TF)
    return x.reshape(b, t, d)
```

```python
import functools

import numpy as np
import jax
import jax.numpy as jnp
from jax import lax
from jax.experimental import pallas as pl
from jax.experimental.pallas import tpu as pltpu

F32 = jnp.float32
BF16 = jnp.bfloat16

LANES = 128
HEAD = 64
PAIR = 2 * HEAD
CHUNK = 64
INV_BLOCK = 8
PREV_CHUNKS = 8
BAND = (PREV_CHUNKS + 1) * CHUNK
REL_MAX = 256
POOL_WINDOWS = (2, 4, 8, 16)
LORA_W, LORA_A, LORA_G, LORA_V = 32, 32, 64, 32
RMS_EPS = 1e-6
GN_EPS = 64e-5
NEG_INF = -1e30
LOG2_E = 1.4426950408889634
VMEM_LIMIT = 56 * 1024 * 1024
FFN_TM = 1024
FFN_TF = 256
ROW_TM = 1024
MIX_TC = 512
ATTN_QB = 4
MEM_KV_ROWS = 4


def _dot(a, b, precision=None):
    return jnp.dot(a, b, preferred_element_type=F32, precision=precision)


def _dot_nt(a, b):
    return lax.dot_general(a, b, (((1,), (1,)), ((), ())), preferred_element_type=F32)


def _dot_tn(a, b):
    return lax.dot_general(a, b, (((0,), (0,)), ((), ())), preferred_element_type=F32)


def _split(x, terms):
    parts = []
    for _ in range(terms):
        hi = x.astype(BF16)
        parts.append(hi)
        x = x - hi.astype(F32)
    return parts


def _dot_x3(x, w_hi, w_lo):
    x_hi, x_lo = _split(x, 2)
    return _dot(x_hi, w_hi) + (_dot(x_lo, w_hi) + _dot(x_hi, w_lo))


def _rms(x, g):
    return x * lax.rsqrt(jnp.mean(x * x, axis=-1, keepdims=True) + RMS_EPS) * g


def _sigmoid(x):
    return 1.0 / (1.0 + jnp.exp(-x))


def _params(*sem):
    return pltpu.CompilerParams(dimension_semantics=sem, vmem_limit_bytes=VMEM_LIMIT)


def _layer_spec(a, l):
    zeros = (0,) * (a.ndim - 1)
    return pl.BlockSpec((None,) + a.shape[1:], lambda *_: (l,) + zeros, pipeline_mode=pl.Buffered(1))


def _ffn_kernel(x_ref, g_ref, wi_ref, wo_ref, o_ref, *, tf):
    x = x_ref[...]
    h = _rms(x, g_ref[...]).astype(BF16)
    dff = wo_ref.shape[0]
    acc = None
    for c0 in range(0, dff, tf):
        gate = _dot(h, wi_ref[:, c0:c0 + tf].astype(BF16))
        up = _dot(h, wi_ref[:, dff + c0:dff + c0 + tf].astype(BF16))
        act = (gate * _sigmoid(gate) * up).astype(BF16)
        part = _dot(act, wo_ref[c0:c0 + tf, :].astype(BF16))
        acc = part if acc is None else acc + part
    o_ref[...] = x + 0.5 * acc


def _ffn(x, g, wi, wo, l, *, tm, tf):
    n, d = x.shape
    return pl.pallas_call(
        functools.partial(_ffn_kernel, tf=tf),
        grid=(n // tm,),
        in_specs=[pl.BlockSpec((tm, d), lambda i: (i, 0)), _layer_spec(g, l), _layer_spec(wi, l), _layer_spec(wo, l)],
        out_specs=pl.BlockSpec((tm, d), lambda i: (i, 0)),
        out_shape=jax.ShapeDtypeStruct((n, d), F32),
        compiler_params=_params("parallel"),
        name="ffn",
    )(x, g, wi, wo)


def _inproj_kernel(x_ref, g_ref, w_ref, qg_ref, kg_ref,
                   pa_ref, q_ref, k_ref, v_ref, u_ref, *, a_proj, b_width):
    h = _rms(x_ref[...], g_ref[...]).astype(BF16)
    p = _dot(h, w_ref[...].astype(BF16))
    pa_ref[...] = p[:, :a_proj]
    q = p[:, a_proj:a_proj + b_width]
    k = p[:, a_proj + b_width:a_proj + 2 * b_width]
    qms = _head_sum(q * q) * (1.0 / HEAD)
    kms = _head_sum(k * k) * (1.0 / HEAD)
    q_ref[...] = (q * lax.rsqrt(qms + RMS_EPS) * qg_ref[...]).astype(BF16)
    k_ref[...] = (k * lax.rsqrt(kms + RMS_EPS) * kg_ref[...]).astype(BF16)
    v_ref[...] = p[:, a_proj + 2 * b_width:a_proj + 3 * b_width].astype(BF16)
    u_ref[...] = p[:, a_proj + 3 * b_width:]


def _inproj(x, g, w, qg, kg, l, *, tm, a_proj, b_width, c_width):
    n, d = x.shape
    row = lambda w_: pl.BlockSpec((tm, w_), lambda i: (i, 0))
    lay = lambda a: _layer_spec(a, l)
    return pl.pallas_call(
        functools.partial(_inproj_kernel, a_proj=a_proj, b_width=b_width),
        grid=(n // tm,),
        in_specs=[row(d), lay(g), lay(w), lay(qg), lay(kg)],
        out_specs=[row(a_proj), row(b_width), row(b_width), row(b_width), row(c_width)],
        out_shape=[jax.ShapeDtypeStruct((n, a_proj), F32),
                   jax.ShapeDtypeStruct((n, b_width), BF16),
                   jax.ShapeDtypeStruct((n, b_width), BF16),
                   jax.ShapeDtypeStruct((n, b_width), BF16),
                   jax.ShapeDtypeStruct((n, c_width), F32)],
        compiler_params=_params("parallel"),
        name="in_proj",
    )(x, g, w, qg, kg)


def _lane_lt_head(shape):
    return lax.broadcasted_iota(jnp.int32, shape, len(shape) - 1) < HEAD


def _head_sum(x):
    out = []
    for ls in range(0, x.shape[1], PAIR):
        xs = x[:, ls:ls + PAIR]
        first = _lane_lt_head(xs.shape)
        sum_a = jnp.sum(jnp.where(first, xs, 0.0), axis=-1, keepdims=True)
        sum_b = jnp.sum(jnp.where(first, 0.0, xs), axis=-1, keepdims=True)
        out.append(jnp.where(first, sum_a, sum_b))
    return jnp.concatenate(out, axis=1)


def _blockdiag(x2):
    first = _lane_lt_head(x2.shape)
    zero = jnp.zeros_like(x2)
    return jnp.concatenate([jnp.where(first, x2, zero), jnp.where(first, zero, x2)], axis=0)


def _rwkv_chunk_maps(insts):
    c = CHUNK
    b16 = lambda t: t.astype(BF16)
    t_idx = lax.broadcasted_iota(jnp.int32, (c, PAIR), 0)
    s_idx = lax.broadcasted_iota(jnp.int32, (c, PAIR), 1) % HEAD
    strict = t_idx > s_idx
    incl = t_idx >= s_idx
    zero = jnp.zeros((c, PAIR), F32)
    eye = jnp.where(t_idx == s_idx, 1.0, 0.0)
    row = lax.broadcasted_iota(jnp.int32, (PAIR, PAIR), 0)
    col = lax.broadcasted_iota(jnp.int32, (PAIR, PAIR), 1)
    same_head = (row < HEAD) == (col < HEAD)
    zero2 = jnp.zeros((PAIR, PAIR), F32)

    s_all = [_dot_nt(b16(jnp.concatenate([qa, qr], axis=0)),
                     b16(jnp.concatenate([_blockdiag(kb), _blockdiag(kk)], axis=0)))
             for qa, qr, kb, kk, _, _, _, _ in insts]
    a_ab = [jnp.where(strict, s[:c, :PAIR], zero) for s in s_all]
    a_ak = [jnp.where(strict, s[:c, PAIR:], zero) for s in s_all]
    a_rb = [jnp.where(incl, s[c:, :PAIR], zero) for s in s_all]
    a_rk = [jnp.where(incl, s[c:, PAIR:], zero) for s in s_all]
    av = [_dot(b16(jnp.concatenate([ak, rk], axis=0)), b16(_blockdiag(inst[6])))
          for ak, rk, inst in zip(a_ak, a_rk, insts)]

    same_block = lambda blk: (t_idx // blk) == (s_idx // blk)
    a0 = [jnp.where(same_block(INV_BLOCK), a, zero) for a in a_ab]
    a2 = [_dot(b16(a), b16(_blockdiag(a))) for a in a0]
    p1 = [eye + a for a in a0]
    st = [_dot(b16(jnp.concatenate([sq, p], axis=0)), b16(_blockdiag(sq))) for sq, p in zip(a2, p1)]
    p2 = [p + s[c:] for p, s in zip(p1, st)]
    tinv = [p + _dot(b16(p), b16(_blockdiag(s[:c]))) for p, s in zip(p2, st)]
    blk = INV_BLOCK
    while blk < c:
        off_diag = same_block(2 * blk) & jnp.logical_not(same_block(blk))
        x1 = [_dot(b16(jnp.where(off_diag, a, zero)), b16(_blockdiag(t))) for a, t in zip(a_ab, tinv)]
        tinv = [t + _dot(b16(t), b16(_blockdiag(x))) for t, x in zip(tinv, x1)]
        blk *= 2

    r1 = [_dot(b16(t), b16(jnp.concatenate([_blockdiag(inst[0]), _blockdiag(a[:c])], axis=1)))
          for t, a, inst in zip(tinv, av, insts)]
    r2 = [_dot(b16(rb), b16(jnp.concatenate([_blockdiag(r[:, :PAIR]), _blockdiag(r[:, PAIR:])], axis=1)))
          for rb, r in zip(a_rb, r1)]
    out = []
    for r1_i, r2_i, av_i, (qa, qr, kb, kk, kbe, kke, v, wlast_row) in zip(r1, r2, av, insts):
        qa_p, u0 = r1_i[:, :PAIR], r1_i[:, PAIR:]
        m = jnp.where(same_head, _dot_tn(b16(qa_p), b16(kbe)), zero2)
        m = m + jnp.where(row == col, jnp.broadcast_to(wlast_row, (PAIR, PAIR)), zero2)
        n = jnp.where(same_head,
                      _dot_tn(b16(jnp.concatenate([u0, v], axis=0)),
                              b16(jnp.concatenate([kbe, kke], axis=0))), zero2)
        out.append((qr + r2_i[:, :PAIR], r2_i[:, PAIR:] + av_i[c:], m, n))
    return out


def _rwkv_kernel(*refs, has_vres, tc, width, phase="all"):
    if has_vres:
        (pa_ref, mu_ref, wlh_ref, wll_ref, w0_ref, a0_ref, kk_ref, ka_ref, rk_ref, gng_ref, gnb_ref,
         vf_ref, v0_ref, vdh_ref, vdl_ref, vuh_ref, vul_ref, y_ref, prev_ref, s_ref, yraw_ref) = refs
    else:
        (pa_ref, mu_ref, wlh_ref, wll_ref, w0_ref, a0_ref, kk_ref, ka_ref, rk_ref, gng_ref, gnb_ref,
         y_ref, vf_ref, prev_ref, s_ref, yraw_ref) = refs
    n_pairs = width // PAIR
    n_chunks = tc // CHUNK

    if phase != "body":
        @pl.when(pl.program_id(1) == 0)
        def _():
            prev_ref[...] = jnp.zeros_like(prev_ref)
            s_ref[...] = jnp.zeros_like(s_ref)
    if phase == "init":
        return

    p = pa_ref[...]
    row = lax.broadcasted_iota(jnp.int32, p.shape, 0)
    shifted = jnp.where(row == 0, jnp.broadcast_to(prev_ref[0:1, :], p.shape), pltpu.roll(p, 1, axis=0))
    prev_ref[0:1, :] = p[tc - 1:tc, :]
    p = p + mu_ref[...] * (shifted - p)

    r = p[:, :width]
    k = p[:, width:2 * width]
    v = p[:, 2 * width:3 * width]
    lo_in = p[:, 3 * width:]
    lane = lax.broadcasted_iota(jnp.int32, lo_in.shape, 1)
    lo_act = jnp.where(lane < LORA_W, jnp.tanh(lo_in),
                       jnp.where(lane < LORA_W + LORA_A, lo_in, _sigmoid(lo_in)))
    lo = _dot_x3(lo_act, wlh_ref[...], wll_ref[...])
    wz = w0_ref[...] + lo[:, :width]
    w = -(jnp.maximum(-wz, 0.0) + jnp.log(1.0 + jnp.exp(-jnp.abs(wz)))) - 0.5
    lw = -jnp.exp(w)
    a = _sigmoid(a0_ref[...] + lo[:, width:2 * width])
    g = lo[:, 2 * width:]

    if has_vres:
        v_lo = _dot_x3(v, vdh_ref[...], vdl_ref[...])
        gate = _sigmoid(v0_ref[...] + _dot_x3(v_lo, vuh_ref[...], vul_ref[...]))
        v = v + (vf_ref[...] - v) * gate
    else:
        vf_ref[...] = v

    kkx = k * kk_ref[...]
    kk = kkx / jnp.maximum(jnp.sqrt(_head_sum(kkx * kkx)), 1e-12)
    kmod = k * (1.0 + (a - 1.0) * ka_ref[...])

    grp = 2 * CHUNK
    ti = lax.broadcasted_iota(jnp.int32, (grp, grp), 0)
    tj = lax.broadcasted_iota(jnp.int32, (grp, grp), 1)
    tri = jnp.where(((ti // CHUNK) == (tj // CHUNK)) & (tj <= ti), 1.0, 0.0).astype(BF16)
    lw_hi, lw_lo = _split(lw, 2)
    cum = jnp.concatenate([_dot(tri, lw_hi[r0:r0 + grp]) + _dot(tri, lw_lo[r0:r0 + grp])
                           for r0 in range(0, tc, grp)], axis=0)
    tot = jnp.concatenate(
        [jnp.broadcast_to(cum[(ci + 1) * CHUNK - 1:(ci + 1) * CHUNK, :], (CHUNK, width)) for ci in range(n_chunks)],
        axis=0)
    w_inc = jnp.exp(cum)
    w_exc = jnp.exp(cum - lw)
    w_inv = jnp.exp(-cum)
    w_end = jnp.exp(tot - cum)
    w_tot = jnp.exp(tot)

    kka = kk * a
    qa_all = -kk * w_exc
    qr_all = r * w_inc
    kb_all = kka * w_inv
    kk_all = kmod * w_inv
    kbe_all = kka * w_end
    kke_all = kmod * w_end

    insts = []
    for ci in range(n_chunks):
        rs = slice(ci * CHUNK, (ci + 1) * CHUNK)
        for pi in range(n_pairs):
            ls = slice(pi * PAIR, (pi + 1) * PAIR)
            insts.append((qa_all[rs, ls], qr_all[rs, ls], kb_all[rs, ls], kk_all[rs, ls],
                          kbe_all[rs, ls], kke_all[rs, ls], v[rs, ls], w_tot[ci * CHUNK:ci * CHUNK + 1, ls]))
    maps = _rwkv_chunk_maps(insts)
    states = [s_ref[pi] for pi in range(n_pairs)]
    for ci in range(n_chunks):
        for pi in range(n_pairs):
            qr_p, y0, m, n = maps[ci * n_pairs + pi]
            s16 = states[pi].astype(BF16)
            yraw_ref[ci * CHUNK:(ci + 1) * CHUNK, pi * PAIR:(pi + 1) * PAIR] = _dot_nt(qr_p.astype(BF16), s16) + y0
            states[pi] = _dot(s16, m.astype(BF16)) + n
    for pi in range(n_pairs):
        s_ref[pi] = states[pi]

    y = yraw_ref[...]
    mean = _head_sum(y) * (1.0 / HEAD)
    yc = y - mean
    var = _head_sum(yc * yc) * (1.0 / HEAD)
    yn = yc * lax.rsqrt(var + GN_EPS) * gng_ref[...] + gnb_ref[...]
    bonus = _head_sum(r * kmod * rk_ref[...]) * v
    y_ref[...] = ((yn + bonus) * g).astype(y_ref.dtype)


def _attn_pool_kernel(q_ref, k_ref, v_ref, bias_ref, u_ref, pw_ref, ps_ref, yb_ref, yc_ref,
                      kpad_ref, vpad_ref, upad_ref, *, width, qb, subs=1, phase="all"):
    rows = qb * CHUNK
    win_rows = (PREV_CHUNKS + qb) * CHUNK
    n_pairs = width // PAIR

    if phase != "body":
        @pl.when(pl.program_id(1) == 0)
        def _():
            kpad_ref[:PREV_CHUNKS * CHUNK, :] = jnp.zeros((PREV_CHUNKS * CHUNK, width), BF16)
            vpad_ref[:PREV_CHUNKS * CHUNK, :] = jnp.zeros((PREV_CHUNKS * CHUNK, width), BF16)
            kpad_ref[PREV_CHUNKS * CHUNK:, :] = k_ref[...]
            vpad_ref[PREV_CHUNKS * CHUNK:, :] = v_ref[...]
            upad_ref[:CHUNK, :] = jnp.zeros((CHUNK, upad_ref.shape[1]), F32)
            upad_ref[CHUNK:, :] = u_ref[...]
    if phase == "init":
        return

    first = _lane_lt_head((rows, PAIR))
    slot = lax.broadcasted_iota(jnp.int32, (1, win_rows), 1)
    for sub in range(subs):
        out_rows = slice(sub * rows, (sub + 1) * rows)
        start = pl.multiple_of((pl.program_id(1) * subs + sub) * rows, rows)
        before_start = jnp.where(slot >= PREV_CHUNKS * CHUNK - start, 0.0, NEG_INF)
        q = q_ref[out_rows, :]
        scores = []
        for pi in range(n_pairs):
            ls = slice(pi * PAIR, (pi + 1) * PAIR)
            q2 = q[:, ls]
            zq = jnp.zeros_like(q2)
            qs = jnp.concatenate([jnp.where(first, q2, zq), jnp.where(first, zq, q2)], axis=0)
            scores.append(_dot_nt(qs, kpad_ref[pl.ds(start, win_rows), ls]) + bias_ref[pi] + before_start)
        probs, sums = [], []
        for s in scores:
            e = jnp.exp2(s - jnp.max(s, axis=-1, keepdims=True))
            sums.append(jnp.sum(e, axis=-1, keepdims=True))
            probs.append(e.astype(BF16))
        for pi in range(n_pairs):
            ls = slice(pi * PAIR, (pi + 1) * PAIR)
            o = _dot(probs[pi], vpad_ref[pl.ds(start, win_rows), ls]) / sums[pi]
            yb_ref[out_rows, ls] = jnp.where(first, o[:rows], o[rows:]).astype(yb_ref.dtype)

        x = upad_ref[pl.ds(start, rows + CHUNK), :]
        cw = x.shape[1]
        acc, span, win_sums = x, 1, []
        for win in POOL_WINDOWS:
            while span < win:
                acc = acc + pltpu.roll(acc, span, axis=0)
                span *= 2
            win_sums.append(acc)
        lane = lax.broadcasted_iota(jnp.int32, (rows, cw), 1)
        t1 = (lax.broadcasted_iota(jnp.int32, (rows, cw), 0) + start + 1).astype(F32)
        pooled = jnp.zeros((rows, cw), F32)
        for gi, win in enumerate(POOL_WINDOWS):
            grp = (lane >= gi * HEAD) & (lane < (gi + 1) * HEAD)
            pooled = jnp.where(grp, win_sums[gi][CHUNK:] / jnp.minimum(t1, float(win)), pooled)
        pooled = pooled - x[CHUNK:]
        yc_ref[out_rows, :] = (_dot(pooled.astype(BF16), pw_ref[...]) * ps_ref[...]).astype(yc_ref.dtype)


def _mixers_kernel(*refs, n_in_a, n_in_bc, n_out_a, n_scr_a, has_vres, tc, width_a, width_b, qb):
    n_out_bc = 2
    in_a = refs[:n_in_a]
    in_bc = refs[n_in_a:n_in_a + n_in_bc]
    outs = refs[n_in_a + n_in_bc:n_in_a + n_in_bc + n_out_a + n_out_bc]
    out_a, out_bc = outs[:n_out_a], outs[n_out_a:]
    scr = refs[n_in_a + n_in_bc + n_out_a + n_out_bc:]
    refs_a = (*in_a, *out_a, *scr[:n_scr_a])
    refs_bc = (*in_bc, *out_bc, *scr[n_scr_a:])
    subs = tc // (qb * CHUNK)
    _rwkv_kernel(*refs_a, has_vres=has_vres, tc=tc, width=width_a, phase="init")
    _attn_pool_kernel(*refs_bc, width=width_b, qb=qb, subs=subs, phase="init")
    _attn_pool_kernel(*refs_bc, width=width_b, qb=qb, subs=subs, phase="body")
    _rwkv_kernel(*refs_a, has_vres=has_vres, tc=tc, width=width_a, phase="body")


def _mixers(pa, vfirst, prm, q, k, v, bias, u, pw, ps, l, *, tc, qb):
    b, t, a_proj = pa.shape
    width_a = prm["w0"].shape[-1]
    width_b = q.shape[-1]
    cw = u.shape[-1]
    has_vres = vfirst is not None
    tile = lambda w_: pl.BlockSpec((None, tc, w_), lambda i, j: (i, j, 0))
    seq = lambda a: pl.BlockSpec((None,) + a.shape[1:], lambda i, j: (i, 0, 0))
    lay = lambda a: _layer_spec(a, l)
    names = ["mu", "wl_hi", "wl_lo", "w0", "a0", "k_k", "k_a", "r_k", "gn_g", "gn_b"]
    args_a = [pa] + [prm[nm] for nm in names]
    specs_a = [tile(a_proj)] + [lay(prm[nm]) for nm in names]
    out_specs = [tile(width_a)]
    out_shape = [jax.ShapeDtypeStruct((b, t, width_a), BF16)]
    if has_vres:
        extra = [prm["v0"], prm["vd_hi"], prm["vd_lo"], prm["vu_hi"], prm["vu_lo"]]
        args_a += [vfirst] + extra
        specs_a += [tile(width_a)] + [_layer_spec(a, l - 1) for a in extra]
    else:
        out_specs.append(tile(width_a))
        out_shape.append(jax.ShapeDtypeStruct((b, t, width_a), F32))
    n_out_a = len(out_specs)
    args_bc = [q, k, v, bias, u, pw, ps]
    specs_bc = [tile(width_b), seq(k), seq(v), lay(bias), seq(u), lay(pw), lay(ps)]
    out_specs += [tile(width_b), tile(cw)]
    out_shape += [jax.ShapeDtypeStruct((b, t, width_b), BF16), jax.ShapeDtypeStruct((b, t, cw), BF16)]
    scratch_a = [pltpu.VMEM((8, a_proj), F32),
                 pltpu.VMEM((width_a // PAIR, PAIR, PAIR), F32),
                 pltpu.VMEM((tc, width_a), F32)]
    scratch_bc = [pltpu.VMEM((t + PREV_CHUNKS * CHUNK, width_b), BF16),
                  pltpu.VMEM((t + PREV_CHUNKS * CHUNK, width_b), BF16),
                  pltpu.VMEM((t + CHUNK, cw), F32)]
    return pl.pallas_call(
        functools.partial(_mixers_kernel, n_in_a=len(args_a), n_in_bc=len(args_bc), n_out_a=n_out_a,
                          n_scr_a=len(scratch_a), has_vres=has_vres, tc=tc, width_a=width_a, width_b=width_b, qb=qb),
        grid=(b, t // tc),
        in_specs=specs_a + specs_bc,
        out_specs=out_specs,
        out_shape=out_shape,
        scratch_shapes=scratch_a + scratch_bc,
        compiler_params=_params("parallel", "arbitrary"),
        name="mixers",
    )(*args_a, *args_bc)


def _out_cross_kernel(x_ref, ya_ref, yb_ref, yc_ref, wout_ref, g_ref, wq_ref, qg_ref,
                      k_ref, v_ref, wo_ref, o_ref, att_ref, *, heads):
    w16 = lambda ref: ref[...].astype(BF16)
    mix = jnp.concatenate([ya_ref[...], yb_ref[...], yc_ref[...]], axis=1)
    x = x_ref[...] + _dot(mix, w16(wout_ref))
    h = _rms(x, g_ref[...]).astype(BF16)
    q = _dot(h, w16(wq_ref))
    hd = q.shape[1] // heads
    cols = [slice(hi * hd, (hi + 1) * hd) for hi in range(heads)]
    qh = [(_rms(q[:, cs], qg_ref[...]) * (hd ** -0.5 * LOG2_E)).astype(BF16) for cs in cols]
    s = [_dot_nt(qi, k_ref[:, cs]) for qi, cs in zip(qh, cols)]
    e = [jnp.exp2(si - jnp.max(si, axis=-1, keepdims=True)) for si in s]
    for ei, cs in zip(e, cols):
        oh = _dot(ei.astype(BF16), v_ref[:, cs]) / jnp.sum(ei, axis=-1, keepdims=True)
        att_ref[:, cs] = oh.astype(BF16)
    o_ref[...] = x + _dot(att_ref[...], w16(wo_ref))


def _out_cross(x, ya, yb, yc, w_out, g, wq, qg, k, v, wo, l, *, tm, heads):
    b, t, d = x.shape
    lay = lambda a: _layer_spec(a, l)
    tile = lambda a: pl.BlockSpec((None, tm, a.shape[-1]), lambda i, j: (i, j, 0))
    mem = lambda a: pl.BlockSpec((None, None) + a.shape[2:], lambda i, j: (l, i, 0, 0))
    return pl.pallas_call(
        functools.partial(_out_cross_kernel, heads=heads),
        grid=(b, t // tm),
        in_specs=[tile(x), tile(ya), tile(yb), tile(yc), lay(w_out),
                  lay(g), lay(wq), lay(qg), mem(k), mem(v), lay(wo)],
        out_specs=tile(x),
        out_shape=jax.ShapeDtypeStruct((b, t, d), F32),
        scratch_shapes=[pltpu.VMEM((tm, d), BF16)],
        compiler_params=_params("parallel", "parallel"),
        name="out_cross",
    )(x, ya, yb, yc, w_out, g, wq, qg, k, v, wo)


def _mem_kv_kernel(mem_ref, g_ref, w_ref, kg_ref, k_ref, v_ref, *, heads):
    nb, m, d = mem_ref.shape
    h = _rms(mem_ref[...].reshape(nb * m, d), g_ref[...]).astype(BF16)
    kv = _dot(h, w_ref[...].astype(BF16))
    hd = d // heads
    for hi in range(heads):
        cs = slice(hi * hd, (hi + 1) * hd)
        k_ref[:, :, cs] = _rms(kv[:, cs], kg_ref[...]).astype(BF16).reshape(nb, m, hd)
    v_ref[...] = kv[:, d:].astype(BF16).reshape(nb, m, d)


def _mem_kv(mem, g, wkv, kg, *, heads):
    b, m, d = mem.shape
    depth = wkv.shape[0]
    nb = max(n for n in range(1, MEM_KV_ROWS + 1) if b % n == 0)
    per_layer = lambda a: pl.BlockSpec((None,) + a.shape[1:], lambda l, i: (l,) + (0,) * (a.ndim - 1))
    out = pl.BlockSpec((None, nb, m, d), lambda l, i: (l, i, 0, 0))
    return pl.pallas_call(
        functools.partial(_mem_kv_kernel, heads=heads),
        grid=(depth, b // nb),
        in_specs=[pl.BlockSpec((nb, m, d), lambda l, i: (i, 0, 0)), per_layer(g), per_layer(wkv), per_layer(kg)],
        out_specs=[out, out],
        out_shape=[jax.ShapeDtypeStruct((depth, b, m, d), BF16)] * 2,
        compiler_params=_params("parallel", "parallel"),
        name="mem_kv",
    )(mem, g, wkv, kg)


def _split_weight(w):
    hi = w.astype(BF16)
    return hi, (w - hi.astype(F32)).astype(BF16)


def _rel_bias_band(rel_bias):
    heads, n_rel = rel_bias.shape
    period = BAND + CHUNK
    ext = jnp.concatenate([rel_bias, jnp.broadcast_to(rel_bias[:, -1:], (heads, BAND - n_rel))], axis=1)
    v = jnp.concatenate([ext[:, ::-1], jnp.broadcast_to(rel_bias[:, -1:], (heads, CHUNK))], axis=1)
    flat = jnp.tile(v, (1, CHUNK))[:, :CHUNK * (period - 1)]
    return flat.reshape(heads, CHUNK, period - 1)[:, :, :BAND]


def _rel_bias_window(rel_bias, qb):
    band = _rel_bias_band(rel_bias)
    heads = band.shape[0]
    per_chunk = [jnp.pad(band, ((0, 0), (0, 0), (qi * CHUNK, (qb - 1 - qi) * CHUNK)), constant_values=NEG_INF)
                 for qi in range(qb)]
    win = jnp.stack(per_chunk, axis=1)
    return win.reshape(heads // 2, 2 * qb * CHUNK, (PREV_CHUNKS + qb) * CHUNK)


def _lora_weight(w_up, a_up, g_up):
    depth, _, width = w_up.shape
    z = lambda r: jnp.zeros((depth, r, width), F32)
    return jnp.concatenate([
        jnp.concatenate([w_up, z(LORA_W), z(LORA_W)], axis=2),
        jnp.concatenate([z(LORA_A), a_up, z(LORA_A)], axis=2),
        jnp.concatenate([z(LORA_G), z(LORA_G), g_up], axis=2)], axis=1)


def _pool_weight(pool_w):
    depth, groups, cg, _ = pool_w.shape
    rows = []
    for gi in range(groups):
        blocks = [pool_w[:, gi] if gj == gi else jnp.zeros((depth, cg, cg), pool_w.dtype) for gj in range(groups)]
        rows.append(jnp.concatenate(blocks, axis=2))
    return jnp.concatenate(rows, axis=1)


def _stack_rows(a):
    return a.reshape(a.shape[0], 1, -1)


def _attn_pool_params(rel_bias, pool_w, pool_scale, qb):
    return {"bias": jax.vmap(functools.partial(_rel_bias_window, qb=qb))(rel_bias * LOG2_E),
            "pw": _pool_weight(pool_w).astype(BF16), "ps": _stack_rows(pool_scale)}


def _rwkv_params(a_mu, a_w0, a_w_up, a_a0, a_a_up, a_g_up, a_k_k, a_k_a, a_r_k, a_gn_g, a_gn_b,
                 a_v0, a_v_down, a_v_up):
    wl_hi, wl_lo = _split_weight(_lora_weight(a_w_up, a_a_up, a_g_up))
    pad_lanes = LANES - LORA_V
    vd_hi, vd_lo = _split_weight(jnp.pad(a_v_down, ((0, 0), (0, 0), (0, pad_lanes))))
    vu_hi, vu_lo = _split_weight(jnp.pad(a_v_up, ((0, 0), (0, pad_lanes), (0, 0))))
    return {"mu": _stack_rows(a_mu), "wl_hi": wl_hi, "wl_lo": wl_lo, "w0": _stack_rows(a_w0),
            "a0": _stack_rows(a_a0), "k_k": _stack_rows(a_k_k), "k_a": _stack_rows(a_k_a),
            "r_k": _stack_rows(a_r_k), "gn_g": _stack_rows(a_gn_g), "gn_b": _stack_rows(a_gn_b),
            "v0": _stack_rows(a_v0),
            "vd_hi": vd_hi, "vd_lo": vd_lo, "vu_hi": vu_hi, "vu_lo": vu_lo}


def kernel(x, mem, norm_ffn1, ffn1_wi, ffn1_wo, norm_mix, w_in, w_out, a_mu, a_w0, a_w_up, a_a0, a_a_up, a_g_up,
           a_k_k, a_k_a, a_r_k, a_gn_g, a_gn_b, a_v0, a_v_down, a_v_up, b_q_gain, b_k_gain, b_rel_bias,
           c_pool_w, c_pool_scale, norm_cross, norm_mem, x_wq, x_wkv, x_wo, x_q_gain, x_k_gain,
           norm_ffn2, ffn2_wi, ffn2_wo):
    b, t, d = x.shape
    depth = w_in.shape[0]
    a_proj = a_mu.shape[-1]
    b_width = b_rel_bias.shape[1] * HEAD
    c_width = c_pool_scale.shape[-1]
    x_heads = d // x_q_gain.shape[-1]
    n_tok = b * t

    ffn1 = (_stack_rows(norm_ffn1), ffn1_wi, ffn1_wo)
    ffn2 = (_stack_rows(norm_ffn2), ffn2_wi, ffn2_wo)
    q_gain = _stack_rows(jnp.tile(b_q_gain, (1, b_width // HEAD))) * (HEAD ** -0.5 * LOG2_E)
    k_gain = _stack_rows(jnp.tile(b_k_gain, (1, b_width // HEAD)))
    rwkv_prm = _rwkv_params(a_mu, a_w0, a_w_up, a_a0, a_a_up, a_g_up, a_k_k, a_k_a, a_r_k, a_gn_g, a_gn_b,
                            a_v0, a_v_down, a_v_up)
    ap_prm = _attn_pool_params(b_rel_bias, c_pool_w, c_pool_scale, ATTN_QB)
    mem_k, mem_v = _mem_kv(mem, _stack_rows(norm_mem), x_wkv, _stack_rows(x_k_gain), heads=x_heads)

    seq = lambda a: a.reshape(b, t, a.shape[-1])
    x = x.reshape(n_tok, d)
    v_first = None
    for l in range(depth):
        x = _ffn(x, *ffn1, l, tm=FFN_TM, tf=FFN_TF)
        pa, q, k, v, u = _inproj(x, _stack_rows(norm_mix), w_in, q_gain, k_gain, l,
                                 tm=ROW_TM, a_proj=a_proj, b_width=b_width, c_width=c_width)
        mixed = _mixers(seq(pa), v_first, rwkv_prm, seq(q), seq(k), seq(v), ap_prm["bias"], seq(u),
                        ap_prm["pw"], ap_prm["ps"], l, tc=MIX_TC, qb=ATTN_QB)
        if l == 0:
            y_a, v_first, y_b, y_c = mixed
        else:
            y_a, y_b, y_c = mixed
        x = _out_cross(seq(x), y_a, y_b, y_c, w_out, _stack_rows(norm_cross), x_wq, _stack_rows(x_q_gain),
                       mem_k, mem_v, x_wo, l, tm=ROW_TM, heads=x_heads)
        x = _ffn(x.reshape(n_tok, d), *ffn2, l, tm=FFN_TM, tf=FFN_TF)
    return x.reshape(b, t, d)
```

```python
import functools

import jax
import jax.numpy as jnp
from jax import lax
from jax.experimental import pallas as pl
from jax.experimental.pallas import tpu as pltpu

F32 = jnp.float32
BF16 = jnp.bfloat16

LANES = 128
HEAD = 64
PAIR = 2 * HEAD
CHUNK = 64
INV_BLOCK = 8
PREV_CHUNKS = 8
BAND = (PREV_CHUNKS + 1) * CHUNK
REL_MAX = 256
POOL_WINDOWS = (2, 4, 8, 16)
LORA_W, LORA_A, LORA_G, LORA_V = 32, 32, 64, 32
RMS_EPS = 1e-6
GN_EPS = 64e-5
NEG_INF = -1e30
LOG2_E = 1.4426950408889634
V7X_VMEM_BYTES = 64 * 1024 * 1024
VMEM_LIMIT = V7X_VMEM_BYTES * 7 // 8
FFN_TM = 1024
FFN_TF = 256
ROW_TM = 1024
MIX_TC = 512
ATTN_QB = 4
MEM_KV_ROWS = 4


def _dot(a, b):
    return jnp.dot(a, b, preferred_element_type=F32)


def _dot_nt(a, b):
    return lax.dot_general(a, b, (((1,), (1,)), ((), ())), preferred_element_type=F32)


def _dot_tn(a, b):
    return lax.dot_general(a, b, (((0,), (0,)), ((), ())), preferred_element_type=F32)


def _split(x, terms):
    parts = []
    for _ in range(terms):
        hi = x.astype(BF16)
        parts.append(hi)
        x = x - hi.astype(F32)
    return parts


def _dot_x3(x, w_hi, w_lo):
    x_hi, x_lo = _split(x, 2)
    return _dot(x_hi, w_hi) + (_dot(x_lo, w_hi) + _dot(x_hi, w_lo))


def _rms(x, g):
    return x * lax.rsqrt(jnp.mean(x * x, axis=-1, keepdims=True) + RMS_EPS) * g


def _sigmoid(x):
    return 1.0 / (1.0 + jnp.exp(-x))


def _params(*sem):
    return pltpu.CompilerParams(dimension_semantics=sem, vmem_limit_bytes=VMEM_LIMIT)


def _layer_spec(a, l):
    zeros = (0,) * (a.ndim - 1)
    return pl.BlockSpec((None,) + a.shape[1:], lambda *_: (l,) + zeros, pipeline_mode=pl.Buffered(1))


def _ffn_kernel(x_ref, g_ref, wi_ref, wo_ref, o_ref, *, tf):
    x = x_ref[...]
    h = _rms(x, g_ref[...]).astype(BF16)
    dff = wo_ref.shape[0]
    acc = None
    for c0 in range(0, dff, tf):
        gate = _dot(h, wi_ref[:, c0:c0 + tf].astype(BF16))
        up = _dot(h, wi_ref[:, dff + c0:dff + c0 + tf].astype(BF16))
        act = (gate * _sigmoid(gate) * up).astype(BF16)
        part = _dot(act, wo_ref[c0:c0 + tf, :].astype(BF16))
        acc = part if acc is None else acc + part
    o_ref[...] = x + 0.5 * acc


def _ffn(x, g, wi, wo, l, *, tm, tf):
    n, d = x.shape
    return pl.pallas_call(
        functools.partial(_ffn_kernel, tf=tf),
        grid=(n // tm,),
        in_specs=[pl.BlockSpec((tm, d), lambda i: (i, 0)), _layer_spec(g, l), _layer_spec(wi, l), _layer_spec(wo, l)],
        out_specs=pl.BlockSpec((tm, d), lambda i: (i, 0)),
        out_shape=jax.ShapeDtypeStruct((n, d), F32),
        compiler_params=_params("parallel"),
        name="ffn",
    )(x, g, wi, wo)


def _inproj_kernel(x_ref, g_ref, w_ref, qg_ref, kg_ref,
                   pa_ref, q_ref, k_ref, v_ref, u_ref, *, a_proj, b_width):
    h = _rms(x_ref[...], g_ref[...]).astype(BF16)
    p = _dot(h, w_ref[...].astype(BF16))
    pa_ref[...] = p[:, :a_proj]
    q = p[:, a_proj:a_proj + b_width]
    k = p[:, a_proj + b_width:a_proj + 2 * b_width]
    qms = _head_sum(q * q) * (1.0 / HEAD)
    kms = _head_sum(k * k) * (1.0 / HEAD)
    q_ref[...] = (q * lax.rsqrt(qms + RMS_EPS) * qg_ref[...]).astype(BF16)
    k_ref[...] = (k * lax.rsqrt(kms + RMS_EPS) * kg_ref[...]).astype(BF16)
    v_ref[...] = p[:, a_proj + 2 * b_width:a_proj + 3 * b_width].astype(BF16)
    u_ref[...] = p[:, a_proj + 3 * b_width:]


def _inproj(x, g, w, qg, kg, l, *, tm, a_proj, b_width, c_width):
    n, d = x.shape
    row = lambda w_: pl.BlockSpec((tm, w_), lambda i: (i, 0))
    lay = lambda a: _layer_spec(a, l)
    return pl.pallas_call(
        functools.partial(_inproj_kernel, a_proj=a_proj, b_width=b_width),
        grid=(n // tm,),
        in_specs=[row(d), lay(g), lay(w), lay(qg), lay(kg)],
        out_specs=[row(a_proj), row(b_width), row(b_width), row(b_width), row(c_width)],
        out_shape=[jax.ShapeDtypeStruct((n, a_proj), F32),
                   jax.ShapeDtypeStruct((n, b_width), BF16),
                   jax.ShapeDtypeStruct((n, b_width), BF16),
                   jax.ShapeDtypeStruct((n, b_width), BF16),
                   jax.ShapeDtypeStruct((n, c_width), F32)],
        compiler_params=_params("parallel"),
        name="in_proj",
    )(x, g, w, qg, kg)


def _lane_lt_head(shape):
    return lax.broadcasted_iota(jnp.int32, shape, len(shape) - 1) < HEAD


def _head_sum(x):
    out = []
    for ls in range(0, x.shape[1], PAIR):
        xs = x[:, ls:ls + PAIR]
        first = _lane_lt_head(xs.shape)
        sum_a = jnp.sum(jnp.where(first, xs, 0.0), axis=-1, keepdims=True)
        sum_b = jnp.sum(jnp.where(first, 0.0, xs), axis=-1, keepdims=True)
        out.append(jnp.where(first, sum_a, sum_b))
    return jnp.concatenate(out, axis=1)


def _blockdiag(x2):
    first = _lane_lt_head(x2.shape)
    zero = jnp.zeros_like(x2)
    return jnp.concatenate([jnp.where(first, x2, zero), jnp.where(first, zero, x2)], axis=0)


def _rwkv_chunk_maps(insts):
    c = CHUNK
    b16 = lambda t: t.astype(BF16)
    t_idx = lax.broadcasted_iota(jnp.int32, (c, PAIR), 0)
    s_idx = lax.broadcasted_iota(jnp.int32, (c, PAIR), 1) % HEAD
    strict = t_idx > s_idx
    incl = t_idx >= s_idx
    zero = jnp.zeros((c, PAIR), F32)
    eye = jnp.where(t_idx == s_idx, 1.0, 0.0)
    row = lax.broadcasted_iota(jnp.int32, (PAIR, PAIR), 0)
    col = lax.broadcasted_iota(jnp.int32, (PAIR, PAIR), 1)
    same_head = (row < HEAD) == (col < HEAD)
    zero2 = jnp.zeros((PAIR, PAIR), F32)

    s_all = [_dot_nt(b16(jnp.concatenate([qa, qr], axis=0)),
                     b16(jnp.concatenate([_blockdiag(kb), _blockdiag(kk)], axis=0)))
             for qa, qr, kb, kk, _, _, _, _ in insts]
    a_ab = [jnp.where(strict, s[:c, :PAIR], zero) for s in s_all]
    a_ak = [jnp.where(strict, s[:c, PAIR:], zero) for s in s_all]
    a_rb = [jnp.where(incl, s[c:, :PAIR], zero) for s in s_all]
    a_rk = [jnp.where(incl, s[c:, PAIR:], zero) for s in s_all]
    av = [_dot(b16(jnp.concatenate([ak, rk], axis=0)), b16(_blockdiag(inst[6])))
          for ak, rk, inst in zip(a_ak, a_rk, insts)]

    same_block = lambda blk: (t_idx // blk) == (s_idx // blk)
    a0 = [jnp.where(same_block(INV_BLOCK), a, zero) for a in a_ab]
    a2 = [_dot(b16(a), b16(_blockdiag(a))) for a in a0]
    p1 = [eye + a for a in a0]
    st = [_dot(b16(jnp.concatenate([sq, p], axis=0)), b16(_blockdiag(sq))) for sq, p in zip(a2, p1)]
    p2 = [p + s[c:] for p, s in zip(p1, st)]
    tinv = [p + _dot(b16(p), b16(_blockdiag(s[:c]))) for p, s in zip(p2, st)]
    blk = INV_BLOCK
    while blk < c:
        off_diag = same_block(2 * blk) & jnp.logical_not(same_block(blk))
        x1 = [_dot(b16(jnp.where(off_diag, a, zero)), b16(_blockdiag(t))) for a, t in zip(a_ab, tinv)]
        tinv = [t + _dot(b16(t), b16(_blockdiag(x))) for t, x in zip(tinv, x1)]
        blk *= 2

    r1 = [_dot(b16(t), b16(jnp.concatenate([_blockdiag(inst[0]), _blockdiag(a[:c])], axis=1)))
          for t, a, inst in zip(tinv, av, insts)]
    r2 = [_dot(b16(rb), b16(jnp.concatenate([_blockdiag(r[:, :PAIR]), _blockdiag(r[:, PAIR:])], axis=1)))
          for rb, r in zip(a_rb, r1)]
    out = []
    for r1_i, r2_i, av_i, (qa, qr, kb, kk, kbe, kke, v, wlast_row) in zip(r1, r2, av, insts):
        qa_p, u0 = r1_i[:, :PAIR], r1_i[:, PAIR:]
        m = jnp.where(same_head, _dot_tn(b16(qa_p), b16(kbe)), zero2)
        m = m + jnp.where(row == col, jnp.broadcast_to(wlast_row, (PAIR, PAIR)), zero2)
        n = jnp.where(same_head,
                      _dot_tn(b16(jnp.concatenate([u0, v], axis=0)),
                              b16(jnp.concatenate([kbe, kke], axis=0))), zero2)
        out.append((qr + r2_i[:, :PAIR], r2_i[:, PAIR:] + av_i[c:], m, n))
    return out


def _rwkv_init(*refs):
    prev_ref, s_ref = refs[-3], refs[-2]

    @pl.when(pl.program_id(1) == 0)
    def _():
        prev_ref[...] = jnp.zeros_like(prev_ref)
        s_ref[...] = jnp.zeros_like(s_ref)


def _rwkv_body(*refs, has_vres, tc, width):
    if has_vres:
        (pa_ref, mu_ref, wlh_ref, wll_ref, w0_ref, a0_ref, kk_ref, ka_ref, rk_ref, gng_ref, gnb_ref,
         vf_ref, v0_ref, vdh_ref, vdl_ref, vuh_ref, vul_ref, y_ref, prev_ref, s_ref, yraw_ref) = refs
    else:
        (pa_ref, mu_ref, wlh_ref, wll_ref, w0_ref, a0_ref, kk_ref, ka_ref, rk_ref, gng_ref, gnb_ref,
         y_ref, vf_ref, prev_ref, s_ref, yraw_ref) = refs
    n_pairs = width // PAIR
    n_chunks = tc // CHUNK

    p = pa_ref[...]
    row = lax.broadcasted_iota(jnp.int32, p.shape, 0)
    shifted = jnp.where(row == 0, jnp.broadcast_to(prev_ref[0:1, :], p.shape), pltpu.roll(p, 1, axis=0))
    prev_ref[0:1, :] = p[tc - 1:tc, :]
    p = p + mu_ref[...] * (shifted - p)

    r = p[:, :width]
    k = p[:, width:2 * width]
    v = p[:, 2 * width:3 * width]
    lo_in = p[:, 3 * width:]
    lane = lax.broadcasted_iota(jnp.int32, lo_in.shape, 1)
    lo_act = jnp.where(lane < LORA_W, jnp.tanh(lo_in),
                       jnp.where(lane < LORA_W + LORA_A, lo_in, _sigmoid(lo_in)))
    lo = _dot_x3(lo_act, wlh_ref[...], wll_ref[...])
    wz = w0_ref[...] + lo[:, :width]
    w = -(jnp.maximum(-wz, 0.0) + jnp.log(1.0 + jnp.exp(-jnp.abs(wz)))) - 0.5
    lw = -jnp.exp(w)
    a = _sigmoid(a0_ref[...] + lo[:, width:2 * width])
    g = lo[:, 2 * width:]

    if has_vres:
        v_lo = _dot_x3(v, vdh_ref[...], vdl_ref[...])
        gate = _sigmoid(v0_ref[...] + _dot_x3(v_lo, vuh_ref[...], vul_ref[...]))
        v = v + (vf_ref[...] - v) * gate
    else:
        vf_ref[...] = v

    kkx = k * kk_ref[...]
    kk = kkx / jnp.maximum(jnp.sqrt(_head_sum(kkx * kkx)), 1e-12)
    kmod = k * (1.0 + (a - 1.0) * ka_ref[...])

    grp = 2 * CHUNK
    ti = lax.broadcasted_iota(jnp.int32, (grp, grp), 0)
    tj = lax.broadcasted_iota(jnp.int32, (grp, grp), 1)
    tri = jnp.where(((ti // CHUNK) == (tj // CHUNK)) & (tj <= ti), 1.0, 0.0).astype(BF16)
    lw_hi, lw_lo = _split(lw, 2)
    cum = jnp.concatenate([_dot(tri, lw_hi[r0:r0 + grp]) + _dot(tri, lw_lo[r0:r0 + grp])
                           for r0 in range(0, tc, grp)], axis=0)
    tot = jnp.concatenate(
        [jnp.broadcast_to(cum[(ci + 1) * CHUNK - 1:(ci + 1) * CHUNK, :], (CHUNK, width)) for ci in range(n_chunks)],
        axis=0)
    w_inc = jnp.exp(cum)
    w_exc = jnp.exp(cum - lw)
    w_inv = jnp.exp(-cum)
    w_end = jnp.exp(tot - cum)
    w_tot = jnp.exp(tot)

    kka = kk * a
    qa_all = -kk * w_exc
    qr_all = r * w_inc
    kb_all = kka * w_inv
    kk_all = kmod * w_inv
    kbe_all = kka * w_end
    kke_all = kmod * w_end

    insts = []
    for ci in range(n_chunks):
        rs = slice(ci * CHUNK, (ci + 1) * CHUNK)
        for pi in range(n_pairs):
            ls = slice(pi * PAIR, (pi + 1) * PAIR)
            insts.append((qa_all[rs, ls], qr_all[rs, ls], kb_all[rs, ls], kk_all[rs, ls],
                          kbe_all[rs, ls], kke_all[rs, ls], v[rs, ls], w_tot[ci * CHUNK:ci * CHUNK + 1, ls]))
    maps = _rwkv_chunk_maps(insts)
    states = [s_ref[pi] for pi in range(n_pairs)]
    for ci in range(n_chunks):
        for pi in range(n_pairs):
            qr_p, y0, m, n = maps[ci * n_pairs + pi]
            s16 = states[pi].astype(BF16)
            yraw_ref[ci * CHUNK:(ci + 1) * CHUNK, pi * PAIR:(pi + 1) * PAIR] = _dot_nt(qr_p.astype(BF16), s16) + y0
            states[pi] = _dot(s16, m.astype(BF16)) + n
    for pi in range(n_pairs):
        s_ref[pi] = states[pi]

    y = yraw_ref[...]
    mean = _head_sum(y) * (1.0 / HEAD)
    yc = y - mean
    var = _head_sum(yc * yc) * (1.0 / HEAD)
    yn = yc * lax.rsqrt(var + GN_EPS) * gng_ref[...] + gnb_ref[...]
    bonus = _head_sum(r * kmod * rk_ref[...]) * v
    y_ref[...] = ((yn + bonus) * g).astype(y_ref.dtype)


def _attn_pool_init(q_ref, k_ref, v_ref, bias_ref, u_ref, pw_ref, ps_ref, yb_ref, yc_ref,
                    kpad_ref, vpad_ref, upad_ref):
    width = k_ref.shape[1]

    @pl.when(pl.program_id(1) == 0)
    def _():
        kpad_ref[:PREV_CHUNKS * CHUNK, :] = jnp.zeros((PREV_CHUNKS * CHUNK, width), BF16)
        vpad_ref[:PREV_CHUNKS * CHUNK, :] = jnp.zeros((PREV_CHUNKS * CHUNK, width), BF16)
        kpad_ref[PREV_CHUNKS * CHUNK:, :] = k_ref[...]
        vpad_ref[PREV_CHUNKS * CHUNK:, :] = v_ref[...]
        upad_ref[:CHUNK, :] = jnp.zeros((CHUNK, upad_ref.shape[1]), F32)
        upad_ref[CHUNK:, :] = u_ref[...]


def _attn_pool_body(q_ref, k_ref, v_ref, bias_ref, u_ref, pw_ref, ps_ref, yb_ref, yc_ref,
                    kpad_ref, vpad_ref, upad_ref, *, width, qb, subs):
    rows = qb * CHUNK
    win_rows = (PREV_CHUNKS + qb) * CHUNK
    n_pairs = width // PAIR
    first = _lane_lt_head((rows, PAIR))
    slot = lax.broadcasted_iota(jnp.int32, (1, win_rows), 1)
    for sub in range(subs):
        out_rows = slice(sub * rows, (sub + 1) * rows)
        start = pl.multiple_of((pl.program_id(1) * subs + sub) * rows, rows)
        before_start = jnp.where(slot >= PREV_CHUNKS * CHUNK - start, 0.0, NEG_INF)
        q = q_ref[out_rows, :]
        scores = []
        for pi in range(n_pairs):
            ls = slice(pi * PAIR, (pi + 1) * PAIR)
            q2 = q[:, ls]
            zq = jnp.zeros_like(q2)
            qs = jnp.concatenate([jnp.where(first, q2, zq), jnp.where(first, zq, q2)], axis=0)
            scores.append(_dot_nt(qs, kpad_ref[pl.ds(start, win_rows), ls]) + bias_ref[pi] + before_start)
        probs, sums = [], []
        for s in scores:
            e = jnp.exp2(s - jnp.max(s, axis=-1, keepdims=True))
            sums.append(jnp.sum(e, axis=-1, keepdims=True))
            probs.append(e.astype(BF16))
        for pi in range(n_pairs):
            ls = slice(pi * PAIR, (pi + 1) * PAIR)
            o = _dot(probs[pi], vpad_ref[pl.ds(start, win_rows), ls]) / sums[pi]
            yb_ref[out_rows, ls] = jnp.where(first, o[:rows], o[rows:]).astype(yb_ref.dtype)

        x = upad_ref[pl.ds(start, rows + CHUNK), :]
        cw = x.shape[1]
        acc, span, win_sums = x, 1, []
        for win in POOL_WINDOWS:
            while span < win:
                acc = acc + pltpu.roll(acc, span, axis=0)
                span *= 2
            win_sums.append(acc)
        lane = lax.broadcasted_iota(jnp.int32, (rows, cw), 1)
        t1 = (lax.broadcasted_iota(jnp.int32, (rows, cw), 0) + start + 1).astype(F32)
        pooled = jnp.zeros((rows, cw), F32)
        for gi, win in enumerate(POOL_WINDOWS):
            grp = (lane >= gi * HEAD) & (lane < (gi + 1) * HEAD)
            pooled = jnp.where(grp, win_sums[gi][CHUNK:] / jnp.minimum(t1, float(win)), pooled)
        pooled = pooled - x[CHUNK:]
        yc_ref[out_rows, :] = (_dot(pooled.astype(BF16), pw_ref[...]) * ps_ref[...]).astype(yc_ref.dtype)


def _mixers_kernel(*refs, n_in_a, n_in_bc, n_out_a, n_scr_a, has_vres, tc, width_a, width_b, qb):
    n_out_bc = 2
    in_a = refs[:n_in_a]
    in_bc = refs[n_in_a:n_in_a + n_in_bc]
    outs = refs[n_in_a + n_in_bc:n_in_a + n_in_bc + n_out_a + n_out_bc]
    out_a, out_bc = outs[:n_out_a], outs[n_out_a:]
    scr = refs[n_in_a + n_in_bc + n_out_a + n_out_bc:]
    refs_a = (*in_a, *out_a, *scr[:n_scr_a])
    refs_bc = (*in_bc, *out_bc, *scr[n_scr_a:])
    subs = tc // (qb * CHUNK)
    _rwkv_init(*refs_a)
    _attn_pool_init(*refs_bc)
    _attn_pool_body(*refs_bc, width=width_b, qb=qb, subs=subs)
    _rwkv_body(*refs_a, has_vres=has_vres, tc=tc, width=width_a)


def _mixers(pa, vfirst, prm, q, k, v, bias, u, pw, ps, l, *, tc, qb):
    b, t, a_proj = pa.shape
    width_a = prm["w0"].shape[-1]
    width_b = q.shape[-1]
    cw = u.shape[-1]
    has_vres = vfirst is not None
    tile = lambda w_: pl.BlockSpec((None, tc, w_), lambda i, j: (i, j, 0))
    seq = lambda a: pl.BlockSpec((None,) + a.shape[1:], lambda i, j: (i, 0, 0))
    lay = lambda a: _layer_spec(a, l)
    names = ["mu", "wl_hi", "wl_lo", "w0", "a0", "k_k", "k_a", "r_k", "gn_g", "gn_b"]
    args_a = [pa] + [prm[nm] for nm in names]
    specs_a = [tile(a_proj)] + [lay(prm[nm]) for nm in names]
    out_specs = [tile(width_a)]
    out_shape = [jax.ShapeDtypeStruct((b, t, width_a), BF16)]
    if has_vres:
        extra = [prm["v0"], prm["vd_hi"], prm["vd_lo"], prm["vu_hi"], prm["vu_lo"]]
        args_a += [vfirst] + extra
        specs_a += [tile(width_a)] + [_layer_spec(a, l - 1) for a in extra]
    else:
        out_specs.append(tile(width_a))
        out_shape.append(jax.ShapeDtypeStruct((b, t, width_a), F32))
    n_out_a = len(out_specs)
    args_bc = [q, k, v, bias, u, pw, ps]
    specs_bc = [tile(width_b), seq(k), seq(v), lay(bias), seq(u), lay(pw), lay(ps)]
    out_specs += [tile(width_b), tile(cw)]
    out_shape += [jax.ShapeDtypeStruct((b, t, width_b), BF16), jax.ShapeDtypeStruct((b, t, cw), BF16)]
    scratch_a = [pltpu.VMEM((8, a_proj), F32),
                 pltpu.VMEM((width_a // PAIR, PAIR, PAIR), F32),
                 pltpu.VMEM((tc, width_a), F32)]
    scratch_bc = [pltpu.VMEM((t + PREV_CHUNKS * CHUNK, width_b), BF16),
                  pltpu.VMEM((t + PREV_CHUNKS * CHUNK, width_b), BF16),
                  pltpu.VMEM((t + CHUNK, cw), F32)]
    return pl.pallas_call(
        functools.partial(_mixers_kernel, n_in_a=len(args_a), n_in_bc=len(args_bc), n_out_a=n_out_a,
                          n_scr_a=len(scratch_a), has_vres=has_vres, tc=tc, width_a=width_a, width_b=width_b, qb=qb),
        grid=(b, t // tc),
        in_specs=specs_a + specs_bc,
        out_specs=out_specs,
        out_shape=out_shape,
        scratch_shapes=scratch_a + scratch_bc,
        compiler_params=_params("parallel", "arbitrary"),
        name="mixers",
    )(*args_a, *args_bc)


def _out_cross_kernel(x_ref, ya_ref, yb_ref, yc_ref, wout_ref, g_ref, wq_ref, qg_ref,
                      k_ref, v_ref, wo_ref, o_ref, att_ref, *, heads):
    w16 = lambda ref: ref[...].astype(BF16)
    mix = jnp.concatenate([ya_ref[...], yb_ref[...], yc_ref[...]], axis=1)
    x = x_ref[...] + _dot(mix, w16(wout_ref))
    h = _rms(x, g_ref[...]).astype(BF16)
    q = _dot(h, w16(wq_ref))
    hd = q.shape[1] // heads
    cols = [slice(hi * hd, (hi + 1) * hd) for hi in range(heads)]
    qh = [(_rms(q[:, cs], qg_ref[...]) * (hd ** -0.5 * LOG2_E)).astype(BF16) for cs in cols]
    s = [_dot_nt(qi, k_ref[:, cs]) for qi, cs in zip(qh, cols)]
    e = [jnp.exp2(si - jnp.max(si, axis=-1, keepdims=True)) for si in s]
    for ei, cs in zip(e, cols):
        oh = _dot(ei.astype(BF16), v_ref[:, cs]) / jnp.sum(ei, axis=-1, keepdims=True)
        att_ref[:, cs] = oh.astype(BF16)
    o_ref[...] = x + _dot(att_ref[...], w16(wo_ref))


def _out_cross(x, ya, yb, yc, w_out, g, wq, qg, k, v, wo, l, *, tm, heads):
    b, t, d = x.shape
    lay = lambda a: _layer_spec(a, l)
    tile = lambda a: pl.BlockSpec((None, tm, a.shape[-1]), lambda i, j: (i, j, 0))
    mem = lambda a: pl.BlockSpec((None, None) + a.shape[2:], lambda i, j: (l, i, 0, 0))
    return pl.pallas_call(
        functools.partial(_out_cross_kernel, heads=heads),
        grid=(b, t // tm),
        in_specs=[tile(x), tile(ya), tile(yb), tile(yc), lay(w_out),
                  lay(g), lay(wq), lay(qg), mem(k), mem(v), lay(wo)],
        out_specs=tile(x),
        out_shape=jax.ShapeDtypeStruct((b, t, d), F32),
        scratch_shapes=[pltpu.VMEM((tm, d), BF16)],
        compiler_params=_params("parallel", "parallel"),
        name="out_cross",
    )(x, ya, yb, yc, w_out, g, wq, qg, k, v, wo)


def _mem_kv_kernel(mem_ref, g_ref, w_ref, kg_ref, k_ref, v_ref, *, heads):
    nb, m, d = mem_ref.shape
    h = _rms(mem_ref[...].reshape(nb * m, d), g_ref[...]).astype(BF16)
    kv = _dot(h, w_ref[...].astype(BF16))
    hd = d // heads
    for hi in range(heads):
        cs = slice(hi * hd, (hi + 1) * hd)
        k_ref[:, :, cs] = _rms(kv[:, cs], kg_ref[...]).astype(BF16).reshape(nb, m, hd)
    v_ref[...] = kv[:, d:].astype(BF16).reshape(nb, m, d)


def _mem_kv(mem, g, wkv, kg, *, heads):
    b, m, d = mem.shape
    depth = wkv.shape[0]
    nb = max(n for n in range(1, MEM_KV_ROWS + 1) if b % n == 0)
    per_layer = lambda a: pl.BlockSpec((None,) + a.shape[1:], lambda l, i: (l,) + (0,) * (a.ndim - 1))
    out = pl.BlockSpec((None, nb, m, d), lambda l, i: (l, i, 0, 0))
    return pl.pallas_call(
        functools.partial(_mem_kv_kernel, heads=heads),
        grid=(depth, b // nb),
        in_specs=[pl.BlockSpec((nb, m, d), lambda l, i: (i, 0, 0)), per_layer(g), per_layer(wkv), per_layer(kg)],
        out_specs=[out, out],
        out_shape=[jax.ShapeDtypeStruct((depth, b, m, d), BF16)] * 2,
        compiler_params=_params("parallel", "parallel"),
        name="mem_kv",
    )(mem, g, wkv, kg)


def _split_weight(w):
    hi = w.astype(BF16)
    return hi, (w - hi.astype(F32)).astype(BF16)


def _rel_bias_band(rel_bias):
    heads, n_rel = rel_bias.shape
    period = BAND + CHUNK
    ext = jnp.concatenate([rel_bias, jnp.broadcast_to(rel_bias[:, -1:], (heads, BAND - n_rel))], axis=1)
    v = jnp.concatenate([ext[:, ::-1], jnp.broadcast_to(rel_bias[:, -1:], (heads, CHUNK))], axis=1)
    flat = jnp.tile(v, (1, CHUNK))[:, :CHUNK * (period - 1)]
    return flat.reshape(heads, CHUNK, period - 1)[:, :, :BAND]


def _rel_bias_window(rel_bias, qb):
    band = _rel_bias_band(rel_bias)
    heads = band.shape[0]
    per_chunk = [jnp.pad(band, ((0, 0), (0, 0), (qi * CHUNK, (qb - 1 - qi) * CHUNK)), constant_values=NEG_INF)
                 for qi in range(qb)]
    win = jnp.stack(per_chunk, axis=1)
    return win.reshape(heads // 2, 2 * qb * CHUNK, (PREV_CHUNKS + qb) * CHUNK)


def _lora_weight(w_up, a_up, g_up):
    depth, _, width = w_up.shape
    z = lambda r: jnp.zeros((depth, r, width), F32)
    return jnp.concatenate([
        jnp.concatenate([w_up, z(LORA_W), z(LORA_W)], axis=2),
        jnp.concatenate([z(LORA_A), a_up, z(LORA_A)], axis=2),
        jnp.concatenate([z(LORA_G), z(LORA_G), g_up], axis=2)], axis=1)


def _pool_weight(pool_w):
    depth, groups, cg, _ = pool_w.shape
    rows = []
    for gi in range(groups):
        blocks = [pool_w[:, gi] if gj == gi else jnp.zeros((depth, cg, cg), pool_w.dtype) for gj in range(groups)]
        rows.append(jnp.concatenate(blocks, axis=2))
    return jnp.concatenate(rows, axis=1)


def _stack_rows(a):
    return a.reshape(a.shape[0], 1, -1)


def _attn_pool_params(rel_bias, pool_w, pool_scale, qb):
    return {"bias": jax.vmap(functools.partial(_rel_bias_window, qb=qb))(rel_bias * LOG2_E),
            "pw": _pool_weight(pool_w).astype(BF16), "ps": _stack_rows(pool_scale)}


def _rwkv_params(a_mu, a_w0, a_w_up, a_a0, a_a_up, a_g_up, a_k_k, a_k_a, a_r_k, a_gn_g, a_gn_b,
                 a_v0, a_v_down, a_v_up):
    wl_hi, wl_lo = _split_weight(_lora_weight(a_w_up, a_a_up, a_g_up))
    pad_lanes = LANES - LORA_V
    vd_hi, vd_lo = _split_weight(jnp.pad(a_v_down, ((0, 0), (0, 0), (0, pad_lanes))))
    vu_hi, vu_lo = _split_weight(jnp.pad(a_v_up, ((0, 0), (0, pad_lanes), (0, 0))))
    return {"mu": _stack_rows(a_mu), "wl_hi": wl_hi, "wl_lo": wl_lo, "w0": _stack_rows(a_w0),
            "a0": _stack_rows(a_a0), "k_k": _stack_rows(a_k_k), "k_a": _stack_rows(a_k_a),
            "r_k": _stack_rows(a_r_k), "gn_g": _stack_rows(a_gn_g), "gn_b": _stack_rows(a_gn_b),
            "v0": _stack_rows(a_v0),
            "vd_hi": vd_hi, "vd_lo": vd_lo, "vu_hi": vu_hi, "vu_lo": vu_lo}


def kernel(x, mem, norm_ffn1, ffn1_wi, ffn1_wo, norm_mix, w_in, w_out, a_mu, a_w0, a_w_up, a_a0, a_a_up, a_g_up,
           a_k_k, a_k_a, a_r_k, a_gn_g, a_gn_b, a_v0, a_v_down, a_v_up, b_q_gain, b_k_gain, b_rel_bias,
           c_pool_w, c_pool_scale, norm_cross, norm_mem, x_wq, x_wkv, x_wo, x_q_gain, x_k_gain,
           norm_ffn2, ffn2_wi, ffn2_wo):
    b, t, d = x.shape
    depth = w_in.shape[0]
    a_proj = a_mu.shape[-1]
    b_width = b_rel_bias.shape[1] * HEAD
    c_width = c_pool_scale.shape[-1]
    x_heads = d // x_q_gain.shape[-1]
    n_tok = b * t

    ffn1 = (_stack_rows(norm_ffn1), ffn1_wi, ffn1_wo)
    ffn2 = (_stack_rows(norm_ffn2), ffn2_wi, ffn2_wo)
    q_gain = _stack_rows(jnp.tile(b_q_gain, (1, b_width // HEAD))) * (HEAD ** -0.5 * LOG2_E)
    k_gain = _stack_rows(jnp.tile(b_k_gain, (1, b_width // HEAD)))
    rwkv_prm = _rwkv_params(a_mu, a_w0, a_w_up, a_a0, a_a_up, a_g_up, a_k_k, a_k_a, a_r_k, a_gn_g, a_gn_b,
                            a_v0, a_v_down, a_v_up)
    ap_prm = _attn_pool_params(b_rel_bias, c_pool_w, c_pool_scale, ATTN_QB)
    mem_k, mem_v = _mem_kv(mem, _stack_rows(norm_mem), x_wkv, _stack_rows(x_k_gain), heads=x_heads)

    seq = lambda a: a.reshape(b, t, a.shape[-1])
    x = x.reshape(n_tok, d)
    v_first = None
    for l in range(depth):
        x = _ffn(x, *ffn1, l, tm=FFN_TM, tf=FFN_TF)
        pa, q, k, v, u = _inproj(x, _stack_rows(norm_mix), w_in, q_gain, k_gain, l,
                                 tm=ROW_TM, a_proj=a_proj, b_width=b_width, c_width=c_width)
        mixed = _mixers(seq(pa), v_first, rwkv_prm, seq(q), seq(k), seq(v), ap_prm["bias"], seq(u),
                        ap_prm["pw"], ap_prm["ps"], l, tc=MIX_TC, qb=ATTN_QB)
        if l == 0:
            y_a, v_first, y_b, y_c = mixed
        else:
            y_a, y_b, y_c = mixed
        x = _out_cross(seq(x), y_a, y_b, y_c, w_out, _stack_rows(norm_cross), x_wq, _stack_rows(x_q_gain),
                       mem_k, mem_v, x_wo, l, tm=ROW_TM, heads=x_heads)
        x = _ffn(x.reshape(n_tok, d), *ffn2, l, tm=FFN_TM, tf=FFN_TF)
    return x.reshape(b, t, d)
```

```python
import functools

import jax
import jax.numpy as jnp
from jax import lax
from jax.experimental import pallas as pl
from jax.experimental.pallas import tpu as pltpu

F32 = jnp.float32
BF16 = jnp.bfloat16

LANES = 128
HEAD = 64
PAIR = 2 * HEAD
CHUNK = 64
INV_BLOCK = 8
PREV_CHUNKS = 8
BAND = (PREV_CHUNKS + 1) * CHUNK
REL_MAX = 256
POOL_WINDOWS = (2, 4, 8, 16)
LORA_W, LORA_A, LORA_G, LORA_V = 32, 32, 64, 32
RMS_EPS = 1e-6
GN_EPS = 64e-5
NEG_INF = -1e30
LOG2_E = 1.4426950408889634
V7X_VMEM_BYTES = 64 * 1024 * 1024
VMEM_LIMIT = V7X_VMEM_BYTES * 7 // 8
FFN_TM = 1024
FFN_TF = 256
ROW_TM = 1024
MIX_TC = 512
ATTN_QB = 4
MEM_KV_ROWS = 4


def _dot(a, b):
    return jnp.dot(a, b, preferred_element_type=F32)


def _dot_nt(a, b):
    return lax.dot_general(a, b, (((1,), (1,)), ((), ())), preferred_element_type=F32)


def _dot_tn(a, b):
    return lax.dot_general(a, b, (((0,), (0,)), ((), ())), preferred_element_type=F32)


def _split(x, terms):
    parts = []
    for _ in range(terms):
        hi = x.astype(BF16)
        parts.append(hi)
        x = x - hi.astype(F32)
    return parts


def _dot_x3(x, w_hi, w_lo):
    x_hi, x_lo = _split(x, 2)
    return _dot(x_hi, w_hi) + (_dot(x_lo, w_hi) + _dot(x_hi, w_lo))


def _rms(x, g):
    return x * lax.rsqrt(jnp.mean(x * x, axis=-1, keepdims=True) + RMS_EPS) * g


def _sigmoid(x):
    return 1.0 / (1.0 + jnp.exp(-x))


def _params(*sem):
    return pltpu.CompilerParams(dimension_semantics=sem, vmem_limit_bytes=VMEM_LIMIT)


def _layer_spec(a, l):
    zeros = (0,) * (a.ndim - 1)
    return pl.BlockSpec((None,) + a.shape[1:], lambda *_: (l,) + zeros, pipeline_mode=pl.Buffered(1))


def _ffn_kernel(x_ref, g_ref, wi_ref, wo_ref, o_ref, *, tf):
    x = x_ref[...]
    h = _rms(x, g_ref[...]).astype(BF16)
    dff = wo_ref.shape[0]
    acc = None
    for c0 in range(0, dff, tf):
        gate = _dot(h, wi_ref[:, c0:c0 + tf].astype(BF16))
        up = _dot(h, wi_ref[:, dff + c0:dff + c0 + tf].astype(BF16))
        act = (gate * _sigmoid(gate) * up).astype(BF16)
        part = _dot(act, wo_ref[c0:c0 + tf, :].astype(BF16))
        acc = part if acc is None else acc + part
    o_ref[...] = x + 0.5 * acc


def _ffn(x, g, wi, wo, l, *, tm, tf):
    n, d = x.shape
    return pl.pallas_call(
        functools.partial(_ffn_kernel, tf=tf),
        grid=(n // tm,),
        in_specs=[pl.BlockSpec((tm, d), lambda i: (i, 0)), _layer_spec(g, l), _layer_spec(wi, l), _layer_spec(wo, l)],
        out_specs=pl.BlockSpec((tm, d), lambda i: (i, 0)),
        out_shape=jax.ShapeDtypeStruct((n, d), F32),
        compiler_params=_params("parallel"),
        name="ffn",
    )(x, g, wi, wo)


def _inproj_kernel(x_ref, g_ref, w_ref, qg_ref, kg_ref,
                   pa_ref, q_ref, k_ref, v_ref, u_ref, *, a_proj, b_width):
    h = _rms(x_ref[...], g_ref[...]).astype(BF16)
    p = _dot(h, w_ref[...].astype(BF16))
    pa_ref[...] = p[:, :a_proj]
    q = p[:, a_proj:a_proj + b_width]
    k = p[:, a_proj + b_width:a_proj + 2 * b_width]
    qms = _head_sum(q * q) * (1.0 / HEAD)
    kms = _head_sum(k * k) * (1.0 / HEAD)
    q_ref[...] = (q * lax.rsqrt(qms + RMS_EPS) * qg_ref[...]).astype(BF16)
    k_ref[...] = (k * lax.rsqrt(kms + RMS_EPS) * kg_ref[...]).astype(BF16)
    v_ref[...] = p[:, a_proj + 2 * b_width:a_proj + 3 * b_width].astype(BF16)
    u_ref[...] = p[:, a_proj + 3 * b_width:]


def _inproj(x, g, w, qg, kg, l, *, tm, a_proj, b_width, c_width):
    n, d = x.shape
    row = lambda w_: pl.BlockSpec((tm, w_), lambda i: (i, 0))
    lay = lambda a: _layer_spec(a, l)
    return pl.pallas_call(
        functools.partial(_inproj_kernel, a_proj=a_proj, b_width=b_width),
        grid=(n // tm,),
        in_specs=[row(d), lay(g), lay(w), lay(qg), lay(kg)],
        out_specs=[row(a_proj), row(b_width), row(b_width), row(b_width), row(c_width)],
        out_shape=[jax.ShapeDtypeStruct((n, a_proj), F32),
                   jax.ShapeDtypeStruct((n, b_width), BF16),
                   jax.ShapeDtypeStruct((n, b_width), BF16),
                   jax.ShapeDtypeStruct((n, b_width), BF16),
                   jax.ShapeDtypeStruct((n, c_width), F32)],
        compiler_params=_params("parallel"),
        name="in_proj",
    )(x, g, w, qg, kg)


def _lane_lt_head(shape):
    return lax.broadcasted_iota(jnp.int32, shape, len(shape) - 1) < HEAD


def _head_sum(x):
    out = []
    for ls in range(0, x.shape[1], PAIR):
        xs = x[:, ls:ls + PAIR]
        first = _lane_lt_head(xs.shape)
        sum_a = jnp.sum(jnp.where(first, xs, 0.0), axis=-1, keepdims=True)
        sum_b = jnp.sum(jnp.where(first, 0.0, xs), axis=-1, keepdims=True)
        out.append(jnp.where(first, sum_a, sum_b))
    return jnp.concatenate(out, axis=1)


def _blockdiag(x2):
    first = _lane_lt_head(x2.shape)
    zero = jnp.zeros_like(x2)
    return jnp.concatenate([jnp.where(first, x2, zero), jnp.where(first, zero, x2)], axis=0)


def _rwkv_chunk_maps(insts):
    c = CHUNK
    b16 = lambda t: t.astype(BF16)
    t_idx = lax.broadcasted_iota(jnp.int32, (c, PAIR), 0)
    s_idx = lax.broadcasted_iota(jnp.int32, (c, PAIR), 1) % HEAD
    strict = t_idx > s_idx
    incl = t_idx >= s_idx
    zero = jnp.zeros((c, PAIR), F32)
    eye = jnp.where(t_idx == s_idx, 1.0, 0.0)
    row = lax.broadcasted_iota(jnp.int32, (PAIR, PAIR), 0)
    col = lax.broadcasted_iota(jnp.int32, (PAIR, PAIR), 1)
    same_head = (row < HEAD) == (col < HEAD)
    zero2 = jnp.zeros((PAIR, PAIR), F32)

    s_all = [_dot_nt(b16(jnp.concatenate([qa, qr], axis=0)),
                     b16(jnp.concatenate([_blockdiag(kb), _blockdiag(kk)], axis=0)))
             for qa, qr, kb, kk, _, _, _, _ in insts]
    a_ab = [jnp.where(strict, s[:c, :PAIR], zero) for s in s_all]
    a_ak = [jnp.where(strict, s[:c, PAIR:], zero) for s in s_all]
    a_rb = [jnp.where(incl, s[c:, :PAIR], zero) for s in s_all]
    a_rk = [jnp.where(incl, s[c:, PAIR:], zero) for s in s_all]
    av = [_dot(b16(jnp.concatenate([ak, rk], axis=0)), b16(_blockdiag(inst[6])))
          for ak, rk, inst in zip(a_ak, a_rk, insts)]

    same_block = lambda blk: (t_idx // blk) == (s_idx // blk)
    a0 = [jnp.where(same_block(INV_BLOCK), a, zero) for a in a_ab]
    apow = [_dot(b16(a), b16(_blockdiag(a))) for a in a0]
    prod = [eye + a for a in a0]
    span = 2
    while 2 * span < INV_BLOCK:
        st = [_dot(b16(jnp.concatenate([sq, p], axis=0)), b16(_blockdiag(sq))) for sq, p in zip(apow, prod)]
        prod = [p + s[c:] for p, s in zip(prod, st)]
        apow = [s[:c] for s in st]
        span *= 2
    tinv = [p + _dot(b16(p), b16(_blockdiag(sq))) for p, sq in zip(prod, apow)]
    blk = INV_BLOCK
    while blk < c:
        off_diag = same_block(2 * blk) & jnp.logical_not(same_block(blk))
        x1 = [_dot(b16(jnp.where(off_diag, a, zero)), b16(_blockdiag(t))) for a, t in zip(a_ab, tinv)]
        tinv = [t + _dot(b16(t), b16(_blockdiag(x))) for t, x in zip(tinv, x1)]
        blk *= 2

    r1 = [_dot(b16(t), b16(jnp.concatenate([_blockdiag(inst[0]), _blockdiag(a[:c])], axis=1)))
          for t, a, inst in zip(tinv, av, insts)]
    r2 = [_dot(b16(rb), b16(jnp.concatenate([_blockdiag(r[:, :PAIR]), _blockdiag(r[:, PAIR:])], axis=1)))
          for rb, r in zip(a_rb, r1)]
    out = []
    for r1_i, r2_i, av_i, (qa, qr, kb, kk, kbe, kke, v, wlast_row) in zip(r1, r2, av, insts):
        qa_p, u0 = r1_i[:, :PAIR], r1_i[:, PAIR:]
        m = jnp.where(same_head, _dot_tn(b16(qa_p), b16(kbe)), zero2)
        m = m + jnp.where(row == col, jnp.broadcast_to(wlast_row, (PAIR, PAIR)), zero2)
        n = jnp.where(same_head,
                      _dot_tn(b16(jnp.concatenate([u0, v], axis=0)),
                              b16(jnp.concatenate([kbe, kke], axis=0))), zero2)
        out.append((qr + r2_i[:, :PAIR], r2_i[:, PAIR:] + av_i[c:], m, n))
    return out


def _rwkv_init(*refs):
    prev_ref, s_ref = refs[-3], refs[-2]

    @pl.when(pl.program_id(1) == 0)
    def _():
        prev_ref[...] = jnp.zeros_like(prev_ref)
        s_ref[...] = jnp.zeros_like(s_ref)


def _rwkv_body(*refs, has_vres, tc, width):
    if has_vres:
        (pa_ref, mu_ref, wlh_ref, wll_ref, w0_ref, a0_ref, kk_ref, ka_ref, rk_ref, gng_ref, gnb_ref,
         vf_ref, v0_ref, vdh_ref, vdl_ref, vuh_ref, vul_ref, y_ref, prev_ref, s_ref, yraw_ref) = refs
    else:
        (pa_ref, mu_ref, wlh_ref, wll_ref, w0_ref, a0_ref, kk_ref, ka_ref, rk_ref, gng_ref, gnb_ref,
         y_ref, vf_ref, prev_ref, s_ref, yraw_ref) = refs
    n_pairs = width // PAIR
    n_chunks = tc // CHUNK

    p = pa_ref[...]
    row = lax.broadcasted_iota(jnp.int32, p.shape, 0)
    shifted = jnp.where(row == 0, jnp.broadcast_to(prev_ref[0:1, :], p.shape), pltpu.roll(p, 1, axis=0))
    prev_ref[0:1, :] = p[tc - 1:tc, :]
    p = p + mu_ref[...] * (shifted - p)

    r = p[:, :width]
    k = p[:, width:2 * width]
    v = p[:, 2 * width:3 * width]
    lo_in = p[:, 3 * width:]
    lane = lax.broadcasted_iota(jnp.int32, lo_in.shape, 1)
    lo_act = jnp.where(lane < LORA_W, jnp.tanh(lo_in),
                       jnp.where(lane < LORA_W + LORA_A, lo_in, _sigmoid(lo_in)))
    lo = _dot_x3(lo_act, wlh_ref[...], wll_ref[...])
    wz = w0_ref[...] + lo[:, :width]
    w = -(jnp.maximum(-wz, 0.0) + jnp.log(1.0 + jnp.exp(-jnp.abs(wz)))) - 0.5
    lw = -jnp.exp(w)
    a = _sigmoid(a0_ref[...] + lo[:, width:2 * width])
    g = lo[:, 2 * width:]

    if has_vres:
        v_lo = _dot_x3(v, vdh_ref[...], vdl_ref[...])
        gate = _sigmoid(v0_ref[...] + _dot_x3(v_lo, vuh_ref[...], vul_ref[...]))
        v = v + (vf_ref[...] - v) * gate
    else:
        vf_ref[...] = v

    kkx = k * kk_ref[...]
    kk = kkx / jnp.maximum(jnp.sqrt(_head_sum(kkx * kkx)), 1e-12)
    kmod = k * (1.0 + (a - 1.0) * ka_ref[...])

    grp = 2 * CHUNK
    ti = lax.broadcasted_iota(jnp.int32, (grp, grp), 0)
    tj = lax.broadcasted_iota(jnp.int32, (grp, grp), 1)
    tri = jnp.where(((ti // CHUNK) == (tj // CHUNK)) & (tj <= ti), 1.0, 0.0).astype(BF16)
    lw_hi, lw_lo = _split(lw, 2)
    cum = jnp.concatenate([_dot(tri, lw_hi[r0:r0 + grp]) + _dot(tri, lw_lo[r0:r0 + grp])
                           for r0 in range(0, tc, grp)], axis=0)
    tot = jnp.concatenate(
        [jnp.broadcast_to(cum[(ci + 1) * CHUNK - 1:(ci + 1) * CHUNK, :], (CHUNK, width)) for ci in range(n_chunks)],
        axis=0)
    w_inc = jnp.exp(cum)
    w_exc = jnp.exp(cum - lw)
    w_inv = jnp.exp(-cum)
    w_end = jnp.exp(tot - cum)
    w_tot = jnp.exp(tot)

    kka = kk * a
    qa_all = -kk * w_exc
    qr_all = r * w_inc
    kb_all = kka * w_inv
    kk_all = kmod * w_inv
    kbe_all = kka * w_end
    kke_all = kmod * w_end

    insts = []
    for ci in range(n_chunks):
        rs = slice(ci * CHUNK, (ci + 1) * CHUNK)
        for pi in range(n_pairs):
            ls = slice(pi * PAIR, (pi + 1) * PAIR)
            insts.append((qa_all[rs, ls], qr_all[rs, ls], kb_all[rs, ls], kk_all[rs, ls],
                          kbe_all[rs, ls], kke_all[rs, ls], v[rs, ls], w_tot[ci * CHUNK:ci * CHUNK + 1, ls]))
    maps = _rwkv_chunk_maps(insts)
    states = [s_ref[pi] for pi in range(n_pairs)]
    for ci in range(n_chunks):
        for pi in range(n_pairs):
            qr_p, y0, m, n = maps[ci * n_pairs + pi]
            s16 = states[pi].astype(BF16)
            yraw_ref[ci * CHUNK:(ci + 1) * CHUNK, pi * PAIR:(pi + 1) * PAIR] = _dot_nt(qr_p.astype(BF16), s16) + y0
            states[pi] = _dot(s16, m.astype(BF16)) + n
    for pi in range(n_pairs):
        s_ref[pi] = states[pi]

    y = yraw_ref[...]
    mean = _head_sum(y) * (1.0 / HEAD)
    yc = y - mean
    var = _head_sum(yc * yc) * (1.0 / HEAD)
    yn = yc * lax.rsqrt(var + GN_EPS) * gng_ref[...] + gnb_ref[...]
    bonus = _head_sum(r * kmod * rk_ref[...]) * v
    y_ref[...] = ((yn + bonus) * g).astype(y_ref.dtype)


def _attn_pool_init(q_ref, k_ref, v_ref, bias_ref, u_ref, pw_ref, ps_ref, yb_ref, yc_ref,
                    kpad_ref, vpad_ref, upad_ref):
    width = k_ref.shape[1]

    @pl.when(pl.program_id(1) == 0)
    def _():
        kpad_ref[:PREV_CHUNKS * CHUNK, :] = jnp.zeros((PREV_CHUNKS * CHUNK, width), BF16)
        vpad_ref[:PREV_CHUNKS * CHUNK, :] = jnp.zeros((PREV_CHUNKS * CHUNK, width), BF16)
        kpad_ref[PREV_CHUNKS * CHUNK:, :] = k_ref[...]
        vpad_ref[PREV_CHUNKS * CHUNK:, :] = v_ref[...]
        upad_ref[:CHUNK, :] = jnp.zeros((CHUNK, upad_ref.shape[1]), F32)
        upad_ref[CHUNK:, :] = u_ref[...]


def _attn_pool_body(q_ref, k_ref, v_ref, bias_ref, u_ref, pw_ref, ps_ref, yb_ref, yc_ref,
                    kpad_ref, vpad_ref, upad_ref, *, width, qb, subs):
    rows = qb * CHUNK
    win_rows = (PREV_CHUNKS + qb) * CHUNK
    n_pairs = width // PAIR
    first = _lane_lt_head((rows, PAIR))
    slot = lax.broadcasted_iota(jnp.int32, (1, win_rows), 1)
    for sub in range(subs):
        out_rows = slice(sub * rows, (sub + 1) * rows)
        start = pl.multiple_of((pl.program_id(1) * subs + sub) * rows, rows)
        before_start = jnp.where(slot >= PREV_CHUNKS * CHUNK - start, 0.0, NEG_INF)
        q = q_ref[out_rows, :]
        scores = []
        for pi in range(n_pairs):
            ls = slice(pi * PAIR, (pi + 1) * PAIR)
            q2 = q[:, ls]
            zq = jnp.zeros_like(q2)
            qs = jnp.concatenate([jnp.where(first, q2, zq), jnp.where(first, zq, q2)], axis=0)
            scores.append(_dot_nt(qs, kpad_ref[pl.ds(start, win_rows), ls]) + bias_ref[pi] + before_start)
        probs, sums = [], []
        for s in scores:
            e = jnp.exp2(s - jnp.max(s, axis=-1, keepdims=True))
            sums.append(jnp.sum(e, axis=-1, keepdims=True))
            probs.append(e.astype(BF16))
        for pi in range(n_pairs):
            ls = slice(pi * PAIR, (pi + 1) * PAIR)
            o = _dot(probs[pi], vpad_ref[pl.ds(start, win_rows), ls]) / sums[pi]
            yb_ref[out_rows, ls] = jnp.where(first, o[:rows], o[rows:]).astype(yb_ref.dtype)

        x = upad_ref[pl.ds(start, rows + CHUNK), :]
        cw = x.shape[1]
        acc, span, win_sums = x, 1, []
        for win in POOL_WINDOWS:
            while span < win:
                acc = acc + pltpu.roll(acc, span, axis=0)
                span *= 2
            win_sums.append(acc)
        lane = lax.broadcasted_iota(jnp.int32, (rows, cw), 1)
        t1 = (lax.broadcasted_iota(jnp.int32, (rows, cw), 0) + start + 1).astype(F32)
        pooled = jnp.zeros((rows, cw), F32)
        for gi, win in enumerate(POOL_WINDOWS):
            grp = (lane >= gi * HEAD) & (lane < (gi + 1) * HEAD)
            pooled = jnp.where(grp, win_sums[gi][CHUNK:] / jnp.minimum(t1, float(win)), pooled)
        pooled = pooled - x[CHUNK:]
        yc_ref[out_rows, :] = (_dot(pooled.astype(BF16), pw_ref[...]) * ps_ref[...]).astype(yc_ref.dtype)


def _mixers_kernel(*refs, n_in_a, n_in_bc, n_out_a, n_scr_a, has_vres, tc, width_a, width_b, qb):
    n_out_bc = 2
    in_a = refs[:n_in_a]
    in_bc = refs[n_in_a:n_in_a + n_in_bc]
    outs = refs[n_in_a + n_in_bc:n_in_a + n_in_bc + n_out_a + n_out_bc]
    out_a, out_bc = outs[:n_out_a], outs[n_out_a:]
    scr = refs[n_in_a + n_in_bc + n_out_a + n_out_bc:]
    refs_a = (*in_a, *out_a, *scr[:n_scr_a])
    refs_bc = (*in_bc, *out_bc, *scr[n_scr_a:])
    subs = tc // (qb * CHUNK)
    _rwkv_init(*refs_a)
    _attn_pool_init(*refs_bc)
    _attn_pool_body(*refs_bc, width=width_b, qb=qb, subs=subs)
    _rwkv_body(*refs_a, has_vres=has_vres, tc=tc, width=width_a)


def _mixers(pa, vfirst, prm, q, k, v, bias, u, pw, ps, l, *, tc, qb):
    b, t, a_proj = pa.shape
    width_a = prm["w0"].shape[-1]
    width_b = q.shape[-1]
    cw = u.shape[-1]
    has_vres = vfirst is not None
    tile = lambda w_: pl.BlockSpec((None, tc, w_), lambda i, j: (i, j, 0))
    seq = lambda a: pl.BlockSpec((None,) + a.shape[1:], lambda i, j: (i, 0, 0))
    lay = lambda a: _layer_spec(a, l)
    names = ["mu", "wl_hi", "wl_lo", "w0", "a0", "k_k", "k_a", "r_k", "gn_g", "gn_b"]
    args_a = [pa] + [prm[nm] for nm in names]
    specs_a = [tile(a_proj)] + [lay(prm[nm]) for nm in names]
    out_specs = [tile(width_a)]
    out_shape = [jax.ShapeDtypeStruct((b, t, width_a), BF16)]
    if has_vres:
        extra = [prm["v0"], prm["vd_hi"], prm["vd_lo"], prm["vu_hi"], prm["vu_lo"]]
        args_a += [vfirst] + extra
        specs_a += [tile(width_a)] + [_layer_spec(a, l - 1) for a in extra]
    else:
        out_specs.append(tile(width_a))
        out_shape.append(jax.ShapeDtypeStruct((b, t, width_a), F32))
    n_out_a = len(out_specs)
    args_bc = [q, k, v, bias, u, pw, ps]
    specs_bc = [tile(width_b), seq(k), seq(v), lay(bias), seq(u), lay(pw), lay(ps)]
    out_specs += [tile(width_b), tile(cw)]
    out_shape += [jax.ShapeDtypeStruct((b, t, width_b), BF16), jax.ShapeDtypeStruct((b, t, cw), BF16)]
    scratch_a = [pltpu.VMEM((8, a_proj), F32),
                 pltpu.VMEM((width_a // PAIR, PAIR, PAIR), F32),
                 pltpu.VMEM((tc, width_a), F32)]
    scratch_bc = [pltpu.VMEM((t + PREV_CHUNKS * CHUNK, width_b), BF16),
                  pltpu.VMEM((t + PREV_CHUNKS * CHUNK, width_b), BF16),
                  pltpu.VMEM((t + CHUNK, cw), F32)]
    return pl.pallas_call(
        functools.partial(_mixers_kernel, n_in_a=len(args_a), n_in_bc=len(args_bc), n_out_a=n_out_a,
                          n_scr_a=len(scratch_a), has_vres=has_vres, tc=tc, width_a=width_a, width_b=width_b, qb=qb),
        grid=(b, t // tc),
        in_specs=specs_a + specs_bc,
        out_specs=out_specs,
        out_shape=out_shape,
        scratch_shapes=scratch_a + scratch_bc,
        compiler_params=_params("parallel", "arbitrary"),
        name="mixers",
    )(*args_a, *args_bc)


def _out_cross_kernel(x_ref, ya_ref, yb_ref, yc_ref, wout_ref, g_ref, wq_ref, qg_ref,
                      k_ref, v_ref, wo_ref, o_ref, att_ref, *, heads):
    w16 = lambda ref: ref[...].astype(BF16)
    mix = jnp.concatenate([ya_ref[...], yb_ref[...], yc_ref[...]], axis=1)
    x = x_ref[...] + _dot(mix, w16(wout_ref))
    h = _rms(x, g_ref[...]).astype(BF16)
    q = _dot(h, w16(wq_ref))
    hd = q.shape[1] // heads
    cols = [slice(hi * hd, (hi + 1) * hd) for hi in range(heads)]
    qh = [(_rms(q[:, cs], qg_ref[...]) * (hd ** -0.5 * LOG2_E)).astype(BF16) for cs in cols]
    s = [_dot_nt(qi, k_ref[:, cs]) for qi, cs in zip(qh, cols)]
    e = [jnp.exp2(si - jnp.max(si, axis=-1, keepdims=True)) for si in s]
    for ei, cs in zip(e, cols):
        oh = _dot(ei.astype(BF16), v_ref[:, cs]) / jnp.sum(ei, axis=-1, keepdims=True)
        att_ref[:, cs] = oh.astype(BF16)
    o_ref[...] = x + _dot(att_ref[...], w16(wo_ref))


def _out_cross(x, ya, yb, yc, w_out, g, wq, qg, k, v, wo, l, *, tm, heads):
    b, t, d = x.shape
    lay = lambda a: _layer_spec(a, l)
    tile = lambda a: pl.BlockSpec((None, tm, a.shape[-1]), lambda i, j: (i, j, 0))
    mem = lambda a: pl.BlockSpec((None, None) + a.shape[2:], lambda i, j: (l, i, 0, 0))
    return pl.pallas_call(
        functools.partial(_out_cross_kernel, heads=heads),
        grid=(b, t // tm),
        in_specs=[tile(x), tile(ya), tile(yb), tile(yc), lay(w_out),
                  lay(g), lay(wq), lay(qg), mem(k), mem(v), lay(wo)],
        out_specs=tile(x),
        out_shape=jax.ShapeDtypeStruct((b, t, d), F32),
        scratch_shapes=[pltpu.VMEM((tm, d), BF16)],
        compiler_params=_params("parallel", "parallel"),
        name="out_cross",
    )(x, ya, yb, yc, w_out, g, wq, qg, k, v, wo)


def _mem_kv_kernel(mem_ref, g_ref, w_ref, kg_ref, k_ref, v_ref, *, heads):
    nb, m, d = mem_ref.shape
    h = _rms(mem_ref[...].reshape(nb * m, d), g_ref[...]).astype(BF16)
    kv = _dot(h, w_ref[...].astype(BF16))
    hd = d // heads
    for hi in range(heads):
        cs = slice(hi * hd, (hi + 1) * hd)
        k_ref[:, :, cs] = _rms(kv[:, cs], kg_ref[...]).astype(BF16).reshape(nb, m, hd)
    v_ref[...] = kv[:, d:].astype(BF16).reshape(nb, m, d)


def _mem_kv(mem, g, wkv, kg, *, heads):
    b, m, d = mem.shape
    depth = wkv.shape[0]
    nb = max(n for n in range(1, MEM_KV_ROWS + 1) if b % n == 0)
    per_layer = lambda a: pl.BlockSpec((None,) + a.shape[1:], lambda l, i: (l,) + (0,) * (a.ndim - 1))
    out = pl.BlockSpec((None, nb, m, d), lambda l, i: (l, i, 0, 0))
    return pl.pallas_call(
        functools.partial(_mem_kv_kernel, heads=heads),
        grid=(depth, b // nb),
        in_specs=[pl.BlockSpec((nb, m, d), lambda l, i: (i, 0, 0)), per_layer(g), per_layer(wkv), per_layer(kg)],
        out_specs=[out, out],
        out_shape=[jax.ShapeDtypeStruct((depth, b, m, d), BF16)] * 2,
        compiler_params=_params("parallel", "parallel"),
        name="mem_kv",
    )(mem, g, wkv, kg)


def _split_weight(w):
    hi = w.astype(BF16)
    return hi, (w - hi.astype(F32)).astype(BF16)


def _rel_bias_band(rel_bias):
    heads, n_rel = rel_bias.shape
    period = BAND + CHUNK
    ext = jnp.concatenate([rel_bias, jnp.broadcast_to(rel_bias[:, -1:], (heads, BAND - n_rel))], axis=1)
    v = jnp.concatenate([ext[:, ::-1], jnp.broadcast_to(rel_bias[:, -1:], (heads, CHUNK))], axis=1)
    flat = jnp.tile(v, (1, CHUNK))[:, :CHUNK * (period - 1)]
    return flat.reshape(heads, CHUNK, period - 1)[:, :, :BAND]


def _rel_bias_window(rel_bias, qb):
    band = _rel_bias_band(rel_bias)
    heads = band.shape[0]
    per_chunk = [jnp.pad(band, ((0, 0), (0, 0), (qi * CHUNK, (qb - 1 - qi) * CHUNK)), constant_values=NEG_INF)
                 for qi in range(qb)]
    win = jnp.stack(per_chunk, axis=1)
    return win.reshape(heads // 2, 2 * qb * CHUNK, (PREV_CHUNKS + qb) * CHUNK)


def _lora_weight(w_up, a_up, g_up):
    depth, _, width = w_up.shape
    z = lambda r: jnp.zeros((depth, r, width), F32)
    return jnp.concatenate([
        jnp.concatenate([w_up, z(LORA_W), z(LORA_W)], axis=2),
        jnp.concatenate([z(LORA_A), a_up, z(LORA_A)], axis=2),
        jnp.concatenate([z(LORA_G), z(LORA_G), g_up], axis=2)], axis=1)


def _pool_weight(pool_w):
    depth, groups, cg, _ = pool_w.shape
    rows = []
    for gi in range(groups):
        blocks = [pool_w[:, gi] if gj == gi else jnp.zeros((depth, cg, cg), pool_w.dtype) for gj in range(groups)]
        rows.append(jnp.concatenate(blocks, axis=2))
    return jnp.concatenate(rows, axis=1)


def _stack_rows(a):
    return a.reshape(a.shape[0], 1, -1)


def _attn_pool_params(rel_bias, pool_w, pool_scale, qb):
    return {"bias": jax.vmap(functools.partial(_rel_bias_window, qb=qb))(rel_bias * LOG2_E),
            "pw": _pool_weight(pool_w).astype(BF16), "ps": _stack_rows(pool_scale)}


def _rwkv_params(a_mu, a_w0, a_w_up, a_a0, a_a_up, a_g_up, a_k_k, a_k_a, a_r_k, a_gn_g, a_gn_b,
                 a_v0, a_v_down, a_v_up):
    wl_hi, wl_lo = _split_weight(_lora_weight(a_w_up, a_a_up, a_g_up))
    pad_lanes = LANES - LORA_V
    vd_hi, vd_lo = _split_weight(jnp.pad(a_v_down, ((0, 0), (0, 0), (0, pad_lanes))))
    vu_hi, vu_lo = _split_weight(jnp.pad(a_v_up, ((0, 0), (0, pad_lanes), (0, 0))))
    return {"mu": _stack_rows(a_mu), "wl_hi": wl_hi, "wl_lo": wl_lo, "w0": _stack_rows(a_w0),
            "a0": _stack_rows(a_a0), "k_k": _stack_rows(a_k_k), "k_a": _stack_rows(a_k_a),
            "r_k": _stack_rows(a_r_k), "gn_g": _stack_rows(a_gn_g), "gn_b": _stack_rows(a_gn_b),
            "v0": _stack_rows(a_v0),
            "vd_hi": vd_hi, "vd_lo": vd_lo, "vu_hi": vu_hi, "vu_lo": vu_lo}


def kernel(x, mem, norm_ffn1, ffn1_wi, ffn1_wo, norm_mix, w_in, w_out, a_mu, a_w0, a_w_up, a_a0, a_a_up, a_g_up,
           a_k_k, a_k_a, a_r_k, a_gn_g, a_gn_b, a_v0, a_v_down, a_v_up, b_q_gain, b_k_gain, b_rel_bias,
           c_pool_w, c_pool_scale, norm_cross, norm_mem, x_wq, x_wkv, x_wo, x_q_gain, x_k_gain,
           norm_ffn2, ffn2_wi, ffn2_wo):
    b, t, d = x.shape
    depth = w_in.shape[0]
    a_proj = a_mu.shape[-1]
    b_width = b_rel_bias.shape[1] * HEAD
    c_width = c_pool_scale.shape[-1]
    x_heads = d // x_q_gain.shape[-1]
    n_tok = b * t

    ffn1 = (_stack_rows(norm_ffn1), ffn1_wi, ffn1_wo)
    ffn2 = (_stack_rows(norm_ffn2), ffn2_wi, ffn2_wo)
    q_gain = _stack_rows(jnp.tile(b_q_gain, (1, b_width // HEAD))) * (HEAD ** -0.5 * LOG2_E)
    k_gain = _stack_rows(jnp.tile(b_k_gain, (1, b_width // HEAD)))
    rwkv_prm = _rwkv_params(a_mu, a_w0, a_w_up, a_a0, a_a_up, a_g_up, a_k_k, a_k_a, a_r_k, a_gn_g, a_gn_b,
                            a_v0, a_v_down, a_v_up)
    ap_prm = _attn_pool_params(b_rel_bias, c_pool_w, c_pool_scale, ATTN_QB)
    mem_k, mem_v = _mem_kv(mem, _stack_rows(norm_mem), x_wkv, _stack_rows(x_k_gain), heads=x_heads)

    seq = lambda a: a.reshape(b, t, a.shape[-1])
    x = x.reshape(n_tok, d)
    v_first = None
    for l in range(depth):
        x = _ffn(x, *ffn1, l, tm=FFN_TM, tf=FFN_TF)
        pa, q, k, v, u = _inproj(x, _stack_rows(norm_mix), w_in, q_gain, k_gain, l,
                                 tm=ROW_TM, a_proj=a_proj, b_width=b_width, c_width=c_width)
        mixed = _mixers(seq(pa), v_first, rwkv_prm, seq(q), seq(k), seq(v), ap_prm["bias"], seq(u),
                        ap_prm["pw"], ap_prm["ps"], l, tc=MIX_TC, qb=ATTN_QB)
        if l == 0:
            y_a, v_first, y_b, y_c = mixed
        else:
            y_a, y_b, y_c = mixed
        x = _out_cross(seq(x), y_a, y_b, y_c, w_out, _stack_rows(norm_cross), x_wq, _stack_rows(x_q_gain),
                       mem_k, mem_v, x_wo, l, tm=ROW_TM, heads=x_heads)
        x = _ffn(x.reshape(n_tok, d), *ffn2, l, tm=FFN_TM, tf=FFN_TF)
    return x.reshape(b, t, d)
```

```python
import functools

import jax
import jax.numpy as jnp
from jax import lax
from jax.experimental import pallas as pl
from jax.experimental.pallas import tpu as pltpu

F32 = jnp.float32
BF16 = jnp.bfloat16

LANES = 128
HEAD = 64
PAIR = 2 * HEAD
CHUNK = 64
INV_BLOCK = 8
PREV_CHUNKS = 8
BAND = (PREV_CHUNKS + 1) * CHUNK
REL_MAX = 256
POOL_WINDOWS = (2, 4, 8, 16)
LORA_W, LORA_A, LORA_G, LORA_V = 32, 32, 64, 32
RMS_EPS = 1e-6
GN_EPS = 64e-5
NEG_INF = -1e30
LOG2_E = 1.4426950408889634
V7X_VMEM_BYTES = 64 * 1024 * 1024
VMEM_LIMIT = V7X_VMEM_BYTES * 7 // 8
FFN_TM = 1024
FFN_TF = 256
ROW_TM = 1024
MIX_TC = 512
ATTN_QB = 4
MEM_KV_ROWS = 4


def _dot(a, b):
    return jnp.dot(a, b, preferred_element_type=F32)


def _dot_nt(a, b):
    return lax.dot_general(a, b, (((1,), (1,)), ((), ())), preferred_element_type=F32)


def _dot_tn(a, b):
    return lax.dot_general(a, b, (((0,), (0,)), ((), ())), preferred_element_type=F32)


def _split(x, terms):
    parts = []
    for _ in range(terms):
        hi = x.astype(BF16)
        parts.append(hi)
        x = x - hi.astype(F32)
    return parts


def _dot_x3(x, w_hi, w_lo):
    x_hi, x_lo = _split(x, 2)
    return _dot(x_hi, w_hi) + (_dot(x_lo, w_hi) + _dot(x_hi, w_lo))


def _rms(x, g):
    return x * lax.rsqrt(jnp.mean(x * x, axis=-1, keepdims=True) + RMS_EPS) * g


def _sigmoid(x):
    return 1.0 / (1.0 + jnp.exp(-x))


def _params(*sem):
    return pltpu.CompilerParams(dimension_semantics=sem, vmem_limit_bytes=VMEM_LIMIT)


def _layer_spec(a, l):
    zeros = (0,) * (a.ndim - 1)
    return pl.BlockSpec((None,) + a.shape[1:], lambda *_: (l,) + zeros, pipeline_mode=pl.Buffered(1))


def _ffn_kernel(x_ref, g_ref, wi_ref, wo_ref, o_ref, *, tf):
    x = x_ref[...]
    h = _rms(x, g_ref[...]).astype(BF16)
    dff = wo_ref.shape[0]
    acc = None
    for c0 in range(0, dff, tf):
        gate = _dot(h, wi_ref[:, c0:c0 + tf].astype(BF16))
        up = _dot(h, wi_ref[:, dff + c0:dff + c0 + tf].astype(BF16))
        act = (gate * _sigmoid(gate) * up).astype(BF16)
        part = _dot(act, wo_ref[c0:c0 + tf, :].astype(BF16))
        acc = part if acc is None else acc + part
    o_ref[...] = x + 0.5 * acc


def _ffn(x, g, wi, wo, l, *, tm, tf):
    n, d = x.shape
    return pl.pallas_call(
        functools.partial(_ffn_kernel, tf=tf),
        grid=(n // tm,),
        in_specs=[pl.BlockSpec((tm, d), lambda i: (i, 0)), _layer_spec(g, l), _layer_spec(wi, l), _layer_spec(wo, l)],
        out_specs=pl.BlockSpec((tm, d), lambda i: (i, 0)),
        out_shape=jax.ShapeDtypeStruct((n, d), F32),
        compiler_params=_params("parallel"),
        name="ffn",
    )(x, g, wi, wo)


def _inproj_kernel(x_ref, g_ref, w_ref, qg_ref, kg_ref,
                   pa_ref, q_ref, k_ref, v_ref, u_ref, *, a_proj, b_width):
    h = _rms(x_ref[...], g_ref[...]).astype(BF16)
    p = _dot(h, w_ref[...].astype(BF16))
    pa_ref[...] = p[:, :a_proj]
    q = p[:, a_proj:a_proj + b_width]
    k = p[:, a_proj + b_width:a_proj + 2 * b_width]
    qms = _head_sum(q * q) * (1.0 / HEAD)
    kms = _head_sum(k * k) * (1.0 / HEAD)
    q_ref[...] = (q * lax.rsqrt(qms + RMS_EPS) * qg_ref[...]).astype(BF16)
    k_ref[...] = (k * lax.rsqrt(kms + RMS_EPS) * kg_ref[...]).astype(BF16)
    v_ref[...] = p[:, a_proj + 2 * b_width:a_proj + 3 * b_width].astype(BF16)
    u_ref[...] = p[:, a_proj + 3 * b_width:]


def _inproj(x, g, w, qg, kg, l, *, tm, a_proj, b_width, c_width):
    n, d = x.shape
    row = lambda w_: pl.BlockSpec((tm, w_), lambda i: (i, 0))
    lay = lambda a: _layer_spec(a, l)
    return pl.pallas_call(
        functools.partial(_inproj_kernel, a_proj=a_proj, b_width=b_width),
        grid=(n // tm,),
        in_specs=[row(d), lay(g), lay(w), lay(qg), lay(kg)],
        out_specs=[row(a_proj), row(b_width), row(b_width), row(b_width), row(c_width)],
        out_shape=[jax.ShapeDtypeStruct((n, a_proj), F32),
                   jax.ShapeDtypeStruct((n, b_width), BF16),
                   jax.ShapeDtypeStruct((n, b_width), BF16),
                   jax.ShapeDtypeStruct((n, b_width), BF16),
                   jax.ShapeDtypeStruct((n, c_width), F32)],
        compiler_params=_params("parallel"),
        name="in_proj",
    )(x, g, w, qg, kg)


def _lane_lt_head(shape):
    return lax.broadcasted_iota(jnp.int32, shape, len(shape) - 1) < HEAD


def _head_sum(x):
    out = []
    for ls in range(0, x.shape[1], PAIR):
        xs = x[:, ls:ls + PAIR]
        first = _lane_lt_head(xs.shape)
        sum_a = jnp.sum(jnp.where(first, xs, 0.0), axis=-1, keepdims=True)
        sum_b = jnp.sum(jnp.where(first, 0.0, xs), axis=-1, keepdims=True)
        out.append(jnp.where(first, sum_a, sum_b))
    return jnp.concatenate(out, axis=1)


def _blockdiag(x2):
    first = _lane_lt_head(x2.shape)
    zero = jnp.zeros_like(x2)
    return jnp.concatenate([jnp.where(first, x2, zero), jnp.where(first, zero, x2)], axis=0)


def _rwkv_chunk_maps(insts):
    c = CHUNK
    b16 = lambda t: t.astype(BF16)
    t_idx = lax.broadcasted_iota(jnp.int32, (c, PAIR), 0)
    s_idx = lax.broadcasted_iota(jnp.int32, (c, PAIR), 1) % HEAD
    strict = t_idx > s_idx
    incl = t_idx >= s_idx
    zero = jnp.zeros((c, PAIR), F32)
    eye = jnp.where(t_idx == s_idx, 1.0, 0.0)
    row = lax.broadcasted_iota(jnp.int32, (PAIR, PAIR), 0)
    col = lax.broadcasted_iota(jnp.int32, (PAIR, PAIR), 1)
    same_head = (row < HEAD) == (col < HEAD)
    zero2 = jnp.zeros((PAIR, PAIR), F32)

    s_all = [_dot_nt(b16(jnp.concatenate([qa, qr], axis=0)),
                     b16(jnp.concatenate([_blockdiag(kb), _blockdiag(kk)], axis=0)))
             for qa, qr, kb, kk, _, _, _, _ in insts]
    a_ab = [jnp.where(strict, s[:c, :PAIR], zero) for s in s_all]

    same_block = lambda blk: (t_idx // blk) == (s_idx // blk)
    a0 = [jnp.where(same_block(INV_BLOCK), a, zero) for a in a_ab]
    apow = [_dot(b16(a), b16(_blockdiag(a))) for a in a0]
    prod = [eye + a for a in a0]
    span = 2
    while 2 * span < INV_BLOCK:
        st = [_dot(b16(jnp.concatenate([sq, p], axis=0)), b16(_blockdiag(sq))) for sq, p in zip(apow, prod)]
        prod = [p + s[c:] for p, s in zip(prod, st)]
        apow = [s[:c] for s in st]
        span *= 2
    tinv = [p + _dot(b16(p), b16(_blockdiag(sq))) for p, sq in zip(prod, apow)]
    blk = INV_BLOCK
    while blk < c:
        off_diag = same_block(2 * blk) & jnp.logical_not(same_block(blk))
        x1 = [_dot(b16(jnp.where(off_diag, a, zero)), b16(_blockdiag(t))) for a, t in zip(a_ab, tinv)]
        tinv = [t + _dot(b16(t), b16(_blockdiag(x))) for t, x in zip(tinv, x1)]
        blk *= 2

    a_ak = [jnp.where(strict, s[:c, PAIR:], zero) for s in s_all]
    a_rk = [jnp.where(incl, s[c:, PAIR:], zero) for s in s_all]
    av = [_dot(b16(jnp.concatenate([ak, rk], axis=0)), b16(_blockdiag(inst[6])))
          for ak, rk, inst in zip(a_ak, a_rk, insts)]
    r1 = [_dot(b16(t), b16(jnp.concatenate([_blockdiag(inst[0]), _blockdiag(a[:c])], axis=1)))
          for t, a, inst in zip(tinv, av, insts)]
    a_rb = [jnp.where(incl, s[c:, :PAIR], zero) for s in s_all]
    r2 = [_dot(b16(rb), b16(jnp.concatenate([_blockdiag(r[:, :PAIR]), _blockdiag(r[:, PAIR:])], axis=1)))
          for rb, r in zip(a_rb, r1)]
    out = []
    for r1_i, r2_i, av_i, (qa, qr, kb, kk, kbe, kke, v, wlast_row) in zip(r1, r2, av, insts):
        qa_p, u0 = r1_i[:, :PAIR], r1_i[:, PAIR:]
        m = jnp.where(same_head, _dot_tn(b16(qa_p), b16(kbe)), zero2)
        m = m + jnp.where(row == col, jnp.broadcast_to(wlast_row, (PAIR, PAIR)), zero2)
        n = jnp.where(same_head,
                      _dot_tn(b16(jnp.concatenate([u0, v], axis=0)),
                              b16(jnp.concatenate([kbe, kke], axis=0))), zero2)
        out.append((qr + r2_i[:, :PAIR], r2_i[:, PAIR:] + av_i[c:], m, n))
    return out


def _rwkv_init(*refs):
    prev_ref, s_ref = refs[-3], refs[-2]

    @pl.when(pl.program_id(1) == 0)
    def _():
        prev_ref[...] = jnp.zeros_like(prev_ref)
        s_ref[...] = jnp.zeros_like(s_ref)


def _rwkv_body(*refs, has_vres, tc, width):
    if has_vres:
        (pa_ref, mu_ref, wlh_ref, wll_ref, w0_ref, a0_ref, kk_ref, ka_ref, rk_ref, gng_ref, gnb_ref,
         vf_ref, v0_ref, vdh_ref, vdl_ref, vuh_ref, vul_ref, y_ref, prev_ref, s_ref, yraw_ref) = refs
    else:
        (pa_ref, mu_ref, wlh_ref, wll_ref, w0_ref, a0_ref, kk_ref, ka_ref, rk_ref, gng_ref, gnb_ref,
         y_ref, vf_ref, prev_ref, s_ref, yraw_ref) = refs
    n_pairs = width // PAIR
    n_chunks = tc // CHUNK

    p = pa_ref[...]
    row = lax.broadcasted_iota(jnp.int32, p.shape, 0)
    shifted = jnp.where(row == 0, jnp.broadcast_to(prev_ref[0:1, :], p.shape), pltpu.roll(p, 1, axis=0))
    prev_ref[0:1, :] = p[tc - 1:tc, :]
    p = p + mu_ref[...] * (shifted - p)

    r = p[:, :width]
    k = p[:, width:2 * width]
    v = p[:, 2 * width:3 * width]
    lo_in = p[:, 3 * width:]
    lane = lax.broadcasted_iota(jnp.int32, lo_in.shape, 1)
    lo_act = jnp.where(lane < LORA_W, jnp.tanh(lo_in),
                       jnp.where(lane < LORA_W + LORA_A, lo_in, _sigmoid(lo_in)))
    lo = _dot_x3(lo_act, wlh_ref[...], wll_ref[...])
    wz = w0_ref[...] + lo[:, :width]
    w = -(jnp.maximum(-wz, 0.0) + jnp.log(1.0 + jnp.exp(-jnp.abs(wz)))) - 0.5
    lw = -jnp.exp(w)
    a = _sigmoid(a0_ref[...] + lo[:, width:2 * width])
    g = lo[:, 2 * width:]

    if has_vres:
        v_lo = _dot_x3(v, vdh_ref[...], vdl_ref[...])
        gate = _sigmoid(v0_ref[...] + _dot_x3(v_lo, vuh_ref[...], vul_ref[...]))
        v = v + (vf_ref[...] - v) * gate
    else:
        vf_ref[...] = v

    kkx = k * kk_ref[...]
    kk = kkx / jnp.maximum(jnp.sqrt(_head_sum(kkx * kkx)), 1e-12)
    kmod = k * (1.0 + (a - 1.0) * ka_ref[...])

    grp = 2 * CHUNK
    ti = lax.broadcasted_iota(jnp.int32, (grp, grp), 0)
    tj = lax.broadcasted_iota(jnp.int32, (grp, grp), 1)
    tri = jnp.where(((ti // CHUNK) == (tj // CHUNK)) & (tj <= ti), 1.0, 0.0).astype(BF16)
    lw_hi, lw_lo = _split(lw, 2)
    cum = jnp.concatenate([_dot(tri, lw_hi[r0:r0 + grp]) + _dot(tri, lw_lo[r0:r0 + grp])
                           for r0 in range(0, tc, grp)], axis=0)
    tot = jnp.concatenate(
        [jnp.broadcast_to(cum[(ci + 1) * CHUNK - 1:(ci + 1) * CHUNK, :], (CHUNK, width)) for ci in range(n_chunks)],
        axis=0)
    w_inc = jnp.exp(cum)
    w_exc = jnp.exp(cum - lw)
    w_inv = jnp.exp(-cum)
    w_end = jnp.exp(tot - cum)
    w_tot = jnp.exp(tot)

    kka = kk * a
    qa_all = -kk * w_exc
    qr_all = r * w_inc
    kb_all = kka * w_inv
    kk_all = kmod * w_inv
    kbe_all = kka * w_end
    kke_all = kmod * w_end

    insts = []
    for ci in range(n_chunks):
        rs = slice(ci * CHUNK, (ci + 1) * CHUNK)
        for pi in range(n_pairs):
            ls = slice(pi * PAIR, (pi + 1) * PAIR)
            insts.append((qa_all[rs, ls], qr_all[rs, ls], kb_all[rs, ls], kk_all[rs, ls],
                          kbe_all[rs, ls], kke_all[rs, ls], v[rs, ls], w_tot[ci * CHUNK:ci * CHUNK + 1, ls]))
    maps = _rwkv_chunk_maps(insts)
    states = [s_ref[pi] for pi in range(n_pairs)]
    for ci in range(n_chunks):
        for pi in range(n_pairs):
            qr_p, y0, m, n = maps[ci * n_pairs + pi]
            s16 = states[pi].astype(BF16)
            yraw_ref[ci * CHUNK:(ci + 1) * CHUNK, pi * PAIR:(pi + 1) * PAIR] = _dot_nt(qr_p.astype(BF16), s16) + y0
            states[pi] = _dot(s16, m.astype(BF16)) + n
    for pi in range(n_pairs):
        s_ref[pi] = states[pi]

    y = yraw_ref[...]
    mean = _head_sum(y) * (1.0 / HEAD)
    yc = y - mean
    var = _head_sum(yc * yc) * (1.0 / HEAD)
    yn = yc * lax.rsqrt(var + GN_EPS) * gng_ref[...] + gnb_ref[...]
    bonus = _head_sum(r * kmod * rk_ref[...]) * v
    y_ref[...] = ((yn + bonus) * g).astype(y_ref.dtype)


def _attn_pool_init(q_ref, k_ref, v_ref, bias_ref, u_ref, pw_ref, ps_ref, yb_ref, yc_ref,
                    kpad_ref, vpad_ref, upad_ref):
    width = k_ref.shape[1]

    @pl.when(pl.program_id(1) == 0)
    def _():
        kpad_ref[:PREV_CHUNKS * CHUNK, :] = jnp.zeros((PREV_CHUNKS * CHUNK, width), BF16)
        vpad_ref[:PREV_CHUNKS * CHUNK, :] = jnp.zeros((PREV_CHUNKS * CHUNK, width), BF16)
        kpad_ref[PREV_CHUNKS * CHUNK:, :] = k_ref[...]
        vpad_ref[PREV_CHUNKS * CHUNK:, :] = v_ref[...]
        upad_ref[:CHUNK, :] = jnp.zeros((CHUNK, upad_ref.shape[1]), F32)
        upad_ref[CHUNK:, :] = u_ref[...]


def _attn_pool_body(q_ref, k_ref, v_ref, bias_ref, u_ref, pw_ref, ps_ref, yb_ref, yc_ref,
                    kpad_ref, vpad_ref, upad_ref, *, width, qb, subs):
    rows = qb * CHUNK
    win_rows = (PREV_CHUNKS + qb) * CHUNK
    n_pairs = width // PAIR
    first = _lane_lt_head((rows, PAIR))
    slot = lax.broadcasted_iota(jnp.int32, (1, win_rows), 1)
    for sub in range(subs):
        out_rows = slice(sub * rows, (sub + 1) * rows)
        start = pl.multiple_of((pl.program_id(1) * subs + sub) * rows, rows)
        before_start = jnp.where(slot >= PREV_CHUNKS * CHUNK - start, 0.0, NEG_INF)
        q = q_ref[out_rows, :]
        scores = []
        for pi in range(n_pairs):
            ls = slice(pi * PAIR, (pi + 1) * PAIR)
            q2 = q[:, ls]
            zq = jnp.zeros_like(q2)
            qs = jnp.concatenate([jnp.where(first, q2, zq), jnp.where(first, zq, q2)], axis=0)
            scores.append(_dot_nt(qs, kpad_ref[pl.ds(start, win_rows), ls]) + bias_ref[pi] + before_start)
        probs, sums = [], []
        for s in scores:
            e = jnp.exp2(s - jnp.max(s, axis=-1, keepdims=True))
            sums.append(jnp.sum(e, axis=-1, keepdims=True))
            probs.append(e.astype(BF16))
        for pi in range(n_pairs):
            ls = slice(pi * PAIR, (pi + 1) * PAIR)
            o = _dot(probs[pi], vpad_ref[pl.ds(start, win_rows), ls]) / sums[pi]
            yb_ref[out_rows, ls] = jnp.where(first, o[:rows], o[rows:]).astype(yb_ref.dtype)

        x = upad_ref[pl.ds(start, rows + CHUNK), :]
        cw = x.shape[1]
        acc, span, win_sums = x, 1, []
        for win in POOL_WINDOWS:
            while span < win:
                acc = acc + pltpu.roll(acc, span, axis=0)
                span *= 2
            win_sums.append(acc)
        lane = lax.broadcasted_iota(jnp.int32, (rows, cw), 1)
        t1 = (lax.broadcasted_iota(jnp.int32, (rows, cw), 0) + start + 1).astype(F32)
        pooled = jnp.zeros((rows, cw), F32)
        for gi, win in enumerate(POOL_WINDOWS):
            grp = (lane >= gi * HEAD) & (lane < (gi + 1) * HEAD)
            pooled = jnp.where(grp, win_sums[gi][CHUNK:] / jnp.minimum(t1, float(win)), pooled)
        pooled = pooled - x[CHUNK:]
        yc_ref[out_rows, :] = (_dot(pooled.astype(BF16), pw_ref[...]) * ps_ref[...]).astype(yc_ref.dtype)


def _mixers_kernel(*refs, n_in_a, n_in_bc, n_out_a, n_scr_a, has_vres, tc, width_a, width_b, qb):
    n_out_bc = 2
    in_a = refs[:n_in_a]
    in_bc = refs[n_in_a:n_in_a + n_in_bc]
    outs = refs[n_in_a + n_in_bc:n_in_a + n_in_bc + n_out_a + n_out_bc]
    out_a, out_bc = outs[:n_out_a], outs[n_out_a:]
    scr = refs[n_in_a + n_in_bc + n_out_a + n_out_bc:]
    refs_a = (*in_a, *out_a, *scr[:n_scr_a])
    refs_bc = (*in_bc, *out_bc, *scr[n_scr_a:])
    subs = tc // (qb * CHUNK)
    _rwkv_init(*refs_a)
    _attn_pool_init(*refs_bc)
    _attn_pool_body(*refs_bc, width=width_b, qb=qb, subs=subs)
    _rwkv_body(*refs_a, has_vres=has_vres, tc=tc, width=width_a)


def _mixers(pa, vfirst, prm, q, k, v, bias, u, pw, ps, l, *, tc, qb):
    b, t, a_proj = pa.shape
    width_a = prm["w0"].shape[-1]
    width_b = q.shape[-1]
    cw = u.shape[-1]
    has_vres = vfirst is not None
    tile = lambda w_: pl.BlockSpec((None, tc, w_), lambda i, j: (i, j, 0))
    seq = lambda a: pl.BlockSpec((None,) + a.shape[1:], lambda i, j: (i, 0, 0))
    lay = lambda a: _layer_spec(a, l)
    names = ["mu", "wl_hi", "wl_lo", "w0", "a0", "k_k", "k_a", "r_k", "gn_g", "gn_b"]
    args_a = [pa] + [prm[nm] for nm in names]
    specs_a = [tile(a_proj)] + [lay(prm[nm]) for nm in names]
    out_specs = [tile(width_a)]
    out_shape = [jax.ShapeDtypeStruct((b, t, width_a), BF16)]
    if has_vres:
        extra = [prm["v0"], prm["vd_hi"], prm["vd_lo"], prm["vu_hi"], prm["vu_lo"]]
        args_a += [vfirst] + extra
        specs_a += [tile(width_a)] + [_layer_spec(a, l - 1) for a in extra]
    else:
        out_specs.append(tile(width_a))
        out_shape.append(jax.ShapeDtypeStruct((b, t, width_a), F32))
    n_out_a = len(out_specs)
    args_bc = [q, k, v, bias, u, pw, ps]
    specs_bc = [tile(width_b), seq(k), seq(v), lay(bias), seq(u), lay(pw), lay(ps)]
    out_specs += [tile(width_b), tile(cw)]
    out_shape += [jax.ShapeDtypeStruct((b, t, width_b), BF16), jax.ShapeDtypeStruct((b, t, cw), BF16)]
    scratch_a = [pltpu.VMEM((8, a_proj), F32),
                 pltpu.VMEM((width_a // PAIR, PAIR, PAIR), F32),
                 pltpu.VMEM((tc, width_a), F32)]
    scratch_bc = [pltpu.VMEM((t + PREV_CHUNKS * CHUNK, width_b), BF16),
                  pltpu.VMEM((t + PREV_CHUNKS * CHUNK, width_b), BF16),
                  pltpu.VMEM((t + CHUNK, cw), F32)]
    return pl.pallas_call(
        functools.partial(_mixers_kernel, n_in_a=len(args_a), n_in_bc=len(args_bc), n_out_a=n_out_a,
                          n_scr_a=len(scratch_a), has_vres=has_vres, tc=tc, width_a=width_a, width_b=width_b, qb=qb),
        grid=(b, t // tc),
        in_specs=specs_a + specs_bc,
        out_specs=out_specs,
        out_shape=out_shape,
        scratch_shapes=scratch_a + scratch_bc,
        compiler_params=_params("parallel", "arbitrary"),
        name="mixers",
    )(*args_a, *args_bc)


def _out_cross_kernel(x_ref, ya_ref, yb_ref, yc_ref, wout_ref, g_ref, wq_ref, qg_ref,
                      k_ref, v_ref, wo_ref, o_ref, att_ref, *, heads):
    w16 = lambda ref: ref[...].astype(BF16)
    mix = jnp.concatenate([ya_ref[...], yb_ref[...], yc_ref[...]], axis=1)
    x = x_ref[...] + _dot(mix, w16(wout_ref))
    h = _rms(x, g_ref[...]).astype(BF16)
    q = _dot(h, w16(wq_ref))
    hd = q.shape[1] // heads
    cols = [slice(hi * hd, (hi + 1) * hd) for hi in range(heads)]
    qh = [(_rms(q[:, cs], qg_ref[...]) * (hd ** -0.5 * LOG2_E)).astype(BF16) for cs in cols]
    s = [_dot_nt(qi, k_ref[:, cs]) for qi, cs in zip(qh, cols)]
    e = [jnp.exp2(si - jnp.max(si, axis=-1, keepdims=True)) for si in s]
    for ei, cs in zip(e, cols):
        oh = _dot(ei.astype(BF16), v_ref[:, cs]) / jnp.sum(ei, axis=-1, keepdims=True)
        att_ref[:, cs] = oh.astype(BF16)
    o_ref[...] = x + _dot(att_ref[...], w16(wo_ref))


def _out_cross(x, ya, yb, yc, w_out, g, wq, qg, k, v, wo, l, *, tm, heads):
    b, t, d = x.shape
    lay = lambda a: _layer_spec(a, l)
    tile = lambda a: pl.BlockSpec((None, tm, a.shape[-1]), lambda i, j: (i, j, 0))
    mem = lambda a: pl.BlockSpec((None, None) + a.shape[2:], lambda i, j: (l, i, 0, 0))
    return pl.pallas_call(
        functools.partial(_out_cross_kernel, heads=heads),
        grid=(b, t // tm),
        in_specs=[tile(x), tile(ya), tile(yb), tile(yc), lay(w_out),
                  lay(g), lay(wq), lay(qg), mem(k), mem(v), lay(wo)],
        out_specs=tile(x),
        out_shape=jax.ShapeDtypeStruct((b, t, d), F32),
        scratch_shapes=[pltpu.VMEM((tm, d), BF16)],
        compiler_params=_params("parallel", "parallel"),
        name="out_cross",
    )(x, ya, yb, yc, w_out, g, wq, qg, k, v, wo)


def _mem_kv_kernel(mem_ref, g_ref, w_ref, kg_ref, k_ref, v_ref, *, heads):
    nb, m, d = mem_ref.shape
    h = _rms(mem_ref[...].reshape(nb * m, d), g_ref[...]).astype(BF16)
    kv = _dot(h, w_ref[...].astype(BF16))
    hd = d // heads
    for hi in range(heads):
        cs = slice(hi * hd, (hi + 1) * hd)
        k_ref[:, :, cs] = _rms(kv[:, cs], kg_ref[...]).astype(BF16).reshape(nb, m, hd)
    v_ref[...] = kv[:, d:].astype(BF16).reshape(nb, m, d)


def _mem_kv(mem, g, wkv, kg, *, heads):
    b, m, d = mem.shape
    depth = wkv.shape[0]
    nb = max(n for n in range(1, MEM_KV_ROWS + 1) if b % n == 0)
    per_layer = lambda a: pl.BlockSpec((None,) + a.shape[1:], lambda l, i: (l,) + (0,) * (a.ndim - 1))
    out = pl.BlockSpec((None, nb, m, d), lambda l, i: (l, i, 0, 0))
    return pl.pallas_call(
        functools.partial(_mem_kv_kernel, heads=heads),
        grid=(depth, b // nb),
        in_specs=[pl.BlockSpec((nb, m, d), lambda l, i: (i, 0, 0)), per_layer(g), per_layer(wkv), per_layer(kg)],
        out_specs=[out, out],
        out_shape=[jax.ShapeDtypeStruct((depth, b, m, d), BF16)] * 2,
        compiler_params=_params("parallel", "parallel"),
        name="mem_kv",
    )(mem, g, wkv, kg)


def _split_weight(w):
    hi = w.astype(BF16)
    return hi, (w - hi.astype(F32)).astype(BF16)


def _rel_bias_band(rel_bias):
    heads, n_rel = rel_bias.shape
    period = BAND + CHUNK
    ext = jnp.concatenate([rel_bias, jnp.broadcast_to(rel_bias[:, -1:], (heads, BAND - n_rel))], axis=1)
    v = jnp.concatenate([ext[:, ::-1], jnp.broadcast_to(rel_bias[:, -1:], (heads, CHUNK))], axis=1)
    flat = jnp.tile(v, (1, CHUNK))[:, :CHUNK * (period - 1)]
    return flat.reshape(heads, CHUNK, period - 1)[:, :, :BAND]


def _rel_bias_window(rel_bias, qb):
    band = _rel_bias_band(rel_bias)
    heads = band.shape[0]
    per_chunk = [jnp.pad(band, ((0, 0), (0, 0), (qi * CHUNK, (qb - 1 - qi) * CHUNK)), constant_values=NEG_INF)
                 for qi in range(qb)]
    win = jnp.stack(per_chunk, axis=1)
    return win.reshape(heads // 2, 2 * qb * CHUNK, (PREV_CHUNKS + qb) * CHUNK)


def _lora_weight(w_up, a_up, g_up):
    depth, _, width = w_up.shape
    z = lambda r: jnp.zeros((depth, r, width), F32)
    return jnp.concatenate([
        jnp.concatenate([w_up, z(LORA_W), z(LORA_W)], axis=2),
        jnp.concatenate([z(LORA_A), a_up, z(LORA_A)], axis=2),
        jnp.concatenate([z(LORA_G), z(LORA_G), g_up], axis=2)], axis=1)


def _pool_weight(pool_w):
    depth, groups, cg, _ = pool_w.shape
    rows = []
    for gi in range(groups):
        blocks = [pool_w[:, gi] if gj == gi else jnp.zeros((depth, cg, cg), pool_w.dtype) for gj in range(groups)]
        rows.append(jnp.concatenate(blocks, axis=2))
    return jnp.concatenate(rows, axis=1)


def _stack_rows(a):
    return a.reshape(a.shape[0], 1, -1)


def _attn_pool_params(rel_bias, pool_w, pool_scale, qb):
    return {"bias": jax.vmap(functools.partial(_rel_bias_window, qb=qb))(rel_bias * LOG2_E),
            "pw": _pool_weight(pool_w).astype(BF16), "ps": _stack_rows(pool_scale)}


def _rwkv_params(a_mu, a_w0, a_w_up, a_a0, a_a_up, a_g_up, a_k_k, a_k_a, a_r_k, a_gn_g, a_gn_b,
                 a_v0, a_v_down, a_v_up):
    wl_hi, wl_lo = _split_weight(_lora_weight(a_w_up, a_a_up, a_g_up))
    pad_lanes = LANES - LORA_V
    vd_hi, vd_lo = _split_weight(jnp.pad(a_v_down, ((0, 0), (0, 0), (0, pad_lanes))))
    vu_hi, vu_lo = _split_weight(jnp.pad(a_v_up, ((0, 0), (0, pad_lanes), (0, 0))))
    return {"mu": _stack_rows(a_mu), "wl_hi": wl_hi, "wl_lo": wl_lo, "w0": _stack_rows(a_w0),
            "a0": _stack_rows(a_a0), "k_k": _stack_rows(a_k_k), "k_a": _stack_rows(a_k_a),
            "r_k": _stack_rows(a_r_k), "gn_g": _stack_rows(a_gn_g), "gn_b": _stack_rows(a_gn_b),
            "v0": _stack_rows(a_v0),
            "vd_hi": vd_hi, "vd_lo": vd_lo, "vu_hi": vu_hi, "vu_lo": vu_lo}


def kernel(x, mem, norm_ffn1, ffn1_wi, ffn1_wo, norm_mix, w_in, w_out, a_mu, a_w0, a_w_up, a_a0, a_a_up, a_g_up,
           a_k_k, a_k_a, a_r_k, a_gn_g, a_gn_b, a_v0, a_v_down, a_v_up, b_q_gain, b_k_gain, b_rel_bias,
           c_pool_w, c_pool_scale, norm_cross, norm_mem, x_wq, x_wkv, x_wo, x_q_gain, x_k_gain,
           norm_ffn2, ffn2_wi, ffn2_wo):
    b, t, d = x.shape
    depth = w_in.shape[0]
    a_proj = a_mu.shape[-1]
    b_width = b_rel_bias.shape[1] * HEAD
    c_width = c_pool_scale.shape[-1]
    x_heads = d // x_q_gain.shape[-1]
    n_tok = b * t

    ffn1 = (_stack_rows(norm_ffn1), ffn1_wi, ffn1_wo)
    ffn2 = (_stack_rows(norm_ffn2), ffn2_wi, ffn2_wo)
    q_gain = _stack_rows(jnp.tile(b_q_gain, (1, b_width // HEAD))) * (HEAD ** -0.5 * LOG2_E)
    k_gain = _stack_rows(jnp.tile(b_k_gain, (1, b_width // HEAD)))
    rwkv_prm = _rwkv_params(a_mu, a_w0, a_w_up, a_a0, a_a_up, a_g_up, a_k_k, a_k_a, a_r_k, a_gn_g, a_gn_b,
                            a_v0, a_v_down, a_v_up)
    ap_prm = _attn_pool_params(b_rel_bias, c_pool_w, c_pool_scale, ATTN_QB)
    mem_k, mem_v = _mem_kv(mem, _stack_rows(norm_mem), x_wkv, _stack_rows(x_k_gain), heads=x_heads)

    seq = lambda a: a.reshape(b, t, a.shape[-1])
    x = x.reshape(n_tok, d)
    v_first = None
    for l in range(depth):
        x = _ffn(x, *ffn1, l, tm=FFN_TM, tf=FFN_TF)
        pa, q, k, v, u = _inproj(x, _stack_rows(norm_mix), w_in, q_gain, k_gain, l,
                                 tm=ROW_TM, a_proj=a_proj, b_width=b_width, c_width=c_width)
        mixed = _mixers(seq(pa), v_first, rwkv_prm, seq(q), seq(k), seq(v), ap_prm["bias"], seq(u),
                        ap_prm["pw"], ap_prm["ps"], l, tc=MIX_TC, qb=ATTN_QB)
        if l == 0:
            y_a, v_first, y_b, y_c = mixed
        else:
            y_a, y_b, y_c = mixed
        x = _out_cross(seq(x), y_a, y_b, y_c, w_out, _stack_rows(norm_cross), x_wq, _stack_rows(x_q_gain),
                       mem_k, mem_v, x_wo, l, tm=ROW_TM, heads=x_heads)
        x = _ffn(x.reshape(n_tok, d), *ffn2, l, tm=FFN_TM, tf=FFN_TF)
    return x.reshape(b, t, d)
```
